```python
import math
import jax
import jax.numpy as jnp
from jax import lax
import numpy as np

D_MODEL = 1024
BATCH = 2
SEQ = 16384
DEPTH = 2

GRID_W = 64
CTX_LEN = 256
D_CONV = 512
CONV_K = 31
D_SSD = 512
SSD_HEADDIM = 64
SSD_HEADS = D_SSD // SSD_HEADDIM
SSD_GROUPS = 2
SSD_STATE = 128
SSD_CONV_K = 5
SSD_CHUNK = 128
D_BC = SSD_GROUPS * SSD_STATE
D_XBC = D_SSD + 2 * D_BC
D_MIX = D_CONV + D_SSD
D_IN = 2 * D_CONV + D_SSD + D_XBC + 2 * SSD_HEADS
MOE_GROUPS = 4
MOE_PER_GROUP = 4
N_EXPERTS = MOE_GROUPS * MOE_PER_GROUP
MOE_TOPK = 2
EXPERT_FF = 256
RMS_EPS = 1e-6
LN_EPS = 1e-5

kernel_name = "hymba_conformer_ssd_hmoe_prefix_dit"


def rms_norm(x, g):
    xf = x.astype(jnp.float32)
    y = xf * lax.rsqrt(jnp.mean(jnp.square(xf), axis=-1, keepdims=True) + RMS_EPS)
    return (y * g.astype(jnp.float32)).astype(x.dtype)


def layer_norm(x, g, b):
    xf = x.astype(jnp.float32)
    mu = jnp.mean(xf, axis=-1, keepdims=True)
    var = jnp.mean(jnp.square(xf - mu), axis=-1, keepdims=True)
    y = (xf - mu) * lax.rsqrt(var + LN_EPS)
    return (y * g.astype(jnp.float32) + b.astype(jnp.float32)).astype(x.dtype)


def modulate(h, shift, scale):
    return h * (1.0 + scale) + shift


def ada_mod(cvec, w_ada, b_ada, n):
    m = jax.nn.silu(cvec) @ w_ada[:, :n * D_MODEL] + b_ada[:n * D_MODEL]
    return jnp.split(m, n, axis=-1)


def dwconv_seq(v, w):
    k, ch = w.shape
    return lax.conv_general_dilated(
        v, w[:, None, :].astype(v.dtype), (1,), [(k // 2, k // 2)],
        dimension_numbers=("NWC", "WIO", "NWC"), feature_group_count=ch)


def dwconv_grid(v, w, axis):
    k, ch = w.shape
    if axis == 1:
        taps, pad = w[:, None, None, :], [(k // 2, k // 2), (0, 0)]
    else:
        taps, pad = w[None, :, None, :], [(0, 0), (k // 2, k // 2)]
    return lax.conv_general_dilated(
        v, taps.astype(v.dtype), (1, 1), pad,
        dimension_numbers=("NHWC", "HWIO", "NHWC"), feature_group_count=ch)


def split_proj(proj):
    o1, o2, o3, o4 = D_CONV, 2 * D_CONV, 2 * D_CONV + D_SSD, 2 * D_CONV + D_SSD + D_XBC
    return proj[..., :o1], proj[..., o1:o2], proj[..., o2:o3], proj[..., o3:o4], proj[..., o4:]


def conformer_conv(u, gate, p, rows):
    v = u * jax.nn.sigmoid(gate)
    if rows is None:
        y = dwconv_seq(v, p["conv_w"])
    else:
        b, l, ch = v.shape
        half = ch // 2
        vg = v.reshape(b, rows, GRID_W, ch)
        y = jnp.concatenate([dwconv_grid(vg[..., :half], p["conv_w"][:, :half], 2),
                             dwconv_grid(vg[..., half:], p["conv_w"][:, half:], 1)],
                            axis=-1).reshape(b, l, ch)
    return jax.nn.silu(layer_norm(y + p["conv_b"], p["conv_ln_g"], p["conv_ln_b"]))


def group_to_heads(v):
    b, l, _ = v.shape
    return jnp.repeat(v.reshape(b, l, SSD_GROUPS, SSD_STATE), SSD_HEADS // SSD_GROUPS, axis=2)


def ssd_scan_inputs(xbc_raw, dt_raw, p):
    b, l, ch = xbc_raw.shape
    xbc = jax.nn.silu(dwconv_seq(xbc_raw, p["ssd_conv_w"][:, :ch]) + p["ssd_conv_b"][:ch])
    xs = xbc[..., :D_SSD].reshape(b, l, SSD_HEADS, SSD_HEADDIM)
    bm = group_to_heads(xbc[..., D_SSD:D_SSD + D_BC])
    cm = group_to_heads(xbc[..., D_SSD + D_BC:]) if ch == D_XBC else None
    dt = jax.nn.softplus(dt_raw.astype(jnp.float32).reshape(b, l, 2, SSD_HEADS)
                         + p["dt_bias"].astype(jnp.float32))
    return xs, bm, cm, dt


def flip_seq(t):
    return jnp.flip(t, axis=1)


def ssd_chunked(xs, dt, a, bm, cm, h0):
    b, l, nh, hp = xs.shape
    n = bm.shape[-1]
    nc = l // SSD_CHUNK
    a_cum = jnp.cumsum((dt * a).reshape(b, nc, SSD_CHUNK, nh), axis=2)
    xdt = (xs * dt[..., None]).reshape(b, nc, SSD_CHUNK, nh, hp)
    bc = bm.reshape(b, nc, SSD_CHUNK, nh, n)
    cc = cm.reshape(b, nc, SSD_CHUNK, nh, n)
    causal = jnp.tril(jnp.ones((SSD_CHUNK, SSD_CHUNK), dtype=bool))
    seg = a_cum[:, :, :, None, :] - a_cum[:, :, None, :, :]
    decay_ls = jnp.exp(jnp.where(causal[None, None, :, :, None], seg, -jnp.inf))
    scores = jnp.einsum("bclhn,bcshn->bclsh", cc, bc) * decay_ls
    y_diag = jnp.einsum("bclsh,bcshp->bclhp", scores, xdt)
    decay_to_end = jnp.exp(a_cum[:, :, -1:, :] - a_cum)
    chunk_states = jnp.einsum("bclh,bclhn,bclhp->bchpn", decay_to_end, bc, xdt)
    chunk_decay = jnp.exp(a_cum[:, :, -1, :])

    def step(h, inp):
        s, d = inp
        return h * d[:, :, None, None] + s, h

    _, h_prev = lax.scan(step, h0.astype(chunk_states.dtype),
                         (jnp.swapaxes(chunk_states, 0, 1), jnp.swapaxes(chunk_decay, 0, 1)))
    h_prev = jnp.swapaxes(h_prev, 0, 1)
    y_off = jnp.einsum("bclhn,bchpn,bclh->bclhp", cc, h_prev, jnp.exp(a_cum))
    return (y_diag + y_off).reshape(b, l, nh, hp)


def ssd_final_state(xs, dt, a, bm):
    la = jnp.cumsum(dt * a, axis=1)
    w = jnp.exp(la[:, -1:, :] - la) * dt
    return jnp.einsum("blh,blhn,blhp->bhpn", w, bm, xs)


def context_states(xs, bm, dt, a):
    hf = ssd_final_state(xs, dt[:, :, 0], a[0], bm)
    hb = ssd_final_state(flip_seq(xs), flip_seq(dt[:, :, 1]), a[1], flip_seq(bm))
    return hf, hb


def mixer_output(u, gate, z, xs, bm, cm, dt, a, h0_f, h0_b, p, rows):
    conv_out = conformer_conv(u, gate, p, rows)
    y_f = ssd_chunked(xs, dt[:, :, 0], a[0], bm, cm, h0_f)
    y_b = flip_seq(ssd_chunked(flip_seq(xs), flip_seq(dt[:, :, 1]), a[1],
                               flip_seq(bm), flip_seq(cm), h0_b))
    y = y_f + y_b + p["d_skip"][:, None] * xs
    b, l = xs.shape[:2]
    ssd_out = rms_norm(y.reshape(b, l, D_SSD) * jax.nn.silu(z), p["ssd_norm_g"])
    return jnp.concatenate([conv_out, ssd_out], axis=-1) @ p["w_out"]


def hier_moe(h, p):
    b, l, d = h.shape
    t = h.reshape(b * l, d)
    g_logits = (t @ p["w_rg"] + p["b_rg"]).astype(jnp.float32)
    g_sel = jnp.argmax(g_logits, axis=-1)
    g_prob = jnp.take_along_axis(jax.nn.softmax(g_logits, axis=-1), g_sel[:, None], axis=-1)
    e_logits = (t @ p["w_re"] + p["b_re"]).astype(jnp.float32).reshape(-1, MOE_GROUPS, MOE_PER_GROUP)
    e_logits = jnp.take_along_axis(e_logits, g_sel[:, None, None], axis=1)[:, 0]
    top_v, top_i = lax.top_k(jax.nn.softmax(e_logits, axis=-1), MOE_TOPK)
    w_sel = g_prob * top_v / jnp.sum(top_v, axis=-1, keepdims=True)
    e_id = g_sel[:, None] * MOE_PER_GROUP + top_i
    comb = jnp.einsum("tk,tke->et", w_sel, jax.nn.one_hot(e_id, N_EXPERTS, dtype=jnp.float32))

    def expert(acc, ew):
        wg, wu, wd, cw = ew
        hid = jax.nn.silu(t @ wg) * (t @ wu)
        return acc + (cw[:, None] * (hid @ wd)).astype(acc.dtype), None

    out, _ = lax.scan(expert, jnp.zeros(t.shape, jnp.float32),
                      (p["w_gate"], p["w_up"], p["w_down"], comb))
    return out.reshape(b, l, d)


def setup_inputs(seed: int = 0) -> dict:
    key = jax.random.key(seed)
    ks = iter(jax.random.split(key, 40))
    f32 = jnp.float32

    def nrm(shape, scale):
        return scale * jax.random.normal(next(ks), shape, f32)

    def gain(shape):
        return 1.0 + nrm(shape, 0.02)

    dt0 = jnp.exp(jax.random.uniform(next(ks), (DEPTH, 2, SSD_HEADS), f32,
                                     math.log(1e-3), math.log(1e-1)))
    return {
        "x": nrm((BATCH, SEQ, D_MODEL), 1.0),
        "c": nrm((BATCH, D_MODEL), 1.0),
        "ctx": nrm((BATCH, CTX_LEN, D_MODEL), 1.0),
        "c_ctx": nrm((D_MODEL,), 1.0),
        "w_ada": nrm((DEPTH, D_MODEL, 6 * D_MODEL), 0.5 * D_MODEL ** -0.5),
        "b_ada": nrm((DEPTH, 6 * D_MODEL), 0.02),
        "g_mix": gain((DEPTH, D_MODEL)),
        "g_ffn": gain((DEPTH, D_MODEL)),
        "w_in": nrm((DEPTH, D_MODEL, D_IN), D_MODEL ** -0.5),
        "conv_w": nrm((DEPTH, CONV_K, D_CONV), CONV_K ** -0.5),
        "conv_b": nrm((DEPTH, D_CONV), 0.02),
        "conv_ln_g": gain((DEPTH, D_CONV)),
        "conv_ln_b": nrm((DEPTH, D_CONV), 0.02),
        "ssd_conv_w": nrm((DEPTH, SSD_CONV_K, D_XBC), SSD_CONV_K ** -0.5),
        "ssd_conv_b": nrm((DEPTH, D_XBC), 0.02),
        "dt_bias": dt0 + jnp.log(-jnp.expm1(-dt0)),
        "a_log": jnp.log(jax.random.uniform(next(ks), (DEPTH, 2, SSD_HEADS), f32, 1.0, 16.0)),
        "d_skip": gain((DEPTH, SSD_HEADS)),
        "ssd_norm_g": gain((DEPTH, D_SSD)),
        "w_out": nrm((DEPTH, D_MIX, D_MODEL), D_MIX ** -0.5),
        "w_router_group": nrm((DEPTH, D_MODEL, MOE_GROUPS), D_MODEL ** -0.5),
        "b_router_group": nrm((DEPTH, MOE_GROUPS), 0.01),
        "w_router_expert": nrm((DEPTH, D_MODEL, N_EXPERTS), D_MODEL ** -0.5),
        "b_router_expert": nrm((DEPTH, N_EXPERTS), 0.01),
        "w_gate": nrm((DEPTH, N_EXPERTS, D_MODEL, EXPERT_FF), D_MODEL ** -0.5),
        "w_up": nrm((DEPTH, N_EXPERTS, D_MODEL, EXPERT_FF), D_MODEL ** -0.5),
        "w_down": nrm((DEPTH, N_EXPERTS, EXPERT_FF, D_MODEL), EXPERT_FF ** -0.5),
        "g_final": gain((D_MODEL,)),
    }


def reference(x, c, ctx, c_ctx, w_ada, b_ada, g_mix, g_ffn, w_in, conv_w, conv_b, conv_ln_g,
              conv_ln_b, ssd_conv_w, ssd_conv_b, dt_bias, a_log, d_skip, ssd_norm_g, w_out,
              w_router_group, b_router_group, w_router_expert, b_router_expert,
              w_gate, w_up, w_down, g_final):
    rows = x.shape[1] // GRID_W
    h_x, h_c = x, ctx
    lo = 2 * D_CONV + D_SSD
    for l in range(DEPTH):
        last = l == DEPTH - 1
        p = {"w_in": w_in[l], "conv_w": conv_w[l], "conv_b": conv_b[l], "conv_ln_g": conv_ln_g[l],
             "conv_ln_b": conv_ln_b[l], "ssd_conv_w": ssd_conv_w[l], "ssd_conv_b": ssd_conv_b[l],
             "dt_bias": dt_bias[l], "d_skip": d_skip[l], "ssd_norm_g": ssd_norm_g[l],
             "w_out": w_out[l], "w_rg": w_router_group[l], "b_rg": b_router_group[l],
             "w_re": w_router_expert[l], "b_re": b_router_expert[l],
             "w_gate": w_gate[l], "w_up": w_up[l], "w_down": w_down[l]}
        a = -jnp.exp(a_log[l].astype(jnp.float32))
        sh1, sc1, gt1, sh2, sc2, gt2 = [m[:, None, :] for m in ada_mod(c, w_ada[l], b_ada[l], 6)]

        if last:
            csh1, csc1 = ada_mod(c_ctx, w_ada[l], b_ada[l], 2)
            hc = modulate(rms_norm(h_c, g_mix[l]), csh1, csc1)
            w_xb_dt = jnp.concatenate([p["w_in"][:, lo:lo + D_SSD + D_BC], p["w_in"][:, lo + D_XBC:]], axis=1)
            proj_c = hc @ w_xb_dt
            xs_c, bm_c, _, dt_c = ssd_scan_inputs(proj_c[..., :D_SSD + D_BC], proj_c[..., D_SSD + D_BC:], p)
        else:
            csh1, csc1, cgt1, csh2, csc2, cgt2 = ada_mod(c_ctx, w_ada[l], b_ada[l], 6)
            hc = modulate(rms_norm(h_c, g_mix[l]), csh1, csc1)
            uc, gc, zc, xbc_c, dtr_c = split_proj(hc @ p["w_in"])
            xs_c, bm_c, cm_c, dt_c = ssd_scan_inputs(xbc_c, dtr_c, p)
        hf0, hb0 = context_states(xs_c, bm_c, dt_c, a)

        hx = modulate(rms_norm(h_x, g_mix[l]), sh1, sc1)
        u, gate, z, xbc, dt_raw = split_proj(hx @ p["w_in"])
        xs, bm, cm, dt = ssd_scan_inputs(xbc, dt_raw, p)
        h_x = h_x + gt1 * mixer_output(u, gate, z, xs, bm, cm, dt, a, hf0, hb0, p, rows)
        h_x = h_x + gt2 * hier_moe(modulate(rms_norm(h_x, g_ffn[l]), sh2, sc2), p)

        if not last:
            zero = jnp.zeros_like(hf0)
            h_c = h_c + cgt1 * mixer_output(uc, gc, zc, xs_c, bm_c, cm_c, dt_c, a, zero, zero, p, None)
            h_c = h_c + cgt2 * hier_moe(modulate(rms_norm(h_c, g_ffn[l]), csh2, csc2), p)

    return rms_norm(h_x, g_final).astype(x.dtype)
```

```python
import functools

import jax
import jax.numpy as jnp
from jax import lax
from jax.experimental import pallas as pl
from jax.experimental.pallas import tpu as pltpu

F32 = jnp.float32
BF16 = jnp.bfloat16

RMS_EPS = 1e-6
LN_EPS = 1e-5

GRID_W = 64
D_CONV = 512
D_SSD = 512
SSD_HEADS = 8
SSD_HEADDIM = 64
SSD_GROUPS = 2
SSD_STATE = 128
D_BC = SSD_GROUPS * SSD_STATE
D_XBC = D_SSD + 2 * D_BC
MOE_GROUPS = 4
MOE_PER_GROUP = 4
N_EXPERTS = MOE_GROUPS * MOE_PER_GROUP

LANES = 128
SUBLANES = 8
CONV_PAD = 16
SSD_CHUNK = 128
ROUTER_LANE0 = MOE_GROUPS
VMEM_LIMIT = 56 * 1024 * 1024


def _cparams(*sem):
    return pltpu.CompilerParams(dimension_semantics=sem, vmem_limit_bytes=VMEM_LIMIT)


def _dot(a, b):
    return jnp.dot(a, b, preferred_element_type=F32)


def _silu(x):
    return x * jax.nn.sigmoid(x)


def _split_bf16(x):
    hi = x.astype(BF16)
    lo = (x - hi.astype(F32)).astype(BF16)
    return hi, lo


def _ada_kernel(c_ref, w_ref, b_ref, o_ref):
    c = c_ref[...]
    o_ref[...] = jnp.dot(_silu(c), w_ref[...], preferred_element_type=F32,
                         precision=lax.Precision.HIGHEST) + b_ref[...]


def _ada(cvecs, w_ada, b_ada):
    d, n = w_ada.shape
    bn = 1536
    return pl.pallas_call(
        _ada_kernel,
        out_shape=jax.ShapeDtypeStruct((cvecs.shape[0], n), F32),
        grid=(n // bn,),
        in_specs=[pl.BlockSpec((cvecs.shape[0], d), lambda j: (0, 0)),
                  pl.BlockSpec((d, bn), lambda j: (0, j)),
                  pl.BlockSpec((1, bn), lambda j: (0, j))],
        out_specs=pl.BlockSpec((cvecs.shape[0], bn), lambda j: (0, j)),
        compiler_params=_cparams("arbitrary"),
        name="ada",
    )(cvecs, w_ada, b_ada.reshape(1, n))


def _in_kernel(x_ref, sh_ref, sc_ref, g_ref, w_ref, wdt_ref, v_ref, z_ref, xbc_ref, dt_ref):
    x = x_ref[...]
    ms = jnp.mean(x * x, axis=-1, keepdims=True)
    h = x * lax.rsqrt(ms + RMS_EPS) * g_ref[...]
    h = (h * (1.0 + sc_ref[0]) + sh_ref[0]).astype(BF16)
    u = _dot(h, w_ref[:, 0:D_CONV])
    gate = _dot(h, w_ref[:, D_CONV:2 * D_CONV])
    v_ref[...] = u * jax.nn.sigmoid(gate)
    z_ref[...] = _dot(h, w_ref[:, 2 * D_CONV:2 * D_CONV + D_SSD])
    xbc_ref[...] = _dot(h, w_ref[:, 2 * D_CONV + D_SSD:])
    dt_ref[...] = _dot(h, wdt_ref[...])


def _in_proj(x, shift, scale, g, w_main, w_dt, seq, tm):
    t, d = x.shape
    tpb = seq // tm
    n_main = w_main.shape[1]
    tok = lambda i: (i, 0)
    per_batch = lambda i: (i // tpb, 0, 0)
    const = lambda i: (0, 0)
    return pl.pallas_call(
        _in_kernel,
        out_shape=(jax.ShapeDtypeStruct((t, D_CONV), F32), jax.ShapeDtypeStruct((t, D_SSD), F32),
                   jax.ShapeDtypeStruct((t, D_XBC), F32), jax.ShapeDtypeStruct((t, LANES), F32)),
        grid=(t // tm,),
        in_specs=[pl.BlockSpec((tm, d), tok),
                  pl.BlockSpec((1, 1, d), per_batch), pl.BlockSpec((1, 1, d), per_batch),
                  pl.BlockSpec((1, d), const),
                  pl.BlockSpec((d, n_main), const), pl.BlockSpec((d, LANES), const)],
        out_specs=(pl.BlockSpec((tm, D_CONV), tok), pl.BlockSpec((tm, D_SSD), tok),
                   pl.BlockSpec((tm, D_XBC), tok), pl.BlockSpec((tm, LANES), tok)),
        compiler_params=_cparams("arbitrary"),
        name="in_proj",
    )(x, shift, scale, g, w_main, w_dt)


def _xbc_kernel(prev_ref, cur_ref, next_ref, w_ref, b_ref, dtr_ref, dtb_ref, o_ref, dto_ref, scr,
                *, tm, tpb, k):
    i = pl.program_id(0)
    first = (i % tpb) == 0
    last = (i % tpb) == tpb - 1
    scr[0:SUBLANES, :] = jnp.where(first, 0.0, prev_ref[...])
    scr[SUBLANES:SUBLANES + tm, :] = cur_ref[...]
    scr[SUBLANES + tm:2 * SUBLANES + tm, :] = jnp.where(last, 0.0, next_ref[...])
    acc = jnp.broadcast_to(b_ref[...], cur_ref.shape)
    for j in range(k):
        acc = acc + w_ref[j:j + 1, :] * scr[pl.ds(SUBLANES + j - k // 2, tm), :]
    o_ref[...] = _silu(acc)
    xdt = dtr_ref[...] + dtb_ref[...]
    dto_ref[...] = jnp.maximum(xdt, 0.0) + jnp.log1p(jnp.exp(-jnp.abs(xdt)))


def _xbc_conv(xbc, w, b, dt_raw, dt_bias, seq, tm):
    t, c = xbc.shape
    k = w.shape[0]
    tpb = seq // tm
    nb8 = tm // SUBLANES
    last8 = t // SUBLANES - 1
    w_pad = jnp.zeros((SUBLANES, c), F32).at[:k].set(w)
    tok = lambda i: (i, 0)
    const = lambda i: (0, 0)
    return pl.pallas_call(
        functools.partial(_xbc_kernel, tm=tm, tpb=tpb, k=k),
        out_shape=(jax.ShapeDtypeStruct((t, c), F32), jax.ShapeDtypeStruct((t, LANES), F32)),
        grid=(t // tm,),
        in_specs=[pl.BlockSpec((SUBLANES, c), lambda i: (jnp.maximum(i * nb8 - 1, 0), 0)),
                  pl.BlockSpec((tm, c), tok),
                  pl.BlockSpec((SUBLANES, c), lambda i: (jnp.minimum((i + 1) * nb8, last8), 0)),
                  pl.BlockSpec((SUBLANES, c), const), pl.BlockSpec((1, c), const),
                  pl.BlockSpec((tm, LANES), tok), pl.BlockSpec((1, LANES), const)],
        out_specs=(pl.BlockSpec((tm, c), tok), pl.BlockSpec((tm, LANES), tok)),
        scratch_shapes=[pltpu.VMEM((tm + 2 * SUBLANES, c), F32)],
        compiler_params=_cparams("arbitrary"),
        name="xbc_conv",
    )(xbc, xbc, xbc, w_pad, b.reshape(1, c), dt_raw, dt_bias)


def _segconv_kernel(v_ref, w_ref, b_ref, o_ref, scr, *, seg, nseg, k, cb):
    zeros = jnp.zeros((CONV_PAD, cb), F32)
    scr[0:CONV_PAD, :] = zeros
    scr[CONV_PAD + seg:2 * CONV_PAD + seg, :] = zeros

    def body(s, carry):
        base = pl.multiple_of(s * seg, seg)
        scr[CONV_PAD:CONV_PAD + seg, :] = v_ref[pl.ds(base, seg), :]
        for c in range(cb // LANES):
            cs = slice(c * LANES, (c + 1) * LANES)
            acc = jnp.broadcast_to(b_ref[:, cs], (seg, LANES))
            for j in range(k):
                acc = acc + w_ref[j:j + 1, cs] * scr[pl.ds(CONV_PAD + j - k // 2, seg), cs]
            o_ref[pl.ds(base, seg), cs] = acc
        return carry

    lax.fori_loop(0, nseg, body, 0)


def _seg_conv(v, w_pad, b, seg, cb, tm):
    t = v.shape[0]
    k = 31
    tok = lambda i: (i, 0)
    const = lambda i: (0, 0)
    return pl.pallas_call(
        functools.partial(_segconv_kernel, seg=seg, nseg=tm // seg, k=k, cb=cb),
        out_shape=jax.ShapeDtypeStruct((t, cb), F32),
        grid=(t // tm,),
        in_specs=[pl.BlockSpec((tm, cb), tok), pl.BlockSpec((w_pad.shape[0], cb), const),
                  pl.BlockSpec((1, cb), const)],
        out_specs=pl.BlockSpec((tm, cb), tok),
        scratch_shapes=[pltpu.VMEM((seg + 2 * CONV_PAD, cb), F32)],
        compiler_params=_cparams("arbitrary"),
        name="seg_conv",
    )(v, w_pad, b)


def _rowconv_kernel(v_ref, w_ref, b_ref, o_ref, scr, *, width, rows, k):
    half = k // 2
    pad = jnp.zeros((half * width, LANES), F32)
    scr[0:half * width, :] = pad
    scr[(half + rows) * width:(2 * half + rows) * width, :] = pad
    scr[half * width:(half + rows) * width, :] = v_ref[0]

    def body(r, carry):
        acc = jnp.broadcast_to(b_ref[...], (width, LANES))
        for j in range(k):
            start = pl.multiple_of((r + j) * width, width)
            acc = acc + w_ref[j:j + 1, :] * scr[pl.ds(start, width), :]
        o_ref[0, pl.ds(pl.multiple_of(r * width, width), width), :] = acc
        return carry

    lax.fori_loop(0, rows, body, 0)


def _row_conv(v3, w_pad, b, lane_block0, nblk):
    bsz, seq, _ = v3.shape
    rows = seq // GRID_W
    k = 31
    return pl.pallas_call(
        functools.partial(_rowconv_kernel, width=GRID_W, rows=rows, k=k),
        out_shape=jax.ShapeDtypeStruct((bsz, seq, nblk * LANES), F32),
        grid=(bsz, nblk),
        in_specs=[pl.BlockSpec((1, seq, LANES), lambda bi, c: (bi, 0, lane_block0 + c)),
                  pl.BlockSpec((w_pad.shape[0], LANES), lambda bi, c: (0, lane_block0 + c)),
                  pl.BlockSpec((1, LANES), lambda bi, c: (0, lane_block0 + c))],
        out_specs=pl.BlockSpec((1, seq, LANES), lambda bi, c: (bi, 0, c)),
        scratch_shapes=[pltpu.VMEM(((rows + 2 * (k // 2)) * GRID_W, LANES), F32)],
        compiler_params=_cparams("arbitrary", "arbitrary"),
        name="row_conv",
    )(v3, w_pad, b)


def _ssd_kernel(xf_ref, xb_ref, dtf_ref, dtb_ref, a_ref, dskip_ref, h0_ref, tril_ref, triu_ref, e_ref,
                yf_ref, yb_ref, hfin_ref, s_ref, *, chunk, nchunks):
    i = pl.program_id(1)

    @pl.when(i == 0)
    def _():
        s_ref[...] = h0_ref[0]

    L = chunk
    hpg = SSD_HEADS // SSD_GROUPS
    gw = hpg * SSD_HEADDIM
    lane = lax.broadcasted_iota(jnp.int32, (L, LANES), 1)
    row_i = lax.broadcasted_iota(jnp.int32, (L, L), 0)
    col_i = lax.broadcasted_iota(jnp.int32, (L, L), 1)
    head_of_lane = lax.broadcasted_iota(jnp.int32, (L, gw), 1) // SSD_HEADDIM

    def direction(d, x_ref, dt_ref, y_ref):
        xall = x_ref[...]
        x = xall[:, :D_SSD]
        bm = xall[:, D_SSD:D_SSD + D_BC]
        cm = xall[:, D_SSD + D_BC:]
        dt = dt_ref[...]
        dmask = (lane >= SSD_HEADS * d) & (lane < SSD_HEADS * (d + 1))
        da = jnp.where(dmask, dt * a_ref[...], 0.0)
        tri = tril_ref[...] if d == 0 else triu_ref[...]
        da_hi, da_lo = _split_bf16(da)
        cum = _dot(tri, da_hi) + _dot(tri, da_lo)
        tot = cum[L - 1:L, :] if d == 0 else cum[0:1, :]
        e_cum = jnp.where(dmask, jnp.exp(cum), 0.0)
        w_dt = jnp.where(dmask, jnp.exp(tot - cum) * dt, 0.0)
        e_tot = jnp.where(dmask[0:1], jnp.exp(tot), 0.0)
        stack = jnp.concatenate([e_cum, w_dt, jnp.broadcast_to(e_tot, (SUBLANES, LANES))], axis=0)
        s_hi, s_lo = _split_bf16(stack)
        ex = _dot(s_hi, e_ref[...]) + _dot(s_lo, e_ref[...])
        e_cum_x = ex[0:L]
        w_dt_x = ex[L:2 * L]
        e_tot_x = ex[2 * L:2 * L + 1]
        cum_t = cum.T
        dt_t = dt.T
        xw = (x * w_dt_x).astype(BF16)
        bm_t = bm.T
        mask = (row_i >= col_i) if d == 0 else (row_i <= col_i)
        ys = []
        for g in range(SSD_GROUPS):
            cg = cm[:, g * SSD_STATE:(g + 1) * SSD_STATE].astype(BF16)
            bg = bm[:, g * SSD_STATE:(g + 1) * SSD_STATE].astype(BF16)
            cb = lax.dot_general(cg, bg, (((1,), (1,)), ((), ())), preferred_element_type=F32)
            xg = x[:, g * gw:(g + 1) * gw]
            gs, xs = [], []
            for hh in range(hpg):
                c = SSD_HEADS * d + hpg * g + hh
                seg = cum[:, c:c + 1] - cum_t[c:c + 1, :]
                gm = jnp.exp(jnp.where(mask, seg, -jnp.inf)) * cb * dt_t[c:c + 1, :]
                gs.append(gm.astype(BF16))
                xs.append(jnp.where(head_of_lane == hh, xg, 0.0).astype(BF16))
            y_diag = _dot(jnp.concatenate(gs, axis=1), jnp.concatenate(xs, axis=0))
            sg = s_ref[d, g]
            y_off = _dot(cg, sg.astype(BF16)) * e_cum_x[:, g * gw:(g + 1) * gw]
            s_ref[d, g] = (sg * e_tot_x[:, g * gw:(g + 1) * gw]
                           + _dot(bm_t[g * SSD_STATE:(g + 1) * SSD_STATE, :].astype(BF16),
                                  xw[:, g * gw:(g + 1) * gw]))
            ys.append(y_diag + y_off)
        y = jnp.concatenate(ys, axis=1)
        if d == 0:
            y = y + dskip_ref[...] * x
        y_ref[...] = y

    direction(0, xf_ref, dtf_ref, yf_ref)
    direction(1, xb_ref, dtb_ref, yb_ref)

    @pl.when(i == nchunks - 1)
    def _():
        hfin_ref[0] = s_ref[...]


def _ssd(xbc, dt, a_row, dskip_row, h0, consts, bsz, seq):
    t = xbc.shape[0]
    L = SSD_CHUNK
    n = seq // L
    tril, triu, expand = consts
    fwd = lambda b, i: (b * n + i, 0)
    bwd = lambda b, i: (b * n + (n - 1 - i), 0)
    const = lambda b, i: (0, 0)
    state = lambda b, i: (b, 0, 0, 0, 0)
    sshape = h0.shape[1:]
    return pl.pallas_call(
        functools.partial(_ssd_kernel, chunk=L, nchunks=n),
        out_shape=(jax.ShapeDtypeStruct((t, D_SSD), F32), jax.ShapeDtypeStruct((t, D_SSD), F32),
                   jax.ShapeDtypeStruct(h0.shape, F32)),
        grid=(bsz, n),
        in_specs=[pl.BlockSpec((L, D_XBC), fwd), pl.BlockSpec((L, D_XBC), bwd),
                  pl.BlockSpec((L, LANES), fwd), pl.BlockSpec((L, LANES), bwd),
                  pl.BlockSpec((1, LANES), const), pl.BlockSpec((1, D_SSD), const),
                  pl.BlockSpec((1,) + sshape, state),
                  pl.BlockSpec((L, L), const), pl.BlockSpec((L, L), const),
                  pl.BlockSpec((LANES, D_SSD), const)],
        out_specs=(pl.BlockSpec((L, D_SSD), fwd), pl.BlockSpec((L, D_SSD), bwd),
                   pl.BlockSpec((1,) + sshape, state)),
        scratch_shapes=[pltpu.VMEM(sshape, F32)],
        compiler_params=_cparams("arbitrary", "arbitrary"),
        name="ssd_scan",
    )(xbc, xbc, dt, dt, a_row, dskip_row, h0, tril, triu, expand)


def _ssd_consts():
    L = SSD_CHUNK
    r = jnp.arange(L)
    tril = (r[:, None] >= r[None, :]).astype(BF16)
    triu = (r[:, None] <= r[None, :]).astype(BF16)
    c = jnp.arange(LANES)[:, None]
    col = jnp.arange(D_SSD)[None, :]
    expand = ((c < 2 * SSD_HEADS) & ((c % SSD_HEADS) == col // SSD_HEADDIM)).astype(BF16)
    return tril, triu, expand


def _out_kernel(yc_ref, yr_ref, yf_ref, yb_ref, z_ref, hx_ref, gt1_ref, lng_ref, lnb_ref, ng_ref,
                wout_ref, gffn_ref, sh2_ref, sc2_ref, wr_ref, br_ref, hnew_ref, t_ref, comb_ref, *, cw):
    yc = yc_ref[...]
    yr = yr_ref[...]
    nconv = yc.shape[1] + yr.shape[1]
    mu = (jnp.sum(yc, axis=-1, keepdims=True) + jnp.sum(yr, axis=-1, keepdims=True)) / nconv
    dc = yc - mu
    dr = yr - mu
    var = (jnp.sum(dc * dc, axis=-1, keepdims=True) + jnp.sum(dr * dr, axis=-1, keepdims=True)) / nconv
    inv = lax.rsqrt(var + LN_EPS)
    a_c = _silu(dc * inv * lng_ref[:, 0:cw] + lnb_ref[:, 0:cw])
    a_r = _silu(dr * inv * lng_ref[:, cw:] + lnb_ref[:, cw:])
    y = (yf_ref[...] + yb_ref[...]) * _silu(z_ref[...])
    ssd = y * lax.rsqrt(jnp.mean(y * y, axis=-1, keepdims=True) + RMS_EPS) * ng_ref[...]
    mix = (_dot(a_c.astype(BF16), wout_ref[0:cw, :]) + _dot(a_r.astype(BF16), wout_ref[cw:D_CONV, :])
           + _dot(ssd.astype(BF16), wout_ref[D_CONV:, :]))
    h = hx_ref[...] + gt1_ref[0] * mix
    hnew_ref[...] = h
    t = h * lax.rsqrt(jnp.mean(h * h, axis=-1, keepdims=True) + RMS_EPS) * gffn_ref[...]
    t = t * (1.0 + sc2_ref[0]) + sh2_ref[0]
    tb = t.astype(BF16)
    t_ref[...] = tb

    logits = _dot(tb, wr_ref[...]) + br_ref[...]
    lane = lax.broadcasted_iota(jnp.int32, logits.shape, 1).astype(F32)
    big = float(LANES)
    ninf = -jnp.inf
    gl = jnp.where(lane < MOE_GROUPS, logits, ninf)
    gmax = jnp.max(gl, axis=-1, keepdims=True)
    gsel = jnp.min(jnp.where(gl == gmax, lane, big), axis=-1, keepdims=True)
    gprob = 1.0 / jnp.sum(jnp.exp(gl - gmax), axis=-1, keepdims=True)
    lo = ROUTER_LANE0 + gsel * MOE_PER_GROUP
    emask = (lane >= lo) & (lane < lo + MOE_PER_GROUP)
    el = jnp.where(emask, logits, ninf)
    e1 = jnp.max(el, axis=-1, keepdims=True)
    i1 = jnp.min(jnp.where(el == e1, lane, big), axis=-1, keepdims=True)
    el2 = jnp.where(lane == i1, ninf, el)
    e2 = jnp.max(el2, axis=-1, keepdims=True)
    i2 = jnp.min(jnp.where(el2 == e2, lane, big), axis=-1, keepdims=True)
    p2 = jnp.exp(e2 - e1)
    w1 = gprob / (1.0 + p2)
    w2 = gprob * p2 / (1.0 + p2)
    comb_ref[...] = jnp.where(lane == i1, w1, 0.0) + jnp.where(lane == i2, w2, 0.0)


def _out_proj(yc, yr, yf, yb, z, hx, gt1, lng, lnb, ng, wout, gffn, sh2, sc2, wr, br, seq, tm):
    t, d = hx.shape
    cw = yc.shape[1]
    tpb = seq // tm
    tok = lambda i: (i, 0)
    per_batch = lambda i: (i // tpb, 0, 0)
    const = lambda i: (0, 0)
    row = lambda n: pl.BlockSpec((1, n), const)
    return pl.pallas_call(
        functools.partial(_out_kernel, cw=cw),
        out_shape=(jax.ShapeDtypeStruct((t, d), F32), jax.ShapeDtypeStruct((t, d), BF16),
                   jax.ShapeDtypeStruct((t, LANES), F32)),
        grid=(t // tm,),
        in_specs=[pl.BlockSpec((tm, cw), tok), pl.BlockSpec((tm, D_CONV - cw), tok),
                  pl.BlockSpec((tm, D_SSD), tok), pl.BlockSpec((tm, D_SSD), tok), pl.BlockSpec((tm, D_SSD), tok),
                  pl.BlockSpec((tm, d), tok), pl.BlockSpec((1, 1, d), per_batch),
                  row(D_CONV), row(D_CONV), row(D_SSD),
                  pl.BlockSpec((D_CONV + D_SSD, d), const), row(d),
                  pl.BlockSpec((1, 1, d), per_batch), pl.BlockSpec((1, 1, d), per_batch),
                  pl.BlockSpec((d, LANES), const), row(LANES)],
        out_specs=(pl.BlockSpec((tm, d), tok), pl.BlockSpec((tm, d), tok), pl.BlockSpec((tm, LANES), tok)),
        compiler_params=_cparams("arbitrary"),
        name="out_proj",
    )(yc, yr, yf, yb, z, hx, gt1, lng, lnb, ng, wout, gffn, sh2, sc2, wr, br)


def _moe_kernel(t_ref, comb_ref, hx_ref, gt2_ref, wg_ref, wu_ref, wd_ref, gfin_ref, o_ref, acc_ref, *, final):
    e = pl.program_id(1)

    @pl.when(e == 0)
    def _():
        acc_ref[...] = jnp.zeros_like(acc_ref)

    t = t_ref[...]
    comb = comb_ref[...]
    lane = lax.broadcasted_iota(jnp.int32, comb.shape, 1)
    cw = jnp.sum(jnp.where(lane == e + ROUTER_LANE0, comb, 0.0), axis=-1, keepdims=True)
    hid = _silu(_dot(t, wg_ref[0])) * _dot(t, wu_ref[0])
    acc_ref[...] += _dot((hid * cw).astype(BF16), wd_ref[0])

    @pl.when(e == N_EXPERTS - 1)
    def _():
        h = hx_ref[...] + gt2_ref[0] * acc_ref[...]
        if final:
            h = h * lax.rsqrt(jnp.mean(h * h, axis=-1, keepdims=True) + RMS_EPS) * gfin_ref[...]
        o_ref[...] = h


def _moe(t_bf, comb, hx, gt2, wg, wu, wd, gfin, seq, tm, final):
    t, d = hx.shape
    ff = wg.shape[2]
    tpb = seq // tm
    tok = lambda i, e: (i, 0)
    per_batch = lambda i, e: (i // tpb, 0, 0)
    return pl.pallas_call(
        functools.partial(_moe_kernel, final=final),
        out_shape=jax.ShapeDtypeStruct((t, d), F32),
        grid=(t // tm, N_EXPERTS),
        in_specs=[pl.BlockSpec((tm, d), tok), pl.BlockSpec((tm, LANES), tok), pl.BlockSpec((tm, d), tok),
                  pl.BlockSpec((1, 1, d), per_batch),
                  pl.BlockSpec((1, d, ff), lambda i, e: (e, 0, 0)),
                  pl.BlockSpec((1, d, ff), lambda i, e: (e, 0, 0)),
                  pl.BlockSpec((1, ff, d), lambda i, e: (e, 0, 0)),
                  pl.BlockSpec((1, d), lambda i, e: (0, 0))],
        out_specs=pl.BlockSpec((tm, d), tok),
        scratch_shapes=[pltpu.VMEM((tm, d), F32)],
        compiler_params=_cparams("arbitrary", "arbitrary"),
        name="moe",
    )(t_bf, comb, hx, gt2, wg, wu, wd, gfin)


def _pad_rows(w, n):
    return jnp.zeros((n, w.shape[1]), w.dtype).at[:w.shape[0]].set(w)


def kernel(x, c, ctx, c_ctx, w_ada, b_ada, g_mix, g_ffn, w_in, conv_w, conv_b, conv_ln_g, conv_ln_b,
           ssd_conv_w, ssd_conv_b, dt_bias, a_log, d_skip, ssd_norm_g, w_out, w_router_group,
           b_router_group, w_router_expert, b_router_expert, w_gate, w_up, w_down, g_final):
    bsz, seq, d = x.shape
    lc = ctx.shape[1]
    depth = w_in.shape[0]
    lo = 2 * D_CONV + D_SSD
    n_main = lo + D_XBC
    tm_x, tm_c = 512, lc
    consts = _ssd_consts()
    gw = (SSD_HEADS // SSD_GROUPS) * SSD_HEADDIM
    zero_state = jnp.zeros((bsz, 2, SSD_GROUPS, SSD_STATE, gw), F32)

    h_x = x.reshape(bsz * seq, d)
    h_c = ctx.reshape(bsz * lc, d)
    cvecs = jnp.zeros((SUBLANES, d), F32).at[:bsz].set(c).at[bsz].set(c_ctx)
    row = lambda v: v.reshape(1, -1)

    for l in range(depth):
        last = l == depth - 1
        mods = _ada(cvecs, w_ada[l], b_ada[l]).reshape(SUBLANES, 6, d)
        mx = [mods[:bsz, j][:, None, :] for j in range(6)]
        mc = [jnp.broadcast_to(mods[bsz, j][None, None, :], (bsz, 1, d)) for j in range(6)]
        w_main = w_in[l][:, :n_main].astype(BF16)
        w_dt = jnp.zeros((d, LANES), F32).at[:, :2 * SSD_HEADS].set(w_in[l][:, n_main:]).astype(BF16)
        dtb_row = jnp.zeros((1, LANES), F32).at[0, :2 * SSD_HEADS].set(dt_bias[l].reshape(-1))
        a_row = jnp.zeros((1, LANES), F32).at[0, :2 * SSD_HEADS].set(-jnp.exp(a_log[l].astype(F32)).reshape(-1))
        dskip_row = jnp.repeat(d_skip[l], SSD_HEADDIM).reshape(1, D_SSD)
        convw_pad = _pad_rows(conv_w[l], 32)
        wout_b = w_out[l].astype(BF16)
        wr = jnp.zeros((d, LANES), F32).at[:, :MOE_GROUPS].set(w_router_group[l])
        wr = wr.at[:, ROUTER_LANE0:ROUTER_LANE0 + N_EXPERTS].set(w_router_expert[l]).astype(BF16)
        br = jnp.zeros((1, LANES), F32).at[0, :MOE_GROUPS].set(b_router_group[l])
        br = br.at[0, ROUTER_LANE0:ROUTER_LANE0 + N_EXPERTS].set(b_router_expert[l])
        wg_b, wu_b, wd_b = w_gate[l].astype(BF16), w_up[l].astype(BF16), w_down[l].astype(BF16)

        def mixer_front(h, m, s, tm):
            v, z, xbc_raw, dt_raw = _in_proj(h, m[0], m[1], row(g_mix[l]), w_main, w_dt, s, tm)
            xbc, dt = _xbc_conv(xbc_raw, ssd_conv_w[l], ssd_conv_b[l], dt_raw, dtb_row, s, tm)
            return v, z, xbc, dt

        def mixer_back(h, m, s, tm, yc, yr, yf, yb, z, final):
            h_mid, t_bf, comb = _out_proj(yc, yr, yf, yb, z, h, m[2], row(conv_ln_g[l]), row(conv_ln_b[l]),
                                          row(ssd_norm_g[l]), wout_b, row(g_ffn[l]), m[3], m[4], wr, br, s, tm)
            return _moe(t_bf, comb, h_mid, m[5], wg_b, wu_b, wd_b, row(g_final), s, tm, final)

        v_c, z_c, xbc_c, dt_c = mixer_front(h_c, mc, lc, tm_c)
        yf_c, yb_c, h_ctx = _ssd(xbc_c, dt_c, a_row, dskip_row, zero_state, consts, bsz, lc)

        v, z, xbc, dt = mixer_front(h_x, mx, seq, tm_x)
        yf, yb, _ = _ssd(xbc, dt, a_row, dskip_row, h_ctx, consts, bsz, seq)
        half = D_CONV // 2
        yc = _seg_conv(v, convw_pad, row(conv_b[l]), GRID_W, half, tm_x)
        yr = _row_conv(v.reshape(bsz, seq, D_CONV), convw_pad, row(conv_b[l]), half // LANES,
                       half // LANES).reshape(bsz * seq, half)
        h_x = mixer_back(h_x, mx, seq, tm_x, yc, yr, yf, yb, z, last)

        if not last:
            y_all = _seg_conv(v_c, convw_pad, row(conv_b[l]), lc, D_CONV, tm_c)
            h_c = mixer_back(h_c, mc, lc, tm_c, y_all[:, :half], y_all[:, half:], yf_c, yb_c, z_c, False)

    return h_x.reshape(bsz, seq, d).astype(x.dtype)
```

```python
import functools

import jax
import jax.numpy as jnp
from jax import lax
from jax.experimental import pallas as pl
from jax.experimental.pallas import tpu as pltpu

F32 = jnp.float32
BF16 = jnp.bfloat16

RMS_EPS = 1e-6
LN_EPS = 1e-5

GRID_W = 64
D_CONV = 512
D_SSD = 512
SSD_HEADS = 8
SSD_HEADDIM = 64
SSD_GROUPS = 2
SSD_STATE = 128
D_BC = SSD_GROUPS * SSD_STATE
D_XBC = D_SSD + 2 * D_BC
MOE_GROUPS = 4
MOE_PER_GROUP = 4
N_EXPERTS = MOE_GROUPS * MOE_PER_GROUP

LANES = 128
SUBLANES = 8
CONV_PAD = 16
SSD_CHUNK = 128
ROUTER_LANE0 = MOE_GROUPS
PAIRS_PER_GROUP = MOE_PER_GROUP * (MOE_PER_GROUP - 1) // 2
N_BUCKETS = MOE_GROUPS * PAIRS_PER_GROUP
META_WA, META_WB, META_KEY = 0, 1, 2
MOE_ROWS = 256
VMEM_LIMIT = 56 * 1024 * 1024


def _cparams(*sem):
    return pltpu.CompilerParams(dimension_semantics=sem, vmem_limit_bytes=VMEM_LIMIT)


def _dot(a, b):
    return jnp.dot(a, b, preferred_element_type=F32)


def _silu(x):
    return x * jax.nn.sigmoid(x)


def _split_bf16(x):
    hi = x.astype(BF16)
    lo = (x - hi.astype(F32)).astype(BF16)
    return hi, lo


def _ada_kernel(c_ref, w_ref, b_ref, o_ref):
    c = c_ref[...]
    o_ref[...] = jnp.dot(_silu(c), w_ref[...], preferred_element_type=F32,
                         precision=lax.Precision.HIGHEST) + b_ref[...]


def _ada(cvecs, w_ada, b_ada):
    d, n = w_ada.shape
    bn = 1536
    return pl.pallas_call(
        _ada_kernel,
        out_shape=jax.ShapeDtypeStruct((cvecs.shape[0], n), F32),
        grid=(n // bn,),
        in_specs=[pl.BlockSpec((cvecs.shape[0], d), lambda j: (0, 0)),
                  pl.BlockSpec((d, bn), lambda j: (0, j)),
                  pl.BlockSpec((1, bn), lambda j: (0, j))],
        out_specs=pl.BlockSpec((cvecs.shape[0], bn), lambda j: (0, j)),
        compiler_params=_cparams("arbitrary"),
        name="ada",
    )(cvecs, w_ada, b_ada.reshape(1, n))


def _in_kernel(*refs, merge):
    if merge:
        (x_ref, moe_ref, gt_ref, sh_ref, sc_ref, g_ref, w_ref, wdt_ref,
         h_ref, v_ref, z_ref, xbc_ref, dt_ref) = refs
        x = x_ref[...] + gt_ref[0] * moe_ref[...]
        h_ref[...] = x
    else:
        x_ref, sh_ref, sc_ref, g_ref, w_ref, wdt_ref, v_ref, z_ref, xbc_ref, dt_ref = refs
        x = x_ref[...]
    ms = jnp.mean(x * x, axis=-1, keepdims=True)
    h = x * lax.rsqrt(ms + RMS_EPS) * g_ref[...]
    h = (h * (1.0 + sc_ref[0]) + sh_ref[0]).astype(BF16)
    u = _dot(h, w_ref[:, 0:D_CONV])
    gate = _dot(h, w_ref[:, D_CONV:2 * D_CONV])
    v_ref[...] = u * jax.nn.sigmoid(gate)
    z_ref[...] = _dot(h, w_ref[:, 2 * D_CONV:2 * D_CONV + D_SSD])
    xbc_ref[...] = _dot(h, w_ref[:, 2 * D_CONV + D_SSD:])
    dt_ref[...] = _dot(h, wdt_ref[...])


def _in_proj(x, shift, scale, g, w_main, w_dt, seq, tm, moe=None, gate=None):
    t, d = x.shape
    tpb = seq // tm
    n_main = w_main.shape[1]
    merge = moe is not None
    tok = lambda i: (i, 0)
    per_batch = lambda i: (i // tpb, 0, 0)
    const = lambda i: (0, 0)
    tok_d = pl.BlockSpec((tm, d), tok)
    mod = pl.BlockSpec((1, 1, d), per_batch)
    outs = (jax.ShapeDtypeStruct((t, D_CONV), F32), jax.ShapeDtypeStruct((t, D_SSD), F32),
            jax.ShapeDtypeStruct((t, D_XBC), F32), jax.ShapeDtypeStruct((t, LANES), F32))
    out_specs = (pl.BlockSpec((tm, D_CONV), tok), pl.BlockSpec((tm, D_SSD), tok),
                 pl.BlockSpec((tm, D_XBC), tok), pl.BlockSpec((tm, LANES), tok))
    args = (x, moe, gate) if merge else (x,)
    in_specs = [tok_d, tok_d, mod] if merge else [tok_d]
    if merge:
        outs = (jax.ShapeDtypeStruct((t, d), F32),) + outs
        out_specs = (tok_d,) + out_specs
    return pl.pallas_call(
        functools.partial(_in_kernel, merge=merge),
        out_shape=outs,
        grid=(t // tm,),
        in_specs=in_specs + [mod, mod, pl.BlockSpec((1, d), const),
                             pl.BlockSpec((d, n_main), const), pl.BlockSpec((d, LANES), const)],
        out_specs=out_specs,
        compiler_params=_cparams("arbitrary"),
        name="in_proj",
    )(*args, shift, scale, g, w_main, w_dt)


def _xbc_kernel(prev_ref, cur_ref, next_ref, w_ref, b_ref, dtr_ref, dtb_ref, o_ref, dto_ref, scr,
                *, tm, tpb, k):
    i = pl.program_id(0)
    first = (i % tpb) == 0
    last = (i % tpb) == tpb - 1
    scr[0:SUBLANES, :] = jnp.where(first, 0.0, prev_ref[...])
    scr[SUBLANES:SUBLANES + tm, :] = cur_ref[...]
    scr[SUBLANES + tm:2 * SUBLANES + tm, :] = jnp.where(last, 0.0, next_ref[...])
    acc = jnp.broadcast_to(b_ref[...], cur_ref.shape)
    for j in range(k):
        acc = acc + w_ref[j:j + 1, :] * scr[pl.ds(SUBLANES + j - k // 2, tm), :]
    o_ref[...] = _silu(acc)
    xdt = dtr_ref[...] + dtb_ref[...]
    dto_ref[...] = jnp.maximum(xdt, 0.0) + jnp.log1p(jnp.exp(-jnp.abs(xdt)))


def _xbc_conv(xbc, w, b, dt_raw, dt_bias, seq, tm):
    t, c = xbc.shape
    k = w.shape[0]
    tpb = seq // tm
    nb8 = tm // SUBLANES
    last8 = t // SUBLANES - 1
    w_pad = jnp.zeros((SUBLANES, c), F32).at[:k].set(w)
    tok = lambda i: (i, 0)
    const = lambda i: (0, 0)
    return pl.pallas_call(
        functools.partial(_xbc_kernel, tm=tm, tpb=tpb, k=k),
        out_shape=(jax.ShapeDtypeStruct((t, c), F32), jax.ShapeDtypeStruct((t, LANES), F32)),
        grid=(t // tm,),
        in_specs=[pl.BlockSpec((SUBLANES, c), lambda i: (jnp.maximum(i * nb8 - 1, 0), 0)),
                  pl.BlockSpec((tm, c), tok),
                  pl.BlockSpec((SUBLANES, c), lambda i: (jnp.minimum((i + 1) * nb8, last8), 0)),
                  pl.BlockSpec((SUBLANES, c), const), pl.BlockSpec((1, c), const),
                  pl.BlockSpec((tm, LANES), tok), pl.BlockSpec((1, LANES), const)],
        out_specs=(pl.BlockSpec((tm, c), tok), pl.BlockSpec((tm, LANES), tok)),
        scratch_shapes=[pltpu.VMEM((tm + 2 * SUBLANES, c), F32)],
        compiler_params=_cparams("arbitrary"),
        name="xbc_conv",
    )(xbc, xbc, xbc, w_pad, b.reshape(1, c), dt_raw, dt_bias)


def _segconv_kernel(v_ref, w_ref, b_ref, o_ref, scr, *, seg, nseg, k, cb):
    zeros = jnp.zeros((CONV_PAD, cb), F32)
    scr[0:CONV_PAD, :] = zeros
    scr[CONV_PAD + seg:2 * CONV_PAD + seg, :] = zeros

    def body(s, carry):
        base = pl.multiple_of(s * seg, seg)
        scr[CONV_PAD:CONV_PAD + seg, :] = v_ref[pl.ds(base, seg), :]
        for c in range(cb // LANES):
            cs = slice(c * LANES, (c + 1) * LANES)
            acc = jnp.broadcast_to(b_ref[:, cs], (seg, LANES))
            for j in range(k):
                acc = acc + w_ref[j:j + 1, cs] * scr[pl.ds(CONV_PAD + j - k // 2, seg), cs]
            o_ref[pl.ds(base, seg), cs] = acc
        return carry

    lax.fori_loop(0, nseg, body, 0)


def _seg_conv(v, w_pad, b, seg, cb, tm):
    t = v.shape[0]
    k = 31
    tok = lambda i: (i, 0)
    const = lambda i: (0, 0)
    return pl.pallas_call(
        functools.partial(_segconv_kernel, seg=seg, nseg=tm // seg, k=k, cb=cb),
        out_shape=jax.ShapeDtypeStruct((t, cb), F32),
        grid=(t // tm,),
        in_specs=[pl.BlockSpec((tm, cb), tok), pl.BlockSpec((w_pad.shape[0], cb), const),
                  pl.BlockSpec((1, cb), const)],
        out_specs=pl.BlockSpec((tm, cb), tok),
        scratch_shapes=[pltpu.VMEM((seg + 2 * CONV_PAD, cb), F32)],
        compiler_params=_cparams("arbitrary"),
        name="seg_conv",
    )(v, w_pad, b)


def _rowconv_kernel(v_ref, w_ref, b_ref, o_ref, scr, *, width, rows, k):
    half = k // 2
    pad = jnp.zeros((half * width, LANES), F32)
    scr[0:half * width, :] = pad
    scr[(half + rows) * width:(2 * half + rows) * width, :] = pad
    scr[half * width:(half + rows) * width, :] = v_ref[0]

    def body(r, carry):
        acc = jnp.broadcast_to(b_ref[...], (width, LANES))
        for j in range(k):
            start = pl.multiple_of((r + j) * width, width)
            acc = acc + w_ref[j:j + 1, :] * scr[pl.ds(start, width), :]
        o_ref[0, pl.ds(pl.multiple_of(r * width, width), width), :] = acc
        return carry

    lax.fori_loop(0, rows, body, 0)


def _row_conv(v3, w_pad, b, lane_block0, nblk):
    bsz, seq, _ = v3.shape
    rows = seq // GRID_W
    k = 31
    return pl.pallas_call(
        functools.partial(_rowconv_kernel, width=GRID_W, rows=rows, k=k),
        out_shape=jax.ShapeDtypeStruct((bsz, seq, nblk * LANES), F32),
        grid=(bsz, nblk),
        in_specs=[pl.BlockSpec((1, seq, LANES), lambda bi, c: (bi, 0, lane_block0 + c)),
                  pl.BlockSpec((w_pad.shape[0], LANES), lambda bi, c: (0, lane_block0 + c)),
                  pl.BlockSpec((1, LANES), lambda bi, c: (0, lane_block0 + c))],
        out_specs=pl.BlockSpec((1, seq, LANES), lambda bi, c: (bi, 0, c)),
        scratch_shapes=[pltpu.VMEM(((rows + 2 * (k // 2)) * GRID_W, LANES), F32)],
        compiler_params=_cparams("arbitrary", "arbitrary"),
        name="row_conv",
    )(v3, w_pad, b)


def _ssd_kernel(xf_ref, xb_ref, dtf_ref, dtb_ref, a_ref, dskip_ref, h0_ref, tril_ref, triu_ref, e_ref,
                yf_ref, yb_ref, hfin_ref, s_ref, *, chunk, nchunks):
    i = pl.program_id(1)

    @pl.when(i == 0)
    def _():
        s_ref[...] = h0_ref[0]

    L = chunk
    hpg = SSD_HEADS // SSD_GROUPS
    gw = hpg * SSD_HEADDIM
    lane = lax.broadcasted_iota(jnp.int32, (L, LANES), 1)
    row_i = lax.broadcasted_iota(jnp.int32, (L, L), 0)
    col_i = lax.broadcasted_iota(jnp.int32, (L, L), 1)
    head_of_lane = lax.broadcasted_iota(jnp.int32, (L, gw), 1) // SSD_HEADDIM

    def direction(d, x_ref, dt_ref, y_ref):
        xall = x_ref[...]
        x = xall[:, :D_SSD]
        bm = xall[:, D_SSD:D_SSD + D_BC]
        cm = xall[:, D_SSD + D_BC:]
        dt = dt_ref[...]
        dmask = (lane >= SSD_HEADS * d) & (lane < SSD_HEADS * (d + 1))
        da = jnp.where(dmask, dt * a_ref[...], 0.0)
        tri = tril_ref[...] if d == 0 else triu_ref[...]
        da_hi, da_lo = _split_bf16(da)
        cum = _dot(tri, da_hi) + _dot(tri, da_lo)
        tot = cum[L - 1:L, :] if d == 0 else cum[0:1, :]
        e_cum = jnp.where(dmask, jnp.exp(cum), 0.0)
        w_dt = jnp.where(dmask, jnp.exp(tot - cum) * dt, 0.0)
        e_tot = jnp.where(dmask[0:1], jnp.exp(tot), 0.0)
        stack = jnp.concatenate([e_cum, w_dt, jnp.broadcast_to(e_tot, (SUBLANES, LANES))], axis=0)
        s_hi, s_lo = _split_bf16(stack)
        ex = _dot(s_hi, e_ref[...]) + _dot(s_lo, e_ref[...])
        e_cum_x = ex[0:L]
        w_dt_x = ex[L:2 * L]
        e_tot_x = ex[2 * L:2 * L + 1]
        cum_t = cum.T
        dt_t = dt.T
        xw = (x * w_dt_x).astype(BF16)
        bm_t = bm.T
        mask = (row_i >= col_i) if d == 0 else (row_i <= col_i)
        ys = []
        for g in range(SSD_GROUPS):
            cg = cm[:, g * SSD_STATE:(g + 1) * SSD_STATE].astype(BF16)
            bg = bm[:, g * SSD_STATE:(g + 1) * SSD_STATE].astype(BF16)
            cb = lax.dot_general(cg, bg, (((1,), (1,)), ((), ())), preferred_element_type=F32)
            xg = x[:, g * gw:(g + 1) * gw]
            gs, xs = [], []
            for hh in range(hpg):
                c = SSD_HEADS * d + hpg * g + hh
                seg = cum[:, c:c + 1] - cum_t[c:c + 1, :]
                gm = jnp.exp(jnp.where(mask, seg, -jnp.inf)) * cb * dt_t[c:c + 1, :]
                gs.append(gm.astype(BF16))
                xs.append(jnp.where(head_of_lane == hh, xg, 0.0).astype(BF16))
            y_diag = _dot(jnp.concatenate(gs, axis=1), jnp.concatenate(xs, axis=0))
            sg = s_ref[d, g]
            y_off = _dot(cg, sg.astype(BF16)) * e_cum_x[:, g * gw:(g + 1) * gw]
            s_ref[d, g] = (sg * e_tot_x[:, g * gw:(g + 1) * gw]
                           + _dot(bm_t[g * SSD_STATE:(g + 1) * SSD_STATE, :].astype(BF16),
                                  xw[:, g * gw:(g + 1) * gw]))
            ys.append(y_diag + y_off)
        y = jnp.concatenate(ys, axis=1)
        if d == 0:
            y = y + dskip_ref[...] * x
        y_ref[...] = y

    direction(0, xf_ref, dtf_ref, yf_ref)
    direction(1, xb_ref, dtb_ref, yb_ref)

    @pl.when(i == nchunks - 1)
    def _():
        hfin_ref[0] = s_ref[...]


def _ssd(xbc, dt, a_row, dskip_row, h0, consts, bsz, seq):
    t = xbc.shape[0]
    L = SSD_CHUNK
    n = seq // L
    tril, triu, expand = consts
    fwd = lambda b, i: (b * n + i, 0)
    bwd = lambda b, i: (b * n + (n - 1 - i), 0)
    const = lambda b, i: (0, 0)
    state = lambda b, i: (b, 0, 0, 0, 0)
    sshape = h0.shape[1:]
    return pl.pallas_call(
        functools.partial(_ssd_kernel, chunk=L, nchunks=n),
        out_shape=(jax.ShapeDtypeStruct((t, D_SSD), F32), jax.ShapeDtypeStruct((t, D_SSD), F32),
                   jax.ShapeDtypeStruct(h0.shape, F32)),
        grid=(bsz, n),
        in_specs=[pl.BlockSpec((L, D_XBC), fwd), pl.BlockSpec((L, D_XBC), bwd),
                  pl.BlockSpec((L, LANES), fwd), pl.BlockSpec((L, LANES), bwd),
                  pl.BlockSpec((1, LANES), const), pl.BlockSpec((1, D_SSD), const),
                  pl.BlockSpec((1,) + sshape, state),
                  pl.BlockSpec((L, L), const), pl.BlockSpec((L, L), const),
                  pl.BlockSpec((LANES, D_SSD), const)],
        out_specs=(pl.BlockSpec((L, D_SSD), fwd), pl.BlockSpec((L, D_SSD), bwd),
                   pl.BlockSpec((1,) + sshape, state)),
        scratch_shapes=[pltpu.VMEM(sshape, F32)],
        compiler_params=_cparams("arbitrary", "arbitrary"),
        name="ssd_scan",
    )(xbc, xbc, dt, dt, a_row, dskip_row, h0, tril, triu, expand)


def _ssd_consts():
    L = SSD_CHUNK
    r = jnp.arange(L)
    tril = (r[:, None] >= r[None, :]).astype(BF16)
    triu = (r[:, None] <= r[None, :]).astype(BF16)
    c = jnp.arange(LANES)[:, None]
    col = jnp.arange(D_SSD)[None, :]
    expand = ((c < 2 * SSD_HEADS) & ((c % SSD_HEADS) == col // SSD_HEADDIM)).astype(BF16)
    return tril, triu, expand


def _out_kernel(yc_ref, yr_ref, yf_ref, yb_ref, z_ref, hx_ref, gt1_ref, lng_ref, lnb_ref, ng_ref,
                wout_ref, gffn_ref, sh2_ref, sc2_ref, wr_ref, br_ref, hnew_ref, text_ref, comb_ref, keys_ref,
                *, cw):
    yc = yc_ref[...]
    yr = yr_ref[...]
    nconv = yc.shape[1] + yr.shape[1]
    mu = (jnp.sum(yc, axis=-1, keepdims=True) + jnp.sum(yr, axis=-1, keepdims=True)) / nconv
    dc = yc - mu
    dr = yr - mu
    var = (jnp.sum(dc * dc, axis=-1, keepdims=True) + jnp.sum(dr * dr, axis=-1, keepdims=True)) / nconv
    inv = lax.rsqrt(var + LN_EPS)
    a_c = _silu(dc * inv * lng_ref[:, 0:cw] + lnb_ref[:, 0:cw])
    a_r = _silu(dr * inv * lng_ref[:, cw:] + lnb_ref[:, cw:])
    y = (yf_ref[...] + yb_ref[...]) * _silu(z_ref[...])
    ssd = y * lax.rsqrt(jnp.mean(y * y, axis=-1, keepdims=True) + RMS_EPS) * ng_ref[...]
    mix = (_dot(a_c.astype(BF16), wout_ref[0:cw, :]) + _dot(a_r.astype(BF16), wout_ref[cw:D_CONV, :])
           + _dot(ssd.astype(BF16), wout_ref[D_CONV:, :]))
    h = hx_ref[...] + gt1_ref[0] * mix
    hnew_ref[...] = h
    t = h * lax.rsqrt(jnp.mean(h * h, axis=-1, keepdims=True) + RMS_EPS) * gffn_ref[...]
    t = t * (1.0 + sc2_ref[0]) + sh2_ref[0]
    tb = t.astype(BF16)

    logits = _dot(tb, wr_ref[...]) + br_ref[...]
    lane = lax.broadcasted_iota(jnp.int32, logits.shape, 1).astype(F32)
    big = float(LANES)
    ninf = -jnp.inf
    gl = jnp.where(lane < MOE_GROUPS, logits, ninf)
    gmax = jnp.max(gl, axis=-1, keepdims=True)
    gsel = jnp.min(jnp.where(gl == gmax, lane, big), axis=-1, keepdims=True)
    gprob = 1.0 / jnp.sum(jnp.exp(gl - gmax), axis=-1, keepdims=True)
    lo = ROUTER_LANE0 + gsel * MOE_PER_GROUP
    emask = (lane >= lo) & (lane < lo + MOE_PER_GROUP)
    el = jnp.where(emask, logits, ninf)
    e1 = jnp.max(el, axis=-1, keepdims=True)
    i1 = jnp.min(jnp.where(el == e1, lane, big), axis=-1, keepdims=True)
    el2 = jnp.where(lane == i1, ninf, el)
    e2 = jnp.max(el2, axis=-1, keepdims=True)
    i2 = jnp.min(jnp.where(el2 == e2, lane, big), axis=-1, keepdims=True)
    p2 = jnp.exp(e2 - e1)
    w1 = gprob / (1.0 + p2)
    w2 = gprob * p2 / (1.0 + p2)
    comb_ref[...] = jnp.where(lane == i1, w1, 0.0) + jnp.where(lane == i2, w2, 0.0)
    first = i1 < i2
    l_lo = jnp.minimum(i1, i2) - lo
    l_hi = jnp.maximum(i1, i2) - lo
    key = gsel * PAIRS_PER_GROUP + l_lo * (7.0 - l_lo) * 0.5 + (l_hi - l_lo - 1.0)
    meta = (jnp.where(lane == META_WA, jnp.where(first, w1, w2), 0.0)
            + jnp.where(lane == META_WB, jnp.where(first, w2, w1), 0.0)
            + jnp.where(lane == META_KEY, key, 0.0))
    d = t.shape[1]
    text_ref[:, 0:d] = t
    text_ref[:, d:] = meta
    keys_ref[...] = meta.T[0:SUBLANES, :]


def _out_proj(yc, yr, yf, yb, z, hx, gt1, lng, lnb, ng, wout, gffn, sh2, sc2, wr, br, seq, tm):
    t, d = hx.shape
    cw = yc.shape[1]
    tpb = seq // tm
    tok = lambda i: (i, 0)
    per_batch = lambda i: (i // tpb, 0, 0)
    const = lambda i: (0, 0)
    row = lambda n: pl.BlockSpec((1, n), const)
    return pl.pallas_call(
        functools.partial(_out_kernel, cw=cw),
        out_shape=(jax.ShapeDtypeStruct((t, d), F32), jax.ShapeDtypeStruct((t, d + LANES), F32),
                   jax.ShapeDtypeStruct((t, LANES), F32), jax.ShapeDtypeStruct((SUBLANES, t), F32)),
        grid=(t // tm,),
        in_specs=[pl.BlockSpec((tm, cw), tok), pl.BlockSpec((tm, D_CONV - cw), tok),
                  pl.BlockSpec((tm, D_SSD), tok), pl.BlockSpec((tm, D_SSD), tok), pl.BlockSpec((tm, D_SSD), tok),
                  pl.BlockSpec((tm, d), tok), pl.BlockSpec((1, 1, d), per_batch),
                  row(D_CONV), row(D_CONV), row(D_SSD),
                  pl.BlockSpec((D_CONV + D_SSD, d), const), row(d),
                  pl.BlockSpec((1, 1, d), per_batch), pl.BlockSpec((1, 1, d), per_batch),
                  pl.BlockSpec((d, LANES), const), row(LANES)],
        out_specs=(pl.BlockSpec((tm, d), tok), pl.BlockSpec((tm, d + LANES), tok),
                   pl.BlockSpec((tm, LANES), tok), pl.BlockSpec((SUBLANES, tm), lambda i: (0, i))),
        compiler_params=_cparams("arbitrary"),
        name="out_proj",
    )(yc, yr, yf, yb, z, hx, gt1, lng, lnb, ng, wout, gffn, sh2, sc2, wr, br)


def _moe_kernel(t_ref, comb_ref, hx_ref, gt2_ref, wg_ref, wu_ref, wd_ref, o_ref, acc_ref):
    e = pl.program_id(1)

    @pl.when(e == 0)
    def _():
        acc_ref[...] = jnp.zeros_like(acc_ref)

    t = t_ref[...].astype(BF16)
    comb = comb_ref[...]
    lane = lax.broadcasted_iota(jnp.int32, comb.shape, 1)
    cw = jnp.sum(jnp.where(lane == e + ROUTER_LANE0, comb, 0.0), axis=-1, keepdims=True)
    hid = _silu(_dot(t, wg_ref[0])) * _dot(t, wu_ref[0])
    acc_ref[...] += _dot((hid * cw).astype(BF16), wd_ref[0])

    @pl.when(e == N_EXPERTS - 1)
    def _():
        o_ref[...] = hx_ref[...] + gt2_ref[0] * acc_ref[...]


def _moe(t_in, comb, hx, gt2, wg, wu, wd, seq, tm):
    t, d = hx.shape
    ff = wg.shape[2]
    tpb = seq // tm
    tok = lambda i, e: (i, 0)
    per_batch = lambda i, e: (i // tpb, 0, 0)
    return pl.pallas_call(
        _moe_kernel,
        out_shape=jax.ShapeDtypeStruct((t, d), F32),
        grid=(t // tm, N_EXPERTS),
        in_specs=[pl.BlockSpec((tm, d), tok), pl.BlockSpec((tm, LANES), tok), pl.BlockSpec((tm, d), tok),
                  pl.BlockSpec((1, 1, d), per_batch),
                  pl.BlockSpec((1, d, ff), lambda i, e: (e, 0, 0)),
                  pl.BlockSpec((1, d, ff), lambda i, e: (e, 0, 0)),
                  pl.BlockSpec((1, ff, d), lambda i, e: (e, 0, 0))],
        out_specs=pl.BlockSpec((tm, d), tok),
        scratch_shapes=[pltpu.VMEM((tm, d), F32)],
        compiler_params=_cparams("arbitrary", "arbitrary"),
        name="moe_dense",
    )(t_in, comb, hx, gt2, wg, wu, wd)


def _moe_sparse_kernel(ea_ref, eb_ref, nvalid_ref, gidx_ref, gnext_ref, sidx_ref, text_hbm,
                       wga_ref, wua_ref, wda_ref, wgb_ref, wub_ref, wdb_ref, out_hbm,
                       gbuf, obuf, gsem, ssem, *, rows, ntiles, d):
    i = pl.program_id(0)
    slot = i % 2

    def gather_copy(tok, r, s):
        return pltpu.make_async_copy(text_hbm.at[pl.ds(tok, 1), :], gbuf.at[s, pl.ds(r, 1), :], gsem.at[s])

    def scatter_copy(tok, r, s):
        return pltpu.make_async_copy(obuf.at[s, pl.ds(r, 1), :], out_hbm.at[pl.ds(tok, 1), :], ssem.at[s])

    def scatter_wait(n, s):
        @pl.when(n == rows)
        def _():
            for r in range(rows):
                scatter_copy(0, r, s).wait()

        @pl.when(jnp.logical_and(n > 0, n < rows))
        def _():
            def body(r, carry):
                scatter_copy(0, r, s).wait()
                return carry
            lax.fori_loop(0, n, body, 0)

    @pl.when(i == 0)
    def _():
        for r in range(rows):
            gather_copy(gidx_ref[0, 0, r], r, 0).start()

    prev_valid = nvalid_ref[jnp.maximum(i - 1, 0)] > 0

    @pl.when(jnp.logical_or(i == 0, prev_valid))
    def _():
        for r in range(rows):
            gather_copy(0, r, slot).wait()

    @pl.when(i >= 2)
    def _():
        scatter_wait(nvalid_ref[jnp.maximum(i - 2, 0)], slot)

    n_here = nvalid_ref[i]

    @pl.when(n_here > 0)
    def _():
        for r in range(rows):
            gather_copy(gnext_ref[0, 0, r], r, 1 - slot).start()
        buf = gbuf[slot]
        t = buf[:, 0:d].astype(BF16)
        wa = buf[:, d + META_WA:d + META_WA + 1]
        wb = buf[:, d + META_WB:d + META_WB + 1]
        ha = _silu(_dot(t, wga_ref[0])) * _dot(t, wua_ref[0]) * wa
        hb = _silu(_dot(t, wgb_ref[0])) * _dot(t, wub_ref[0]) * wb
        obuf[slot] = _dot(ha.astype(BF16), wda_ref[0]) + _dot(hb.astype(BF16), wdb_ref[0])

        @pl.when(n_here == rows)
        def _():
            for r in range(rows):
                scatter_copy(sidx_ref[0, 0, r], r, slot).start()

        @pl.when(n_here < rows)
        def _():
            def body(r, carry):
                scatter_copy(sidx_ref[0, 0, r], r, slot).start()
                return carry
            lax.fori_loop(0, n_here, body, 0)

    @pl.when(i == ntiles - 1)
    def _():
        @pl.when(n_here > 0)
        def _():
            for r in range(rows):
                gather_copy(0, r, 1 - slot).wait()
        scatter_wait(nvalid_ref[jnp.maximum(i - 1, 0)], 1 - slot)
        scatter_wait(n_here, slot)


def _route_plan(keys, rows):
    t = keys.shape[0]
    ntiles = (t + N_BUCKETS * (rows - 1)) // rows
    kk = jnp.arange(N_BUCKETS, dtype=jnp.int32)
    cnt = jnp.sum((keys[None, :] == kk[:, None]).astype(jnp.int32), axis=1)
    off = jnp.cumsum(cnt) - cnt
    tiles_k = (cnt + rows - 1) // rows
    tile_end = jnp.cumsum(tiles_k)
    tile_off = tile_end - tiles_k
    order = jnp.argsort(keys, stable=True).astype(jnp.int32)
    tile = jnp.arange(ntiles, dtype=jnp.int32)
    k_of_tile = jnp.minimum(jnp.sum((tile[:, None] >= tile_end[None, :]).astype(jnp.int32), axis=1), N_BUCKETS - 1)
    tile_in_k = tile - tile_off[k_of_tile]
    nvalid = jnp.clip(cnt[k_of_tile] - tile_in_k * rows, 0, rows)
    nvalid = jnp.where(tile < tile_end[-1], nvalid, 0).astype(jnp.int32)
    j = jnp.arange(rows, dtype=jnp.int32)
    pos = off[k_of_tile][:, None] + tile_in_k[:, None] * rows + j[None, :]
    tok = order[jnp.clip(pos, 0, t - 1)]
    valid = j[None, :] < nvalid[:, None]
    gidx = jnp.where(valid, tok, 0).astype(jnp.int32).reshape(ntiles, 1, rows)
    g = k_of_tile // PAIRS_PER_GROUP
    pair = k_of_tile % PAIRS_PER_GROUP
    pair_lo = jnp.array([0, 0, 0, 1, 1, 2], jnp.int32)
    pair_hi = jnp.array([1, 2, 3, 2, 3, 3], jnp.int32)
    ea = (g * MOE_PER_GROUP + pair_lo[pair]).astype(jnp.int32)
    eb = (g * MOE_PER_GROUP + pair_hi[pair]).astype(jnp.int32)
    return ea, eb, nvalid, gidx


def _moe_sparse(text, keys, wg, wu, wd):
    t, dw = text.shape
    d = dw - LANES
    ff = wg.shape[2]
    rows = MOE_ROWS
    ea, eb, nvalid, gidx = _route_plan(keys, rows)
    ntiles = gidx.shape[0]
    cur = lambda i, *_: (i, 0, 0)
    nxt = lambda i, *_: (jnp.minimum(i + 1, ntiles - 1), 0, 0)
    idx_spec = lambda m: pl.BlockSpec((1, 1, rows), m, memory_space=pltpu.SMEM)
    wa_up = pl.BlockSpec((1, d, ff), lambda i, ea, eb, nv: (ea[i], 0, 0))
    wb_up = pl.BlockSpec((1, d, ff), lambda i, ea, eb, nv: (eb[i], 0, 0))
    wa_dn = pl.BlockSpec((1, ff, d), lambda i, ea, eb, nv: (ea[i], 0, 0))
    wb_dn = pl.BlockSpec((1, ff, d), lambda i, ea, eb, nv: (eb[i], 0, 0))
    grid_spec = pltpu.PrefetchScalarGridSpec(
        num_scalar_prefetch=3,
        grid=(ntiles,),
        in_specs=[idx_spec(cur), idx_spec(nxt), idx_spec(cur), pl.BlockSpec(memory_space=pl.ANY),
                  wa_up, wa_up, wa_dn, wb_up, wb_up, wb_dn],
        out_specs=pl.BlockSpec(memory_space=pl.ANY),
        scratch_shapes=[pltpu.VMEM((2, rows, dw), F32), pltpu.VMEM((2, rows, d), F32),
                        pltpu.SemaphoreType.DMA((2,)), pltpu.SemaphoreType.DMA((2,))])
    return pl.pallas_call(
        functools.partial(_moe_sparse_kernel, rows=rows, ntiles=ntiles, d=d),
        out_shape=jax.ShapeDtypeStruct((t, d), F32),
        grid_spec=grid_spec,
        compiler_params=_cparams("arbitrary"),
        name="moe_sparse",
    )(ea, eb, nvalid, gidx, gidx, gidx, text, wg, wu, wd, wg, wu, wd)


def _final_kernel(h_ref, moe_ref, gt_ref, g_ref, o_ref):
    h = h_ref[...] + gt_ref[0] * moe_ref[...]
    o_ref[...] = h * lax.rsqrt(jnp.mean(h * h, axis=-1, keepdims=True) + RMS_EPS) * g_ref[...]


def _final_norm(h, moe, gate, g, seq, tm):
    t, d = h.shape
    tpb = seq // tm
    tok = pl.BlockSpec((tm, d), lambda i: (i, 0))
    return pl.pallas_call(
        _final_kernel,
        out_shape=jax.ShapeDtypeStruct((t, d), F32),
        grid=(t // tm,),
        in_specs=[tok, tok, pl.BlockSpec((1, 1, d), lambda i: (i // tpb, 0, 0)),
                  pl.BlockSpec((1, d), lambda i: (0, 0))],
        out_specs=tok,
        compiler_params=_cparams("arbitrary"),
        name="final_norm",
    )(h, moe, gate, g)


def _pad_rows(w, n):
    return jnp.zeros((n, w.shape[1]), w.dtype).at[:w.shape[0]].set(w)


def kernel(x, c, ctx, c_ctx, w_ada, b_ada, g_mix, g_ffn, w_in, conv_w, conv_b, conv_ln_g, conv_ln_b,
           ssd_conv_w, ssd_conv_b, dt_bias, a_log, d_skip, ssd_norm_g, w_out, w_router_group,
           b_router_group, w_router_expert, b_router_expert, w_gate, w_up, w_down, g_final):
    bsz, seq, d = x.shape
    lc = ctx.shape[1]
    depth = w_in.shape[0]
    lo = 2 * D_CONV + D_SSD
    n_main = lo + D_XBC
    tm_x, tm_c = 512, lc
    consts = _ssd_consts()
    gw = (SSD_HEADS // SSD_GROUPS) * SSD_HEADDIM
    zero_state = jnp.zeros((bsz, 2, SSD_GROUPS, SSD_STATE, gw), F32)

    h_x = x.reshape(bsz * seq, d)
    h_c = ctx.reshape(bsz * lc, d)
    cvecs = jnp.zeros((SUBLANES, d), F32).at[:bsz].set(c).at[bsz].set(c_ctx)
    row = lambda v: v.reshape(1, -1)
    pending = None

    for l in range(depth):
        last = l == depth - 1
        mods = _ada(cvecs, w_ada[l], b_ada[l]).reshape(SUBLANES, 6, d)
        mx = [mods[:bsz, j][:, None, :] for j in range(6)]
        mc = [jnp.broadcast_to(mods[bsz, j][None, None, :], (bsz, 1, d)) for j in range(6)]
        w_main = w_in[l][:, :n_main].astype(BF16)
        w_dt = jnp.zeros((d, LANES), F32).at[:, :2 * SSD_HEADS].set(w_in[l][:, n_main:]).astype(BF16)
        dtb_row = jnp.zeros((1, LANES), F32).at[0, :2 * SSD_HEADS].set(dt_bias[l].reshape(-1))
        a_row = jnp.zeros((1, LANES), F32).at[0, :2 * SSD_HEADS].set(-jnp.exp(a_log[l].astype(F32)).reshape(-1))
        dskip_row = jnp.repeat(d_skip[l], SSD_HEADDIM).reshape(1, D_SSD)
        convw_pad = _pad_rows(conv_w[l], 32)
        wout_b = w_out[l].astype(BF16)
        wr = jnp.zeros((d, LANES), F32).at[:, :MOE_GROUPS].set(w_router_group[l])
        wr = wr.at[:, ROUTER_LANE0:ROUTER_LANE0 + N_EXPERTS].set(w_router_expert[l]).astype(BF16)
        br = jnp.zeros((1, LANES), F32).at[0, :MOE_GROUPS].set(b_router_group[l])
        br = br.at[0, ROUTER_LANE0:ROUTER_LANE0 + N_EXPERTS].set(b_router_expert[l])
        wg_b, wu_b, wd_b = w_gate[l].astype(BF16), w_up[l].astype(BF16), w_down[l].astype(BF16)

        def out_proj(h, m, s, tm, yc, yr, yf, yb, z):
            return _out_proj(yc, yr, yf, yb, z, h, m[2], row(conv_ln_g[l]), row(conv_ln_b[l]),
                             row(ssd_norm_g[l]), wout_b, row(g_ffn[l]), m[3], m[4], wr, br, s, tm)

        v_c, z_c, xbc_raw_c, dt_raw_c = _in_proj(h_c, mc[0], mc[1], row(g_mix[l]), w_main, w_dt, lc, tm_c)
        xbc_c, dt_c = _xbc_conv(xbc_raw_c, ssd_conv_w[l], ssd_conv_b[l], dt_raw_c, dtb_row, lc, tm_c)
        yf_c, yb_c, h_ctx = _ssd(xbc_c, dt_c, a_row, dskip_row, zero_state, consts, bsz, lc)

        if pending is None:
            v, z, xbc_raw, dt_raw = _in_proj(h_x, mx[0], mx[1], row(g_mix[l]), w_main, w_dt, seq, tm_x)
        else:
            h_x, v, z, xbc_raw, dt_raw = _in_proj(h_x, mx[0], mx[1], row(g_mix[l]), w_main, w_dt, seq, tm_x,
                                                  moe=pending[0], gate=pending[1])
        xbc, dt = _xbc_conv(xbc_raw, ssd_conv_w[l], ssd_conv_b[l], dt_raw, dtb_row, seq, tm_x)
        yf, yb, _ = _ssd(xbc, dt, a_row, dskip_row, h_ctx, consts, bsz, seq)
        half = D_CONV // 2
        yc = _seg_conv(v, convw_pad, row(conv_b[l]), GRID_W, half, tm_x)
        yr = _row_conv(v.reshape(bsz, seq, D_CONV), convw_pad, row(conv_b[l]), half // LANES,
                       half // LANES).reshape(bsz * seq, half)
        h_x, text, _, keys_t = out_proj(h_x, mx, seq, tm_x, yc, yr, yf, yb, z)
        moe_out = _moe_sparse(text, keys_t[META_KEY].astype(jnp.int32), wg_b, wu_b, wd_b)
        pending = (moe_out, mx[5])

        if not last:
            y_all = _seg_conv(v_c, convw_pad, row(conv_b[l]), lc, D_CONV, tm_c)
            h_c, text_c, comb_c, _ = out_proj(h_c, mc, lc, tm_c, y_all[:, :half], y_all[:, half:], yf_c, yb_c, z_c)
            h_c = _moe(text_c[:, :d], comb_c, h_c, mc[5], wg_b, wu_b, wd_b, lc, tm_c)

    out = _final_norm(h_x, pending[0], pending[1], row(g_final), seq, tm_x)
    return out.reshape(bsz, seq, d).astype(x.dtype)
```

```python
import functools

import jax
import jax.numpy as jnp
from jax import lax
from jax.experimental import pallas as pl
from jax.experimental.pallas import tpu as pltpu

F32 = jnp.float32
BF16 = jnp.bfloat16

RMS_EPS = 1e-6
LN_EPS = 1e-5

GRID_W = 64
D_CONV = 512
D_SSD = 512
SSD_HEADS = 8
SSD_HEADDIM = 64
SSD_GROUPS = 2
SSD_STATE = 128
D_BC = SSD_GROUPS * SSD_STATE
D_XBC = D_SSD + 2 * D_BC
MOE_GROUPS = 4
MOE_PER_GROUP = 4
N_EXPERTS = MOE_GROUPS * MOE_PER_GROUP

LANES = 128
SUBLANES = 8
CONV_PAD = 16
SSD_CHUNK = 128
ROUTER_LANE0 = MOE_GROUPS
PAIRS_PER_GROUP = MOE_PER_GROUP * (MOE_PER_GROUP - 1) // 2
N_BUCKETS = MOE_GROUPS * PAIRS_PER_GROUP
META_WA, META_WB, META_KEY = 0, 1, 2
MOE_ROWS = 256
TEXT_PITCH = 9
OUT_PITCH = 8
VMEM_LIMIT = 56 * 1024 * 1024


def _cparams(*sem):
    return pltpu.CompilerParams(dimension_semantics=sem, vmem_limit_bytes=VMEM_LIMIT)


def _dot(a, b):
    return jnp.dot(a, b, preferred_element_type=F32)


def _silu(x):
    return x * jax.nn.sigmoid(x)


def _split_bf16(x):
    hi = x.astype(BF16)
    lo = (x - hi.astype(F32)).astype(BF16)
    return hi, lo


def _ada_kernel(c_ref, w_ref, b_ref, o_ref):
    c = c_ref[...]
    o_ref[0] = jnp.dot(_silu(c), w_ref[0], preferred_element_type=F32,
                       precision=lax.Precision.HIGHEST) + b_ref[0]


def _ada(cvecs, w_ada, b_ada):
    depth, d, n = w_ada.shape
    bn = 1536
    rows = cvecs.shape[0]
    return pl.pallas_call(
        _ada_kernel,
        out_shape=jax.ShapeDtypeStruct((depth, rows, n), F32),
        grid=(depth, n // bn),
        in_specs=[pl.BlockSpec((rows, d), lambda l, j: (0, 0)),
                  pl.BlockSpec((1, d, bn), lambda l, j: (l, 0, j)),
                  pl.BlockSpec((1, 1, bn), lambda l, j: (l, 0, j))],
        out_specs=pl.BlockSpec((1, rows, bn), lambda l, j: (l, 0, j)),
        compiler_params=_cparams("arbitrary", "arbitrary"),
        name="ada",
    )(cvecs, w_ada, b_ada.reshape(depth, 1, n))


def _in_kernel(*refs, merge, tm, tpb, k):
    if merge:
        (xp_ref, x_ref, xn_ref, mp_ref, m_ref, mn_ref, gt_ref, sh_ref, sc_ref, g_ref, w_ref, wdt_ref,
         cw_ref, cb_ref, dtb_ref, h_ref, v_ref, z_ref, xbc_ref, dt_ref, scr) = refs
    else:
        (xp_ref, x_ref, xn_ref, sh_ref, sc_ref, g_ref, w_ref, wdt_ref,
         cw_ref, cb_ref, dtb_ref, v_ref, z_ref, xbc_ref, dt_ref, scr) = refs
        mp_ref = m_ref = mn_ref = None
    d = x_ref.shape[1]
    i = pl.program_id(0)
    first = (i % tpb) == 0
    last = (i % tpb) == tpb - 1

    def stream(xr, mr, n):
        if not merge:
            return xr[...]
        return xr[...] + gt_ref[0] * _token_major_rows(mr, n, OUT_PITCH, d // LANES)

    x_mid = stream(x_ref, m_ref, tm)
    if merge:
        h_ref[...] = x_mid
    x = jnp.concatenate([stream(xp_ref, mp_ref, SUBLANES), x_mid, stream(xn_ref, mn_ref, SUBLANES)], axis=0)
    ms = jnp.mean(x * x, axis=-1, keepdims=True)
    h = x * lax.rsqrt(ms + RMS_EPS) * g_ref[...]
    h = (h * (1.0 + sc_ref[0]) + sh_ref[0]).astype(BF16)
    mid = slice(SUBLANES, SUBLANES + tm)
    xe = _dot(h, w_ref[:, 2 * D_CONV + D_SSD:])
    scr[0:SUBLANES, :] = jnp.where(first, 0.0, xe[0:SUBLANES])
    scr[mid, :] = xe[mid]
    scr[SUBLANES + tm:2 * SUBLANES + tm, :] = jnp.where(last, 0.0, xe[SUBLANES + tm:])
    u = _dot(h, w_ref[:, 0:D_CONV])[mid]
    gate = _dot(h, w_ref[:, D_CONV:2 * D_CONV])[mid]
    v_ref[...] = u * jax.nn.sigmoid(gate)
    z_ref[...] = _dot(h, w_ref[:, 2 * D_CONV:2 * D_CONV + D_SSD])[mid]
    xdt = _dot(h, wdt_ref[...])[mid] + dtb_ref[...]
    dt_ref[...] = jnp.maximum(xdt, 0.0) + jnp.log1p(jnp.exp(-jnp.abs(xdt)))
    acc = jnp.broadcast_to(cb_ref[...], (tm, D_XBC))
    for j in range(k):
        acc = acc + cw_ref[j:j + 1, :] * scr[pl.ds(SUBLANES + j - k // 2, tm), :]
    xbc_ref[...] = _silu(acc)


def _in_proj(x, shift, scale, g, w_main, w_dt, conv_w, conv_b, dt_bias, seq, tm, moe=None, gate=None):
    t, d = x.shape
    tpb = seq // tm
    n_main = w_main.shape[1]
    k = conv_w.shape[0]
    merge = moe is not None
    nb8 = tm // SUBLANES
    last8 = t // SUBLANES - 1
    tok = lambda i: (i, 0)
    prev = lambda i: (jnp.maximum(i * nb8 - 1, 0), 0)
    nxt = lambda i: (jnp.minimum((i + 1) * nb8, last8), 0)
    per_batch = lambda i: (i // tpb, 0, 0)
    const = lambda i: (0, 0)
    tok_d = pl.BlockSpec((tm, d), tok)
    mod = pl.BlockSpec((1, 1, d), per_batch)
    halo = lambda m: pl.BlockSpec((SUBLANES, d), m)
    mhalo = lambda m: pl.BlockSpec((SUBLANES * OUT_PITCH, LANES), m)
    outs = (jax.ShapeDtypeStruct((t, D_CONV), F32), jax.ShapeDtypeStruct((t, D_SSD), F32),
            jax.ShapeDtypeStruct((t, D_XBC), F32), jax.ShapeDtypeStruct((t, LANES), F32))
    out_specs = (pl.BlockSpec((tm, D_CONV), tok), pl.BlockSpec((tm, D_SSD), tok),
                 pl.BlockSpec((tm, D_XBC), tok), pl.BlockSpec((tm, LANES), tok))
    args = (x, x, x)
    in_specs = [halo(prev), tok_d, halo(nxt)]
    if merge:
        args += (moe, moe, moe, gate)
        in_specs += [mhalo(prev), pl.BlockSpec((tm * OUT_PITCH, LANES), tok), mhalo(nxt), mod]
        outs = (jax.ShapeDtypeStruct((t, d), F32),) + outs
        out_specs = (tok_d,) + out_specs
    w_pad = jnp.zeros((SUBLANES, D_XBC), F32).at[:k].set(conv_w)
    return pl.pallas_call(
        functools.partial(_in_kernel, merge=merge, tm=tm, tpb=tpb, k=k),
        out_shape=outs,
        grid=(t // tm,),
        in_specs=in_specs + [mod, mod, pl.BlockSpec((1, d), const),
                             pl.BlockSpec((d, n_main), const), pl.BlockSpec((d, LANES), const),
                             pl.BlockSpec((SUBLANES, D_XBC), const), pl.BlockSpec((1, D_XBC), const),
                             pl.BlockSpec((1, LANES), const)],
        out_specs=out_specs,
        scratch_shapes=[pltpu.VMEM((tm + 2 * SUBLANES, D_XBC), F32)],
        compiler_params=_cparams("arbitrary"),
        name="in_proj",
    )(*args, shift, scale, g, w_main, w_dt, w_pad, conv_b.reshape(1, D_XBC), dt_bias)


def _segconv_kernel(v_ref, w_ref, b_ref, o_ref, scr, *, seg, nseg, k, cb):
    zeros = jnp.zeros((CONV_PAD, cb), F32)
    scr[0:CONV_PAD, :] = zeros
    scr[CONV_PAD + seg:2 * CONV_PAD + seg, :] = zeros

    def body(s, carry):
        base = pl.multiple_of(s * seg, seg)
        scr[CONV_PAD:CONV_PAD + seg, :] = v_ref[pl.ds(base, seg), :]
        for c in range(cb // LANES):
            cs = slice(c * LANES, (c + 1) * LANES)
            acc = jnp.broadcast_to(b_ref[:, cs], (seg, LANES))
            for j in range(k):
                acc = acc + w_ref[j:j + 1, cs] * scr[pl.ds(CONV_PAD + j - k // 2, seg), cs]
            o_ref[pl.ds(base, seg), cs] = acc
        return carry

    lax.fori_loop(0, nseg, body, 0)


def _seg_conv(v, w_pad, b, seg, cb, tm):
    t = v.shape[0]
    k = 31
    tok = lambda i: (i, 0)
    const = lambda i: (0, 0)
    return pl.pallas_call(
        functools.partial(_segconv_kernel, seg=seg, nseg=tm // seg, k=k, cb=cb),
        out_shape=jax.ShapeDtypeStruct((t, cb), F32),
        grid=(t // tm,),
        in_specs=[pl.BlockSpec((tm, cb), tok), pl.BlockSpec((w_pad.shape[0], cb), const),
                  pl.BlockSpec((1, cb), const)],
        out_specs=pl.BlockSpec((tm, cb), tok),
        scratch_shapes=[pltpu.VMEM((seg + 2 * CONV_PAD, cb), F32)],
        compiler_params=_cparams("arbitrary"),
        name="seg_conv",
    )(v, w_pad, b)


def _rowconv_kernel(v_ref, w_ref, b_ref, o_ref, scr, *, width, rows, k):
    half = k // 2
    pad = jnp.zeros((half * width, LANES), F32)
    scr[0:half * width, :] = pad
    scr[(half + rows) * width:(2 * half + rows) * width, :] = pad
    scr[half * width:(half + rows) * width, :] = v_ref[0]

    def body(r, carry):
        acc = jnp.broadcast_to(b_ref[...], (width, LANES))
        for j in range(k):
            start = pl.multiple_of((r + j) * width, width)
            acc = acc + w_ref[j:j + 1, :] * scr[pl.ds(start, width), :]
        o_ref[0, pl.ds(pl.multiple_of(r * width, width), width), :] = acc
        return carry

    lax.fori_loop(0, rows, body, 0)


def _row_conv(v3, w_pad, b, lane_block0, nblk):
    bsz, seq, _ = v3.shape
    rows = seq // GRID_W
    k = 31
    return pl.pallas_call(
        functools.partial(_rowconv_kernel, width=GRID_W, rows=rows, k=k),
        out_shape=jax.ShapeDtypeStruct((bsz, seq, nblk * LANES), F32),
        grid=(bsz, nblk),
        in_specs=[pl.BlockSpec((1, seq, LANES), lambda bi, c: (bi, 0, lane_block0 + c)),
                  pl.BlockSpec((w_pad.shape[0], LANES), lambda bi, c: (0, lane_block0 + c)),
                  pl.BlockSpec((1, LANES), lambda bi, c: (0, lane_block0 + c))],
        out_specs=pl.BlockSpec((1, seq, LANES), lambda bi, c: (bi, 0, c)),
        scratch_shapes=[pltpu.VMEM(((rows + 2 * (k // 2)) * GRID_W, LANES), F32)],
        compiler_params=_cparams("arbitrary", "arbitrary"),
        name="row_conv",
    )(v3, w_pad, b)


def _ssd_kernel(xf_ref, xb_ref, dtf_ref, dtb_ref, a_ref, dskip_ref, h0_ref, tril_ref, triu_ref, e_ref,
                yf_ref, yb_ref, hfin_ref, s_ref, *, chunk, nchunks, bsz):
    i = pl.program_id(0)

    @pl.when(i == 0)
    def _():
        s_ref[...] = h0_ref[...]

    L = chunk
    hpg = SSD_HEADS // SSD_GROUPS
    gw = hpg * SSD_HEADDIM
    lane = lax.broadcasted_iota(jnp.int32, (L, LANES), 1)
    row_i = lax.broadcasted_iota(jnp.int32, (L, L), 0)
    col_i = lax.broadcasted_iota(jnp.int32, (L, L), 1)
    head_of_lane = lax.broadcasted_iota(jnp.int32, (L, gw), 1) // SSD_HEADDIM

    def direction(b, d, x_ref, dt_ref, y_ref):
        xall = x_ref[b]
        x = xall[:, :D_SSD]
        bm = xall[:, D_SSD:D_SSD + D_BC]
        cm = xall[:, D_SSD + D_BC:]
        dt = dt_ref[b]
        dmask = (lane >= SSD_HEADS * d) & (lane < SSD_HEADS * (d + 1))
        da = jnp.where(dmask, dt * a_ref[...], 0.0)
        tri2 = tril_ref[...] if d == 0 else triu_ref[...]
        cum = _dot(tri2, jnp.concatenate(_split_bf16(da), axis=0))
        tot = cum[L - 1:L, :] if d == 0 else cum[0:1, :]
        e_cum = jnp.where(dmask, jnp.exp(cum), 0.0)
        w_dt = jnp.where(dmask, jnp.exp(tot - cum) * dt, 0.0)
        e_tot = jnp.where(dmask[0:1], jnp.exp(tot), 0.0)
        stack = jnp.concatenate([e_cum, w_dt, jnp.broadcast_to(e_tot, (SUBLANES, LANES))], axis=0)
        ex = _dot(jnp.concatenate(_split_bf16(stack), axis=1), e_ref[...])
        e_cum_x = ex[0:L]
        w_dt_x = ex[L:2 * L]
        e_tot_x = ex[2 * L:2 * L + 1]
        cum_t = cum.T
        dt_t = dt.T
        xw = (x * w_dt_x).astype(BF16)
        bm_t = bm.T
        mask = (row_i >= col_i) if d == 0 else (row_i <= col_i)
        ys = []
        for g in range(SSD_GROUPS):
            cg = cm[:, g * SSD_STATE:(g + 1) * SSD_STATE].astype(BF16)
            bg = bm[:, g * SSD_STATE:(g + 1) * SSD_STATE].astype(BF16)
            cb = lax.dot_general(cg, bg, (((1,), (1,)), ((), ())), preferred_element_type=F32)
            xg = x[:, g * gw:(g + 1) * gw]
            gs, xs = [], []
            for hh in range(hpg):
                c = SSD_HEADS * d + hpg * g + hh
                seg = cum[:, c:c + 1] - cum_t[c:c + 1, :]
                gm = jnp.exp(jnp.where(mask, seg, -jnp.inf)) * cb * dt_t[c:c + 1, :]
                gs.append(gm.astype(BF16))
                xs.append(jnp.where(head_of_lane == hh, xg, 0.0).astype(BF16))
            y_diag = _dot(jnp.concatenate(gs, axis=1), jnp.concatenate(xs, axis=0))
            sg = s_ref[b, d, g]
            y_off = _dot(cg, sg.astype(BF16)) * e_cum_x[:, g * gw:(g + 1) * gw]
            s_ref[b, d, g] = (sg * e_tot_x[:, g * gw:(g + 1) * gw]
                              + _dot(bm_t[g * SSD_STATE:(g + 1) * SSD_STATE, :].astype(BF16),
                                     xw[:, g * gw:(g + 1) * gw]))
            ys.append(y_diag + y_off)
        y = jnp.concatenate(ys, axis=1)
        if d == 0:
            y = y + dskip_ref[...] * x
        y_ref[b] = y

    for b in range(bsz):
        direction(b, 0, xf_ref, dtf_ref, yf_ref)
        direction(b, 1, xb_ref, dtb_ref, yb_ref)

    @pl.when(i == nchunks - 1)
    def _():
        hfin_ref[...] = s_ref[...]


def _ssd(xbc, dt, a_row, dskip_row, h0, consts, bsz, seq):
    t = xbc.shape[0]
    L = SSD_CHUNK
    n = seq // L
    tril2, triu2, expand2 = consts
    xbc3 = xbc.reshape(bsz, seq, D_XBC)
    dt3 = dt.reshape(bsz, seq, LANES)
    fwd = lambda i: (0, i, 0)
    bwd = lambda i: (0, n - 1 - i, 0)
    const = lambda i: (0, 0)
    state = pl.BlockSpec(h0.shape, lambda i: (0, 0, 0, 0, 0))
    y_f, y_b, h_fin = pl.pallas_call(
        functools.partial(_ssd_kernel, chunk=L, nchunks=n, bsz=bsz),
        out_shape=(jax.ShapeDtypeStruct((bsz, seq, D_SSD), F32), jax.ShapeDtypeStruct((bsz, seq, D_SSD), F32),
                   jax.ShapeDtypeStruct(h0.shape, F32)),
        grid=(n,),
        in_specs=[pl.BlockSpec((bsz, L, D_XBC), fwd), pl.BlockSpec((bsz, L, D_XBC), bwd),
                  pl.BlockSpec((bsz, L, LANES), fwd), pl.BlockSpec((bsz, L, LANES), bwd),
                  pl.BlockSpec((1, LANES), const), pl.BlockSpec((1, D_SSD), const), state,
                  pl.BlockSpec((L, 2 * L), const), pl.BlockSpec((L, 2 * L), const),
                  pl.BlockSpec((2 * LANES, D_SSD), const)],
        out_specs=(pl.BlockSpec((bsz, L, D_SSD), fwd), pl.BlockSpec((bsz, L, D_SSD), bwd), state),
        scratch_shapes=[pltpu.VMEM(h0.shape, F32)],
        compiler_params=_cparams("arbitrary"),
        name="ssd_scan",
    )(xbc3, xbc3, dt3, dt3, a_row, dskip_row, h0, tril2, triu2, expand2)
    return y_f.reshape(t, D_SSD), y_b.reshape(t, D_SSD), h_fin


def _ssd_consts():
    L = SSD_CHUNK
    r = jnp.arange(L)
    tril = (r[:, None] >= r[None, :]).astype(BF16)
    triu = (r[:, None] <= r[None, :]).astype(BF16)
    c = jnp.arange(LANES)[:, None]
    col = jnp.arange(D_SSD)[None, :]
    expand = ((c < 2 * SSD_HEADS) & ((c % SSD_HEADS) == col // SSD_HEADDIM)).astype(BF16)
    return (jnp.concatenate([tril, tril], axis=1), jnp.concatenate([triu, triu], axis=1),
            jnp.concatenate([expand, expand], axis=0))


def _out_kernel(yc_ref, yr_ref, yf_ref, yb_ref, z_ref, hx_ref, gt1_ref, lng_ref, lnb_ref, ng_ref,
                wout_ref, gffn_ref, sh2_ref, sc2_ref, wr_ref, br_ref, hnew_ref, text_ref, comb_ref, keys_ref,
                *, cw):
    yc = yc_ref[...]
    yr = yr_ref[...]
    nconv = yc.shape[1] + yr.shape[1]
    mu = (jnp.sum(yc, axis=-1, keepdims=True) + jnp.sum(yr, axis=-1, keepdims=True)) / nconv
    dc = yc - mu
    dr = yr - mu
    var = (jnp.sum(dc * dc, axis=-1, keepdims=True) + jnp.sum(dr * dr, axis=-1, keepdims=True)) / nconv
    inv = lax.rsqrt(var + LN_EPS)
    a_c = _silu(dc * inv * lng_ref[:, 0:cw] + lnb_ref[:, 0:cw])
    a_r = _silu(dr * inv * lng_ref[:, cw:] + lnb_ref[:, cw:])
    y = (yf_ref[...] + yb_ref[...]) * _silu(z_ref[...])
    ssd = y * lax.rsqrt(jnp.mean(y * y, axis=-1, keepdims=True) + RMS_EPS) * ng_ref[...]
    mix = (_dot(a_c.astype(BF16), wout_ref[0:cw, :]) + _dot(a_r.astype(BF16), wout_ref[cw:D_CONV, :])
           + _dot(ssd.astype(BF16), wout_ref[D_CONV:, :]))
    h = hx_ref[...] + gt1_ref[0] * mix
    hnew_ref[...] = h
    t = h * lax.rsqrt(jnp.mean(h * h, axis=-1, keepdims=True) + RMS_EPS) * gffn_ref[...]
    t = t * (1.0 + sc2_ref[0]) + sh2_ref[0]
    tb = t.astype(BF16)

    logits = _dot(tb, wr_ref[...]) + br_ref[...]
    lane = lax.broadcasted_iota(jnp.int32, logits.shape, 1).astype(F32)
    big = float(LANES)
    ninf = -jnp.inf
    gl = jnp.where(lane < MOE_GROUPS, logits, ninf)
    gmax = jnp.max(gl, axis=-1, keepdims=True)
    gsel = jnp.min(jnp.where(gl == gmax, lane, big), axis=-1, keepdims=True)
    gprob = 1.0 / jnp.sum(jnp.exp(gl - gmax), axis=-1, keepdims=True)
    lo = ROUTER_LANE0 + gsel * MOE_PER_GROUP
    emask = (lane >= lo) & (lane < lo + MOE_PER_GROUP)
    el = jnp.where(emask, logits, ninf)
    e1 = jnp.max(el, axis=-1, keepdims=True)
    i1 = jnp.min(jnp.where(el == e1, lane, big), axis=-1, keepdims=True)
    el2 = jnp.where(lane == i1, ninf, el)
    e2 = jnp.max(el2, axis=-1, keepdims=True)
    i2 = jnp.min(jnp.where(el2 == e2, lane, big), axis=-1, keepdims=True)
    p2 = jnp.exp(e2 - e1)
    w1 = gprob / (1.0 + p2)
    w2 = gprob * p2 / (1.0 + p2)
    comb_ref[...] = jnp.where(lane == i1, w1, 0.0) + jnp.where(lane == i2, w2, 0.0)
    first = i1 < i2
    l_lo = jnp.minimum(i1, i2) - lo
    l_hi = jnp.maximum(i1, i2) - lo
    key = gsel * PAIRS_PER_GROUP + l_lo * (7.0 - l_lo) * 0.5 + (l_hi - l_lo - 1.0)
    meta = (jnp.where(lane == META_WA, jnp.where(first, w1, w2), 0.0)
            + jnp.where(lane == META_WB, jnp.where(first, w2, w1), 0.0)
            + jnp.where(lane == META_KEY, key, 0.0))
    tm, d = t.shape
    nchunk = d // LANES
    for c in range(nchunk):
        text_ref[pl.ds(c, tm, stride=TEXT_PITCH), :] = t[:, c * LANES:(c + 1) * LANES]
    text_ref[pl.ds(nchunk, tm, stride=TEXT_PITCH), :] = meta
    keys_ref[...] = meta.T[0:SUBLANES, :]


def _out_proj(yc, yr, yf, yb, z, hx, gt1, lng, lnb, ng, wout, gffn, sh2, sc2, wr, br, seq, tm):
    t, d = hx.shape
    cw = yc.shape[1]
    tpb = seq // tm
    tok = lambda i: (i, 0)
    per_batch = lambda i: (i // tpb, 0, 0)
    const = lambda i: (0, 0)
    row = lambda n: pl.BlockSpec((1, n), const)
    return pl.pallas_call(
        functools.partial(_out_kernel, cw=cw),
        out_shape=(jax.ShapeDtypeStruct((t, d), F32), jax.ShapeDtypeStruct((t * TEXT_PITCH, LANES), F32),
                   jax.ShapeDtypeStruct((t, LANES), F32), jax.ShapeDtypeStruct((SUBLANES, t), F32)),
        grid=(t // tm,),
        in_specs=[pl.BlockSpec((tm, cw), tok), pl.BlockSpec((tm, D_CONV - cw), tok),
                  pl.BlockSpec((tm, D_SSD), tok), pl.BlockSpec((tm, D_SSD), tok), pl.BlockSpec((tm, D_SSD), tok),
                  pl.BlockSpec((tm, d), tok), pl.BlockSpec((1, 1, d), per_batch),
                  row(D_CONV), row(D_CONV), row(D_SSD),
                  pl.BlockSpec((D_CONV + D_SSD, d), const), row(d),
                  pl.BlockSpec((1, 1, d), per_batch), pl.BlockSpec((1, 1, d), per_batch),
                  pl.BlockSpec((d, LANES), const), row(LANES)],
        out_specs=(pl.BlockSpec((tm, d), tok), pl.BlockSpec((tm * TEXT_PITCH, LANES), tok),
                   pl.BlockSpec((tm, LANES), tok), pl.BlockSpec((SUBLANES, tm), lambda i: (0, i))),
        compiler_params=_cparams("arbitrary"),
        name="out_proj",
    )(yc, yr, yf, yb, z, hx, gt1, lng, lnb, ng, wout, gffn, sh2, sc2, wr, br)


def _moe_kernel(t_ref, comb_ref, hx_ref, gt2_ref, wg_ref, wu_ref, wd_ref, o_ref, acc_ref):
    e = pl.program_id(1)

    @pl.when(e == 0)
    def _():
        acc_ref[...] = jnp.zeros_like(acc_ref)

    t = t_ref[...].astype(BF16)
    comb = comb_ref[...]
    lane = lax.broadcasted_iota(jnp.int32, comb.shape, 1)
    cw = jnp.sum(jnp.where(lane == e + ROUTER_LANE0, comb, 0.0), axis=-1, keepdims=True)
    hid = _silu(_dot(t, wg_ref[0])) * _dot(t, wu_ref[0])
    acc_ref[...] += _dot((hid * cw).astype(BF16), wd_ref[0])

    @pl.when(e == N_EXPERTS - 1)
    def _():
        o_ref[...] = hx_ref[...] + gt2_ref[0] * acc_ref[...]


def _moe(t_in, comb, hx, gt2, wg, wu, wd, seq, tm):
    t, d = hx.shape
    ff = wg.shape[2]
    tpb = seq // tm
    tok = lambda i, e: (i, 0)
    per_batch = lambda i, e: (i // tpb, 0, 0)
    return pl.pallas_call(
        _moe_kernel,
        out_shape=jax.ShapeDtypeStruct((t, d), F32),
        grid=(t // tm, N_EXPERTS),
        in_specs=[pl.BlockSpec((tm, d), tok), pl.BlockSpec((tm, LANES), tok), pl.BlockSpec((tm, d), tok),
                  pl.BlockSpec((1, 1, d), per_batch),
                  pl.BlockSpec((1, d, ff), lambda i, e: (e, 0, 0)),
                  pl.BlockSpec((1, d, ff), lambda i, e: (e, 0, 0)),
                  pl.BlockSpec((1, ff, d), lambda i, e: (e, 0, 0))],
        out_specs=pl.BlockSpec((tm, d), tok),
        scratch_shapes=[pltpu.VMEM((tm, d), F32)],
        compiler_params=_cparams("arbitrary", "arbitrary"),
        name="moe_dense",
    )(t_in, comb, hx, gt2, wg, wu, wd)


def _moe_sparse_kernel(ea_ref, eb_ref, nvalid_ref, idx_ref, idx_next_ref, text_hbm, *rest, rows, nsteps, d):
    weights = (rest[0:6], rest[6:12])
    out_hbm, gbuf, obuf, gsem, ssem = rest[12:]
    i = pl.program_id(0)
    nchunk = d // LANES

    def gather_copy(tok, r, s):
        return pltpu.make_async_copy(text_hbm.at[pl.ds(tok * TEXT_PITCH, TEXT_PITCH), :],
                                     gbuf.at[s, pl.ds(r * TEXT_PITCH, TEXT_PITCH), :], gsem.at[s])

    def scatter_copy(tok, r, s):
        return pltpu.make_async_copy(obuf.at[s, pl.ds(r * OUT_PITCH, OUT_PITCH), :],
                                     out_hbm.at[pl.ds(tok * OUT_PITCH, OUT_PITCH), :], ssem.at[s])

    def scatter_wait(n, s):
        @pl.when(n == rows)
        def _():
            for r in range(rows):
                scatter_copy(0, r, s).wait()

        @pl.when(jnp.logical_and(n > 0, n < rows))
        def _():
            def body(r, carry):
                scatter_copy(0, r, s).wait()
                return carry
            lax.fori_loop(0, n, body, 0)

    @pl.when(i == 0)
    def _():
        for r in range(rows):
            gather_copy(idx_ref[0, 0, r], r, 0).start()

    for s in range(2):
        tile = 2 * i + s
        wga_ref, wua_ref, wda_ref, wgb_ref, wub_ref, wdb_ref = weights[s]
        nxt_ref, nxt_half = (idx_ref, 1) if s == 0 else (idx_next_ref, 0)
        n_here = nvalid_ref[tile]

        @pl.when(jnp.logical_or(tile == 0, nvalid_ref[jnp.maximum(tile - 1, 0)] > 0))
        def _():
            for r in range(rows):
                gather_copy(0, r, s).wait()

        @pl.when(tile >= 2)
        def _():
            scatter_wait(nvalid_ref[jnp.maximum(tile - 2, 0)], s)

        @pl.when(n_here > 0)
        def _():
            for r in range(rows):
                gather_copy(nxt_ref[0, nxt_half, r], r, 1 - s).start()
            gs = gbuf.at[s]
            t = jnp.concatenate([gs[pl.ds(c, rows, stride=TEXT_PITCH), :].astype(BF16) for c in range(nchunk)],
                                axis=1)
            meta = gs[pl.ds(nchunk, rows, stride=TEXT_PITCH), :]
            wa = meta[:, META_WA:META_WA + 1]
            wb = meta[:, META_WB:META_WB + 1]
            ha = _silu(_dot(t, wga_ref[0])) * _dot(t, wua_ref[0]) * wa
            hb = _silu(_dot(t, wgb_ref[0])) * _dot(t, wub_ref[0]) * wb
            out = _dot(ha.astype(BF16), wda_ref[0]) + _dot(hb.astype(BF16), wdb_ref[0])
            os_ = obuf.at[s]
            for c in range(nchunk):
                os_[pl.ds(c, rows, stride=OUT_PITCH), :] = out[:, c * LANES:(c + 1) * LANES]

            @pl.when(n_here == rows)
            def _():
                for r in range(rows):
                    scatter_copy(idx_ref[0, s, r], r, s).start()

            @pl.when(n_here < rows)
            def _():
                def body(r, carry):
                    scatter_copy(idx_ref[0, s, r], r, s).start()
                    return carry
                lax.fori_loop(0, n_here, body, 0)

    @pl.when(i == nsteps - 1)
    def _():
        @pl.when(nvalid_ref[2 * i + 1] > 0)
        def _():
            for r in range(rows):
                gather_copy(0, r, 0).wait()
        scatter_wait(nvalid_ref[2 * i], 0)
        scatter_wait(nvalid_ref[2 * i + 1], 1)


def _route_plan(keys, rows):
    t = keys.shape[0]
    ntiles = 2 * (((t + N_BUCKETS * (rows - 1)) // rows + 1) // 2)
    kk = jnp.arange(N_BUCKETS, dtype=jnp.int32)
    cnt = jnp.sum((keys[None, :] == kk[:, None]).astype(jnp.int32), axis=1)
    off = jnp.cumsum(cnt) - cnt
    tiles_k = (cnt + rows - 1) // rows
    tile_end = jnp.cumsum(tiles_k)
    tile_off = tile_end - tiles_k
    order = jnp.argsort(keys, stable=True).astype(jnp.int32)
    tile = jnp.arange(ntiles, dtype=jnp.int32)
    in_k = (tile[:, None] >= tile_off[None, :]) & ((tile[:, None] < tile_end[None, :]) | (kk[None, :] == N_BUCKETS - 1))
    in_k = in_k & ((tiles_k[None, :] > 0) | (kk[None, :] == N_BUCKETS - 1))
    pick = lambda v: jnp.sum(jnp.where(in_k, v[None, :], 0), axis=1)
    k_of_tile = pick(kk)
    tile_in_k = tile - pick(tile_off)
    nvalid = jnp.clip(pick(cnt) - tile_in_k * rows, 0, rows)
    nvalid = jnp.where(tile < tile_end[-1], nvalid, 0).astype(jnp.int32)
    j = jnp.arange(rows, dtype=jnp.int32)
    pos = pick(off)[:, None] + tile_in_k[:, None] * rows + j[None, :]
    tok = order[jnp.clip(pos, 0, t - 1)]
    valid = j[None, :] < nvalid[:, None]
    gidx = jnp.where(valid, tok, 0).astype(jnp.int32).reshape(ntiles // 2, 2, rows)
    g = k_of_tile // PAIRS_PER_GROUP
    pair = k_of_tile % PAIRS_PER_GROUP
    ge3 = (pair >= 3).astype(jnp.int32)
    ge5 = (pair >= 5).astype(jnp.int32)
    ea = (g * MOE_PER_GROUP + ge3 + ge5).astype(jnp.int32)
    eb = (g * MOE_PER_GROUP + pair + 1 - 2 * ge3 - ge5).astype(jnp.int32)
    return ea, eb, nvalid, gidx


def _moe_sparse(text, keys, wg, wu, wd):
    t = keys.shape[0]
    d, ff = wg.shape[1], wg.shape[2]
    rows = MOE_ROWS
    ea, eb, nvalid, gidx = _route_plan(keys, rows)
    nsteps = gidx.shape[0]
    cur = lambda i, *_: (i, 0, 0)
    nxt = lambda i, *_: (jnp.minimum(i + 1, nsteps - 1), 0, 0)
    idx_spec = lambda m: pl.BlockSpec((1, 2, rows), m, memory_space=pltpu.SMEM)

    def weight_specs(half):
        up = lambda e: pl.BlockSpec((1, d, ff), lambda i, ea, eb, nv: ((ea, eb)[e][2 * i + half], 0, 0))
        dn = lambda e: pl.BlockSpec((1, ff, d), lambda i, ea, eb, nv: ((ea, eb)[e][2 * i + half], 0, 0))
        return [up(0), up(0), dn(0), up(1), up(1), dn(1)]

    grid_spec = pltpu.PrefetchScalarGridSpec(
        num_scalar_prefetch=3,
        grid=(nsteps,),
        in_specs=[idx_spec(cur), idx_spec(nxt), pl.BlockSpec(memory_space=pl.ANY)]
                 + weight_specs(0) + weight_specs(1),
        out_specs=pl.BlockSpec(memory_space=pl.ANY),
        scratch_shapes=[pltpu.VMEM((2, rows * TEXT_PITCH, LANES), F32),
                        pltpu.VMEM((2, rows * OUT_PITCH, LANES), F32),
                        pltpu.SemaphoreType.DMA((2,)), pltpu.SemaphoreType.DMA((2,))])
    return pl.pallas_call(
        functools.partial(_moe_sparse_kernel, rows=rows, nsteps=nsteps, d=d),
        out_shape=jax.ShapeDtypeStruct((t * OUT_PITCH, LANES), F32),
        grid_spec=grid_spec,
        compiler_params=_cparams("arbitrary"),
        name="moe_sparse",
    )(ea, eb, nvalid, gidx, gidx, text, *([wg, wu, wd] * 4))


def _token_major_rows(ref, n, pitch, nchunk):
    return jnp.concatenate([ref[pl.ds(c, n, stride=pitch), :] for c in range(nchunk)], axis=1)


def _final_kernel(h_ref, moe_ref, gt_ref, g_ref, o_ref):
    tm, d = h_ref.shape
    h = h_ref[...] + gt_ref[0] * _token_major_rows(moe_ref, tm, OUT_PITCH, d // LANES)
    o_ref[...] = h * lax.rsqrt(jnp.mean(h * h, axis=-1, keepdims=True) + RMS_EPS) * g_ref[...]


def _final_norm(h, moe, gate, g, seq, tm):
    t, d = h.shape
    tpb = seq // tm
    tok = pl.BlockSpec((tm, d), lambda i: (i, 0))
    return pl.pallas_call(
        _final_kernel,
        out_shape=jax.ShapeDtypeStruct((t, d), F32),
        grid=(t // tm,),
        in_specs=[tok, pl.BlockSpec((tm * OUT_PITCH, LANES), lambda i: (i, 0)),
                  pl.BlockSpec((1, 1, d), lambda i: (i // tpb, 0, 0)),
                  pl.BlockSpec((1, d), lambda i: (0, 0))],
        out_specs=tok,
        compiler_params=_cparams("arbitrary"),
        name="final_norm",
    )(h, moe, gate, g)


def _pad_rows(w, n):
    return jnp.zeros((n, w.shape[1]), w.dtype).at[:w.shape[0]].set(w)


def kernel(x, c, ctx, c_ctx, w_ada, b_ada, g_mix, g_ffn, w_in, conv_w, conv_b, conv_ln_g, conv_ln_b,
           ssd_conv_w, ssd_conv_b, dt_bias, a_log, d_skip, ssd_norm_g, w_out, w_router_group,
           b_router_group, w_router_expert, b_router_expert, w_gate, w_up, w_down, g_final):
    bsz, seq, d = x.shape
    lc = ctx.shape[1]
    depth = w_in.shape[0]
    lo = 2 * D_CONV + D_SSD
    n_main = lo + D_XBC
    tm_x, tm_c = 512, lc
    consts = _ssd_consts()
    gw = (SSD_HEADS // SSD_GROUPS) * SSD_HEADDIM
    zero_state = jnp.zeros((bsz, 2, SSD_GROUPS, SSD_STATE, gw), F32)

    h_x = x.reshape(bsz * seq, d)
    h_c = ctx.reshape(bsz * lc, d)
    cvecs = jnp.zeros((SUBLANES, d), F32).at[:bsz].set(c).at[bsz].set(c_ctx)
    row = lambda v: v.reshape(1, -1)
    pending = None

    mods_all = _ada(cvecs, w_ada, b_ada)

    for l in range(depth):
        last = l == depth - 1
        mods = mods_all[l].reshape(SUBLANES, 6, d)
        mx = [mods[:bsz, j][:, None, :] for j in range(6)]
        mc = [jnp.broadcast_to(mods[bsz, j][None, None, :], (bsz, 1, d)) for j in range(6)]
        w_main = w_in[l][:, :n_main].astype(BF16)
        w_dt = jnp.zeros((d, LANES), F32).at[:, :2 * SSD_HEADS].set(w_in[l][:, n_main:]).astype(BF16)
        dtb_row = jnp.zeros((1, LANES), F32).at[0, :2 * SSD_HEADS].set(dt_bias[l].reshape(-1))
        a_row = jnp.zeros((1, LANES), F32).at[0, :2 * SSD_HEADS].set(-jnp.exp(a_log[l].astype(F32)).reshape(-1))
        dskip_row = jnp.repeat(d_skip[l], SSD_HEADDIM).reshape(1, D_SSD)
        convw_pad = _pad_rows(conv_w[l], 32)
        wout_b = w_out[l].astype(BF16)
        wr = jnp.zeros((d, LANES), F32).at[:, :MOE_GROUPS].set(w_router_group[l])
        wr = wr.at[:, ROUTER_LANE0:ROUTER_LANE0 + N_EXPERTS].set(w_router_expert[l]).astype(BF16)
        br = jnp.zeros((1, LANES), F32).at[0, :MOE_GROUPS].set(b_router_group[l])
        br = br.at[0, ROUTER_LANE0:ROUTER_LANE0 + N_EXPERTS].set(b_router_expert[l])
        wg_b, wu_b, wd_b = w_gate[l].astype(BF16), w_up[l].astype(BF16), w_down[l].astype(BF16)

        def out_proj(h, m, s, tm, yc, yr, yf, yb, z):
            return _out_proj(yc, yr, yf, yb, z, h, m[2], row(conv_ln_g[l]), row(conv_ln_b[l]),
                             row(ssd_norm_g[l]), wout_b, row(g_ffn[l]), m[3], m[4], wr, br, s, tm)

        in_w = (row(g_mix[l]), w_main, w_dt, ssd_conv_w[l], ssd_conv_b[l], dtb_row)
        v_c, z_c, xbc_c, dt_c = _in_proj(h_c, mc[0], mc[1], *in_w, lc, tm_c)
        yf_c, yb_c, h_ctx = _ssd(xbc_c, dt_c, a_row, dskip_row, zero_state, consts, bsz, lc)

        if pending is None:
            v, z, xbc, dt = _in_proj(h_x, mx[0], mx[1], *in_w, seq, tm_x)
        else:
            h_x, v, z, xbc, dt = _in_proj(h_x, mx[0], mx[1], *in_w, seq, tm_x, moe=pending[0], gate=pending[1])
        yf, yb, _ = _ssd(xbc, dt, a_row, dskip_row, h_ctx, consts, bsz, seq)
        half = D_CONV // 2
        yc = _seg_conv(v, convw_pad, row(conv_b[l]), GRID_W, half, tm_x)
        yr = _row_conv(v.reshape(bsz, seq, D_CONV), convw_pad, row(conv_b[l]), half // LANES,
                       half // LANES).reshape(bsz * seq, half)
        h_x, text, _, keys_t = out_proj(h_x, mx, seq, tm_x, yc, yr, yf, yb, z)
        moe_out = _moe_sparse(text, keys_t[META_KEY].astype(jnp.int32), wg_b, wu_b, wd_b)
        pending = (moe_out, mx[5])

        if not last:
            y_all = _seg_conv(v_c, convw_pad, row(conv_b[l]), lc, D_CONV, tm_c)
            h_c, text_c, comb_c, _ = out_proj(h_c, mc, lc, tm_c, y_all[:, :half], y_all[:, half:], yf_c, yb_c, z_c)
            t_c = text_c.reshape(bsz * lc, TEXT_PITCH, LANES)[:, :d // LANES].reshape(bsz * lc, d)
            h_c = _moe(t_c, comb_c, h_c, mc[5], wg_b, wu_b, wd_b, lc, tm_c)

    out = _final_norm(h_x, pending[0], pending[1], row(g_final), seq, tm_x)
    return out.reshape(bsz, seq, d).astype(x.dtype)
```

```python
import functools

import jax
import jax.numpy as jnp
from jax import lax
from jax.experimental import pallas as pl
from jax.experimental.pallas import tpu as pltpu

F32 = jnp.float32
BF16 = jnp.bfloat16

RMS_EPS = 1e-6
LN_EPS = 1e-5

GRID_W = 64
D_CONV = 512
D_SSD = 512
SSD_HEADS = 8
SSD_HEADDIM = 64
SSD_GROUPS = 2
SSD_STATE = 128
D_BC = SSD_GROUPS * SSD_STATE
D_XBC = D_SSD + 2 * D_BC
MOE_GROUPS = 4
MOE_PER_GROUP = 4
N_EXPERTS = MOE_GROUPS * MOE_PER_GROUP

LANES = 128
SUBLANES = 8
CONV_PAD = 16
SSD_CHUNK = 128
ROUTER_LANE0 = MOE_GROUPS
PAIRS_PER_GROUP = MOE_PER_GROUP * (MOE_PER_GROUP - 1) // 2
N_BUCKETS = MOE_GROUPS * PAIRS_PER_GROUP
META_WA, META_WB, META_KEY = 0, 1, 2
MOE_ROWS = 256
TEXT_PITCH = 9
OUT_PITCH = 8
VMEM_LIMIT = 56 * 1024 * 1024


def _cparams(*sem):
    return pltpu.CompilerParams(dimension_semantics=sem, vmem_limit_bytes=VMEM_LIMIT)


def _dot(a, b):
    return jnp.dot(a, b, preferred_element_type=F32)


def _silu(x):
    return x * jax.nn.sigmoid(x)


def _split_bf16(x):
    hi = x.astype(BF16)
    lo = (x - hi.astype(F32)).astype(BF16)
    return hi, lo


def _ada_kernel(c_ref, w_ref, b_ref, o_ref):
    c = c_ref[...]
    o_ref[0] = jnp.dot(_silu(c), w_ref[0], preferred_element_type=F32,
                       precision=lax.Precision.HIGHEST) + b_ref[0]


def _ada(cvecs, w_ada, b_ada):
    depth, d, n = w_ada.shape
    bn = 1536
    rows = cvecs.shape[0]
    return pl.pallas_call(
        _ada_kernel,
        out_shape=jax.ShapeDtypeStruct((depth, rows, n), F32),
        grid=(depth, n // bn),
        in_specs=[pl.BlockSpec((rows, d), lambda l, j: (0, 0)),
                  pl.BlockSpec((1, d, bn), lambda l, j: (l, 0, j)),
                  pl.BlockSpec((1, 1, bn), lambda l, j: (l, 0, j))],
        out_specs=pl.BlockSpec((1, rows, bn), lambda l, j: (l, 0, j)),
        compiler_params=_cparams("arbitrary", "arbitrary"),
        name="ada",
    )(cvecs, w_ada, b_ada.reshape(depth, 1, n))


def _in_kernel(*refs, merge, tm, tpb, k):
    if merge:
        (xp_ref, x_ref, xn_ref, mp_ref, m_ref, mn_ref, gt_ref, sh_ref, sc_ref, g_ref, w_ref, wdt_ref,
         cw_ref, cb_ref, dtb_ref, h_ref, v_ref, z_ref, xbc_ref, dt_ref, scr, oscr) = refs
    else:
        (xp_ref, x_ref, xn_ref, sh_ref, sc_ref, g_ref, w_ref, wdt_ref,
         cw_ref, cb_ref, dtb_ref, v_ref, z_ref, xbc_ref, dt_ref, scr, oscr) = refs
        mp_ref = m_ref = mn_ref = None
    d = x_ref.shape[1]
    i = pl.program_id(0)
    first = (i % tpb) == 0
    last = (i % tpb) == tpb - 1

    def stream(xr, mr, n):
        if not merge:
            return xr[...]
        return xr[...] + gt_ref[0] * _token_major_rows(mr, n, OUT_PITCH, d // LANES)

    x_mid = stream(x_ref, m_ref, tm)
    if merge:
        h_ref[...] = x_mid
    x = jnp.concatenate([stream(xp_ref, mp_ref, SUBLANES), x_mid, stream(xn_ref, mn_ref, SUBLANES)], axis=0)
    ms = jnp.mean(x * x, axis=-1, keepdims=True)
    h = x * lax.rsqrt(ms + RMS_EPS) * g_ref[...]
    h = (h * (1.0 + sc_ref[0]) + sh_ref[0]).astype(BF16)
    mid = slice(SUBLANES, SUBLANES + tm)
    xe = _dot(h, w_ref[:, 2 * D_CONV + D_SSD:])
    nslab = D_XBC // LANES
    for c in range(nslab):
        cs = slice(c * LANES, (c + 1) * LANES)
        scr[c, 0:SUBLANES, :] = jnp.where(first, 0.0, xe[0:SUBLANES, cs])
        scr[c, mid, :] = xe[mid, cs]
        scr[c, SUBLANES + tm:2 * SUBLANES + tm, :] = jnp.where(last, 0.0, xe[SUBLANES + tm:, cs])
    u = _dot(h, w_ref[:, 0:D_CONV])[mid]
    gate = _dot(h, w_ref[:, D_CONV:2 * D_CONV])[mid]
    v_ref[...] = u * jax.nn.sigmoid(gate)
    z_ref[...] = _dot(h, w_ref[:, 2 * D_CONV:2 * D_CONV + D_SSD])[mid]
    xdt = _dot(h, wdt_ref[...])[mid] + dtb_ref[...]
    dt_ref[...] = jnp.maximum(xdt, 0.0) + jnp.log1p(jnp.exp(-jnp.abs(xdt)))
    ng = tm // SUBLANES
    for c in range(nslab):
        cs = slice(c * LANES, (c + 1) * LANES)
        slab = scr.at[c]
        loaded = {}

        def phase(q, grp):
            if (q, grp) not in loaded:
                loaded[q, grp] = slab[pl.ds(SUBLANES * (1 + grp) + q, ng, stride=SUBLANES), :]
            return loaded[q, grp]

        for p in range(SUBLANES):
            acc = jnp.broadcast_to(cb_ref[:, cs], (ng, LANES))
            for j in range(k):
                off = p + j - k // 2
                acc = acc + cw_ref[j:j + 1, cs] * phase(off % SUBLANES, off // SUBLANES)
            oscr[c, pl.ds(p, ng, stride=SUBLANES), :] = _silu(acc)
        xbc_ref[:, cs] = oscr[c]


def _in_proj(x, shift, scale, g, w_main, w_dt, conv_w, conv_b, dt_bias, seq, tm, moe=None, gate=None):
    t, d = x.shape
    tpb = seq // tm
    n_main = w_main.shape[1]
    k = conv_w.shape[0]
    merge = moe is not None
    nb8 = tm // SUBLANES
    last8 = t // SUBLANES - 1
    tok = lambda i: (i, 0)
    prev = lambda i: (jnp.maximum(i * nb8 - 1, 0), 0)
    nxt = lambda i: (jnp.minimum((i + 1) * nb8, last8), 0)
    per_batch = lambda i: (i // tpb, 0, 0)
    const = lambda i: (0, 0)
    tok_d = pl.BlockSpec((tm, d), tok)
    mod = pl.BlockSpec((1, 1, d), per_batch)
    halo = lambda m: pl.BlockSpec((SUBLANES, d), m)
    mhalo = lambda m: pl.BlockSpec((SUBLANES * OUT_PITCH, LANES), m)
    outs = (jax.ShapeDtypeStruct((t, D_CONV), F32), jax.ShapeDtypeStruct((t, D_SSD), F32),
            jax.ShapeDtypeStruct((t, D_XBC), F32), jax.ShapeDtypeStruct((t, LANES), F32))
    out_specs = (pl.BlockSpec((tm, D_CONV), tok), pl.BlockSpec((tm, D_SSD), tok),
                 pl.BlockSpec((tm, D_XBC), tok), pl.BlockSpec((tm, LANES), tok))
    args = (x, x, x)
    in_specs = [halo(prev), tok_d, halo(nxt)]
    if merge:
        args += (moe, moe, moe, gate)
        in_specs += [mhalo(prev), pl.BlockSpec((tm * OUT_PITCH, LANES), tok), mhalo(nxt), mod]
        outs = (jax.ShapeDtypeStruct((t, d), F32),) + outs
        out_specs = (tok_d,) + out_specs
    w_pad = jnp.zeros((SUBLANES, D_XBC), F32).at[:k].set(conv_w)
    return pl.pallas_call(
        functools.partial(_in_kernel, merge=merge, tm=tm, tpb=tpb, k=k),
        out_shape=outs,
        grid=(t // tm,),
        in_specs=in_specs + [mod, mod, pl.BlockSpec((1, d), const),
                             pl.BlockSpec((d, n_main), const), pl.BlockSpec((d, LANES), const),
                             pl.BlockSpec((SUBLANES, D_XBC), const), pl.BlockSpec((1, D_XBC), const),
                             pl.BlockSpec((1, LANES), const)],
        out_specs=out_specs,
        scratch_shapes=[pltpu.VMEM((D_XBC // LANES, tm + 2 * SUBLANES, LANES), F32),
                        pltpu.VMEM((D_XBC // LANES, tm, LANES), F32)],
        compiler_params=_cparams("arbitrary"),
        name="in_proj",
    )(*args, shift, scale, g, w_main, w_dt, w_pad, conv_b.reshape(1, D_XBC), dt_bias)


def _segconv_kernel(v_ref, w_ref, b_ref, o_ref, scr, *, seg, nseg, k, cb):
    zeros = jnp.zeros((CONV_PAD, cb), F32)
    scr[0:CONV_PAD, :] = zeros
    scr[CONV_PAD + seg:2 * CONV_PAD + seg, :] = zeros

    def body(s, carry):
        base = pl.multiple_of(s * seg, seg)
        scr[CONV_PAD:CONV_PAD + seg, :] = v_ref[pl.ds(base, seg), :]
        for c in range(cb // LANES):
            cs = slice(c * LANES, (c + 1) * LANES)
            acc = jnp.broadcast_to(b_ref[:, cs], (seg, LANES))
            for j in range(k):
                acc = acc + w_ref[j:j + 1, cs] * scr[pl.ds(CONV_PAD + j - k // 2, seg), cs]
            o_ref[pl.ds(base, seg), cs] = acc
        return carry

    lax.fori_loop(0, nseg, body, 0)


def _seg_conv(v, w_pad, b, seg, cb, tm):
    t = v.shape[0]
    k = 31
    tok = lambda i: (i, 0)
    const = lambda i: (0, 0)
    return pl.pallas_call(
        functools.partial(_segconv_kernel, seg=seg, nseg=tm // seg, k=k, cb=cb),
        out_shape=jax.ShapeDtypeStruct((t, cb), F32),
        grid=(t // tm,),
        in_specs=[pl.BlockSpec((tm, cb), tok), pl.BlockSpec((w_pad.shape[0], cb), const),
                  pl.BlockSpec((1, cb), const)],
        out_specs=pl.BlockSpec((tm, cb), tok),
        scratch_shapes=[pltpu.VMEM((seg + 2 * CONV_PAD, cb), F32)],
        compiler_params=_cparams("arbitrary"),
        name="seg_conv",
    )(v, w_pad, b)


def _phaseconv_kernel(v_ref, w_ref, b_ref, o_ref, *, nseg, k):
    seg = SUBLANES * SUBLANES
    sub = lax.broadcasted_iota(jnp.int32, (SUBLANES, LANES), 0)

    def body(s, carry):
        base = pl.multiple_of(s * seg, seg)
        x = [v_ref[pl.ds(base + j, SUBLANES, stride=SUBLANES), :] for j in range(SUBLANES)]
        shifted = {}

        def tap(q, c):
            if c == 0:
                return x[q]
            if (q, c) not in shifted:
                moved = pltpu.roll(x[q], (-c) % SUBLANES, axis=0)
                shifted[q, c] = jnp.where((sub + c >= 0) & (sub + c < SUBLANES), moved, 0.0)
            return shifted[q, c]

        for j in range(SUBLANES):
            acc = jnp.broadcast_to(b_ref[...], (SUBLANES, LANES))
            for t in range(k):
                off = j + t - k // 2
                acc = acc + w_ref[t:t + 1, :] * tap(off % SUBLANES, off // SUBLANES)
            o_ref[pl.ds(base + j, SUBLANES, stride=SUBLANES), :] = acc
        return carry

    lax.fori_loop(0, nseg, body, 0)


def _phase_conv(v, w_pad, b, nblk, tm):
    t = v.shape[0]
    seg = SUBLANES * SUBLANES
    return pl.pallas_call(
        functools.partial(_phaseconv_kernel, nseg=tm // seg, k=31),
        out_shape=jax.ShapeDtypeStruct((t, nblk * LANES), F32),
        grid=(t // tm, nblk),
        in_specs=[pl.BlockSpec((tm, LANES), lambda i, c: (i, c)),
                  pl.BlockSpec((w_pad.shape[0], LANES), lambda i, c: (0, c)),
                  pl.BlockSpec((1, LANES), lambda i, c: (0, c))],
        out_specs=pl.BlockSpec((tm, LANES), lambda i, c: (i, c)),
        compiler_params=_cparams("arbitrary", "arbitrary"),
        name="phase_conv",
    )(v, w_pad, b)


def _rowconv_kernel(v_ref, w_ref, b_ref, o_ref, scr, *, width, rows, k):
    half = k // 2
    pad = jnp.zeros((half * width, LANES), F32)
    scr[0:half * width, :] = pad
    scr[(half + rows) * width:(2 * half + rows) * width, :] = pad
    scr[half * width:(half + rows) * width, :] = v_ref[0]

    def body(r, carry):
        acc = jnp.broadcast_to(b_ref[...], (width, LANES))
        for j in range(k):
            start = pl.multiple_of((r + j) * width, width)
            acc = acc + w_ref[j:j + 1, :] * scr[pl.ds(start, width), :]
        o_ref[0, pl.ds(pl.multiple_of(r * width, width), width), :] = acc
        return carry

    lax.fori_loop(0, rows, body, 0)


def _row_conv(v3, w_pad, b, lane_block0, nblk):
    bsz, seq, _ = v3.shape
    rows = seq // GRID_W
    k = 31
    return pl.pallas_call(
        functools.partial(_rowconv_kernel, width=GRID_W, rows=rows, k=k),
        out_shape=jax.ShapeDtypeStruct((bsz, seq, nblk * LANES), F32),
        grid=(bsz, nblk),
        in_specs=[pl.BlockSpec((1, seq, LANES), lambda bi, c: (bi, 0, lane_block0 + c)),
                  pl.BlockSpec((w_pad.shape[0], LANES), lambda bi, c: (0, lane_block0 + c)),
                  pl.BlockSpec((1, LANES), lambda bi, c: (0, lane_block0 + c))],
        out_specs=pl.BlockSpec((1, seq, LANES), lambda bi, c: (bi, 0, c)),
        scratch_shapes=[pltpu.VMEM(((rows + 2 * (k // 2)) * GRID_W, LANES), F32)],
        compiler_params=_cparams("arbitrary", "arbitrary"),
        name="row_conv",
    )(v3, w_pad, b)


def _ssd_kernel(xf_ref, xb_ref, dtf_ref, dtb_ref, a_ref, dskip_ref, h0_ref, tril_ref, triu_ref, e_ref,
                yf_ref, yb_ref, hfin_ref, s_ref, *, chunk, nchunks, bsz):
    i = pl.program_id(0)

    @pl.when(i == 0)
    def _():
        s_ref[...] = h0_ref[...]

    L = chunk
    hpg = SSD_HEADS // SSD_GROUPS
    gw = hpg * SSD_HEADDIM
    lane = lax.broadcasted_iota(jnp.int32, (L, LANES), 1)
    row_i = lax.broadcasted_iota(jnp.int32, (L, L), 0)
    col_i = lax.broadcasted_iota(jnp.int32, (L, L), 1)
    head_of_lane = lax.broadcasted_iota(jnp.int32, (L, gw), 1) // SSD_HEADDIM

    def direction(b, d, x_ref, dt_ref, y_ref):
        xall = x_ref[b]
        x = xall[:, :D_SSD]
        bm = xall[:, D_SSD:D_SSD + D_BC]
        cm = xall[:, D_SSD + D_BC:]
        dt = dt_ref[b]
        dmask = (lane >= SSD_HEADS * d) & (lane < SSD_HEADS * (d + 1))
        da = jnp.where(dmask, dt * a_ref[...], 0.0)
        tri2 = tril_ref[...] if d == 0 else triu_ref[...]
        cum = _dot(tri2, jnp.concatenate(_split_bf16(da), axis=0))
        tot = cum[L - 1:L, :] if d == 0 else cum[0:1, :]
        e_cum = jnp.where(dmask, jnp.exp(cum), 0.0)
        w_dt = jnp.where(dmask, jnp.exp(tot - cum) * dt, 0.0)
        e_tot = jnp.where(dmask[0:1], jnp.exp(tot), 0.0)
        stack = jnp.concatenate([e_cum, w_dt, jnp.broadcast_to(e_tot, (SUBLANES, LANES))], axis=0)
        ex = _dot(jnp.concatenate(_split_bf16(stack), axis=1), e_ref[...])
        e_cum_x = ex[0:L]
        w_dt_x = ex[L:2 * L]
        e_tot_x = ex[2 * L:2 * L + 1]
        cum_t = cum.T
        dt_t = dt.T
        xw = (x * w_dt_x).astype(BF16)
        bm_t = bm.T
        mask = (row_i >= col_i) if d == 0 else (row_i <= col_i)
        ys = []
        for g in range(SSD_GROUPS):
            cg = cm[:, g * SSD_STATE:(g + 1) * SSD_STATE].astype(BF16)
            bg = bm[:, g * SSD_STATE:(g + 1) * SSD_STATE].astype(BF16)
            cb = lax.dot_general(cg, bg, (((1,), (1,)), ((), ())), preferred_element_type=F32)
            xg = x[:, g * gw:(g + 1) * gw]
            gs, xs = [], []
            for hh in range(hpg):
                c = SSD_HEADS * d + hpg * g + hh
                seg = cum[:, c:c + 1] - cum_t[c:c + 1, :]
                gm = jnp.exp(jnp.where(mask, seg, -jnp.inf)) * cb * dt_t[c:c + 1, :]
                gs.append(gm.astype(BF16))
                xs.append(jnp.where(head_of_lane == hh, xg, 0.0).astype(BF16))
            y_diag = _dot(jnp.concatenate(gs, axis=1), jnp.concatenate(xs, axis=0))
            sg = s_ref[b, d, g]
            y_off = _dot(cg, sg.astype(BF16)) * e_cum_x[:, g * gw:(g + 1) * gw]
            s_ref[b, d, g] = (sg * e_tot_x[:, g * gw:(g + 1) * gw]
                              + _dot(bm_t[g * SSD_STATE:(g + 1) * SSD_STATE, :].astype(BF16),
                                     xw[:, g * gw:(g + 1) * gw]))
            ys.append(y_diag + y_off)
        y = jnp.concatenate(ys, axis=1)
        if d == 0:
            y = y + dskip_ref[...] * x
        y_ref[b] = y

    for b in range(bsz):
        direction(b, 0, xf_ref, dtf_ref, yf_ref)
        direction(b, 1, xb_ref, dtb_ref, yb_ref)

    @pl.when(i == nchunks - 1)
    def _():
        hfin_ref[...] = s_ref[...]


def _ssd(xbc, dt, a_row, dskip_row, h0, consts, bsz, seq):
    t = xbc.shape[0]
    L = SSD_CHUNK
    n = seq // L
    tril2, triu2, expand2 = consts
    xbc3 = xbc.reshape(bsz, seq, D_XBC)
    dt3 = dt.reshape(bsz, seq, LANES)
    fwd = lambda i: (0, i, 0)
    bwd = lambda i: (0, n - 1 - i, 0)
    const = lambda i: (0, 0)
    state = pl.BlockSpec(h0.shape, lambda i: (0, 0, 0, 0, 0))
    y_f, y_b, h_fin = pl.pallas_call(
        functools.partial(_ssd_kernel, chunk=L, nchunks=n, bsz=bsz),
        out_shape=(jax.ShapeDtypeStruct((bsz, seq, D_SSD), F32), jax.ShapeDtypeStruct((bsz, seq, D_SSD), F32),
                   jax.ShapeDtypeStruct(h0.shape, F32)),
        grid=(n,),
        in_specs=[pl.BlockSpec((bsz, L, D_XBC), fwd), pl.BlockSpec((bsz, L, D_XBC), bwd),
                  pl.BlockSpec((bsz, L, LANES), fwd), pl.BlockSpec((bsz, L, LANES), bwd),
                  pl.BlockSpec((1, LANES), const), pl.BlockSpec((1, D_SSD), const), state,
                  pl.BlockSpec((L, 2 * L), const), pl.BlockSpec((L, 2 * L), const),
                  pl.BlockSpec((2 * LANES, D_SSD), const)],
        out_specs=(pl.BlockSpec((bsz, L, D_SSD), fwd), pl.BlockSpec((bsz, L, D_SSD), bwd), state),
        scratch_shapes=[pltpu.VMEM(h0.shape, F32)],
        compiler_params=_cparams("arbitrary"),
        name="ssd_scan",
    )(xbc3, xbc3, dt3, dt3, a_row, dskip_row, h0, tril2, triu2, expand2)
    return y_f.reshape(t, D_SSD), y_b.reshape(t, D_SSD), h_fin


def _ssd_consts():
    L = SSD_CHUNK
    r = jnp.arange(L)
    tril = (r[:, None] >= r[None, :]).astype(BF16)
    triu = (r[:, None] <= r[None, :]).astype(BF16)
    c = jnp.arange(LANES)[:, None]
    col = jnp.arange(D_SSD)[None, :]
    expand = ((c < 2 * SSD_HEADS) & ((c % SSD_HEADS) == col // SSD_HEADDIM)).astype(BF16)
    return (jnp.concatenate([tril, tril], axis=1), jnp.concatenate([triu, triu], axis=1),
            jnp.concatenate([expand, expand], axis=0))


def _out_kernel(yc_ref, yr_ref, yf_ref, yb_ref, z_ref, hx_ref, gt1_ref, lng_ref, lnb_ref, ng_ref,
                wout_ref, gffn_ref, sh2_ref, sc2_ref, wr_ref, br_ref, hnew_ref, text_ref, comb_ref, keys_ref,
                *, cw):
    yc = yc_ref[...]
    yr = yr_ref[...]
    nconv = yc.shape[1] + yr.shape[1]
    mu = (jnp.sum(yc, axis=-1, keepdims=True) + jnp.sum(yr, axis=-1, keepdims=True)) / nconv
    dc = yc - mu
    dr = yr - mu
    var = (jnp.sum(dc * dc, axis=-1, keepdims=True) + jnp.sum(dr * dr, axis=-1, keepdims=True)) / nconv
    inv = lax.rsqrt(var + LN_EPS)
    a_c = _silu(dc * inv * lng_ref[:, 0:cw] + lnb_ref[:, 0:cw])
    a_r = _silu(dr * inv * lng_ref[:, cw:] + lnb_ref[:, cw:])
    y = (yf_ref[...] + yb_ref[...]) * _silu(z_ref[...])
    ssd = y * lax.rsqrt(jnp.mean(y * y, axis=-1, keepdims=True) + RMS_EPS) * ng_ref[...]
    mix = (_dot(a_c.astype(BF16), wout_ref[0:cw, :]) + _dot(a_r.astype(BF16), wout_ref[cw:D_CONV, :])
           + _dot(ssd.astype(BF16), wout_ref[D_CONV:, :]))
    h = hx_ref[...] + gt1_ref[0] * mix
    hnew_ref[...] = h
    t = h * lax.rsqrt(jnp.mean(h * h, axis=-1, keepdims=True) + RMS_EPS) * gffn_ref[...]
    t = t * (1.0 + sc2_ref[0]) + sh2_ref[0]
    tb = t.astype(BF16)

    logits = _dot(tb, wr_ref[...]) + br_ref[...]
    lane = lax.broadcasted_iota(jnp.int32, logits.shape, 1).astype(F32)
    big = float(LANES)
    ninf = -jnp.inf
    gl = jnp.where(lane < MOE_GROUPS, logits, ninf)
    gmax = jnp.max(gl, axis=-1, keepdims=True)
    gsel = jnp.min(jnp.where(gl == gmax, lane, big), axis=-1, keepdims=True)
    gprob = 1.0 / jnp.sum(jnp.exp(gl - gmax), axis=-1, keepdims=True)
    lo = ROUTER_LANE0 + gsel * MOE_PER_GROUP
    emask = (lane >= lo) & (lane < lo + MOE_PER_GROUP)
    el = jnp.where(emask, logits, ninf)
    e1 = jnp.max(el, axis=-1, keepdims=True)
    i1 = jnp.min(jnp.where(el == e1, lane, big), axis=-1, keepdims=True)
    el2 = jnp.where(lane == i1, ninf, el)
    e2 = jnp.max(el2, axis=-1, keepdims=True)
    i2 = jnp.min(jnp.where(el2 == e2, lane, big), axis=-1, keepdims=True)
    p2 = jnp.exp(e2 - e1)
    w1 = gprob / (1.0 + p2)
    w2 = gprob * p2 / (1.0 + p2)
    comb_ref[...] = jnp.where(lane == i1, w1, 0.0) + jnp.where(lane == i2, w2, 0.0)
    first = i1 < i2
    l_lo = jnp.minimum(i1, i2) - lo
    l_hi = jnp.maximum(i1, i2) - lo
    key = gsel * PAIRS_PER_GROUP + l_lo * (7.0 - l_lo) * 0.5 + (l_hi - l_lo - 1.0)
    meta = (jnp.where(lane == META_WA, jnp.where(first, w1, w2), 0.0)
            + jnp.where(lane == META_WB, jnp.where(first, w2, w1), 0.0)
            + jnp.where(lane == META_KEY, key, 0.0))
    tm, d = t.shape
    nchunk = d // LANES
    for c in range(nchunk):
        text_ref[pl.ds(c, tm, stride=TEXT_PITCH), :] = t[:, c * LANES:(c + 1) * LANES]
    text_ref[pl.ds(nchunk, tm, stride=TEXT_PITCH), :] = meta
    keys_ref[...] = meta.T[0:SUBLANES, :]


def _out_proj(yc, yr, yf, yb, z, hx, gt1, lng, lnb, ng, wout, gffn, sh2, sc2, wr, br, seq, tm):
    t, d = hx.shape
    cw = yc.shape[1]
    tpb = seq // tm
    tok = lambda i: (i, 0)
    per_batch = lambda i: (i // tpb, 0, 0)
    const = lambda i: (0, 0)
    row = lambda n: pl.BlockSpec((1, n), const)
    return pl.pallas_call(
        functools.partial(_out_kernel, cw=cw),
        out_shape=(jax.ShapeDtypeStruct((t, d), F32), jax.ShapeDtypeStruct((t * TEXT_PITCH, LANES), F32),
                   jax.ShapeDtypeStruct((t, LANES), F32), jax.ShapeDtypeStruct((SUBLANES, t), F32)),
        grid=(t // tm,),
        in_specs=[pl.BlockSpec((tm, cw), tok), pl.BlockSpec((tm, D_CONV - cw), tok),
                  pl.BlockSpec((tm, D_SSD), tok), pl.BlockSpec((tm, D_SSD), tok), pl.BlockSpec((tm, D_SSD), tok),
                  pl.BlockSpec((tm, d), tok), pl.BlockSpec((1, 1, d), per_batch),
                  row(D_CONV), row(D_CONV), row(D_SSD),
                  pl.BlockSpec((D_CONV + D_SSD, d), const), row(d),
                  pl.BlockSpec((1, 1, d), per_batch), pl.BlockSpec((1, 1, d), per_batch),
                  pl.BlockSpec((d, LANES), const), row(LANES)],
        out_specs=(pl.BlockSpec((tm, d), tok), pl.BlockSpec((tm * TEXT_PITCH, LANES), tok),
                   pl.BlockSpec((tm, LANES), tok), pl.BlockSpec((SUBLANES, tm), lambda i: (0, i))),
        compiler_params=_cparams("arbitrary"),
        name="out_proj",
    )(yc, yr, yf, yb, z, hx, gt1, lng, lnb, ng, wout, gffn, sh2, sc2, wr, br)


def _moe_kernel(t_ref, comb_ref, hx_ref, gt2_ref, wg_ref, wu_ref, wd_ref, o_ref, acc_ref):
    e = pl.program_id(1)

    @pl.when(e == 0)
    def _():
        acc_ref[...] = jnp.zeros_like(acc_ref)

    t = t_ref[...].astype(BF16)
    comb = comb_ref[...]
    lane = lax.broadcasted_iota(jnp.int32, comb.shape, 1)
    cw = jnp.sum(jnp.where(lane == e + ROUTER_LANE0, comb, 0.0), axis=-1, keepdims=True)
    hid = _silu(_dot(t, wg_ref[0])) * _dot(t, wu_ref[0])
    acc_ref[...] += _dot((hid * cw).astype(BF16), wd_ref[0])

    @pl.when(e == N_EXPERTS - 1)
    def _():
        o_ref[...] = hx_ref[...] + gt2_ref[0] * acc_ref[...]


def _moe(t_in, comb, hx, gt2, wg, wu, wd, seq, tm):
    t, d = hx.shape
    ff = wg.shape[2]
    tpb = seq // tm
    tok = lambda i, e: (i, 0)
    per_batch = lambda i, e: (i // tpb, 0, 0)
    return pl.pallas_call(
        _moe_kernel,
        out_shape=jax.ShapeDtypeStruct((t, d), F32),
        grid=(t // tm, N_EXPERTS),
        in_specs=[pl.BlockSpec((tm, d), tok), pl.BlockSpec((tm, LANES), tok), pl.BlockSpec((tm, d), tok),
                  pl.BlockSpec((1, 1, d), per_batch),
                  pl.BlockSpec((1, d, ff), lambda i, e: (e, 0, 0)),
                  pl.BlockSpec((1, d, ff), lambda i, e: (e, 0, 0)),
                  pl.BlockSpec((1, ff, d), lambda i, e: (e, 0, 0))],
        out_specs=pl.BlockSpec((tm, d), tok),
        scratch_shapes=[pltpu.VMEM((tm, d), F32)],
        compiler_params=_cparams("arbitrary", "arbitrary"),
        name="moe_dense",
    )(t_in, comb, hx, gt2, wg, wu, wd)


def _moe_sparse_kernel(ea_ref, eb_ref, nvalid_ref, idx_ref, idx_next_ref, text_hbm, *rest, rows, nsteps, d):
    weights = (rest[0:6], rest[6:12])
    out_hbm, gbuf, obuf, gsem, ssem = rest[12:]
    i = pl.program_id(0)
    nchunk = d // LANES

    def gather_copy(tok, r, s):
        return pltpu.make_async_copy(text_hbm.at[pl.ds(tok * TEXT_PITCH, TEXT_PITCH), :],
                                     gbuf.at[s, pl.ds(r * TEXT_PITCH, TEXT_PITCH), :], gsem.at[s])

    def scatter_copy(tok, r, s):
        return pltpu.make_async_copy(obuf.at[s, pl.ds(r * OUT_PITCH, OUT_PITCH), :],
                                     out_hbm.at[pl.ds(tok * OUT_PITCH, OUT_PITCH), :], ssem.at[s])

    def scatter_wait(n, s):
        @pl.when(n == rows)
        def _():
            for r in range(rows):
                scatter_copy(0, r, s).wait()

        @pl.when(jnp.logical_and(n > 0, n < rows))
        def _():
            def body(r, carry):
                scatter_copy(0, r, s).wait()
                return carry
            lax.fori_loop(0, n, body, 0)

    @pl.when(i == 0)
    def _():
        for r in range(rows):
            gather_copy(idx_ref[0, 0, r], r, 0).start(priority=r % 2)

    for s in range(2):
        tile = 2 * i + s
        wga_ref, wua_ref, wda_ref, wgb_ref, wub_ref, wdb_ref = weights[s]
        nxt_ref, nxt_half = (idx_ref, 1) if s == 0 else (idx_next_ref, 0)
        n_here = nvalid_ref[tile]

        @pl.when(jnp.logical_or(tile == 0, nvalid_ref[jnp.maximum(tile - 1, 0)] > 0))
        def _():
            for r in range(rows):
                gather_copy(0, r, s).wait()

        @pl.when(tile >= 2)
        def _():
            scatter_wait(nvalid_ref[jnp.maximum(tile - 2, 0)], s)

        @pl.when(n_here > 0)
        def _():
            for r in range(rows):
                gather_copy(nxt_ref[0, nxt_half, r], r, 1 - s).start(priority=r % 2)
            gs = gbuf.at[s]
            t = jnp.concatenate([gs[pl.ds(c, rows, stride=TEXT_PITCH), :].astype(BF16) for c in range(nchunk)],
                                axis=1)
            meta = gs[pl.ds(nchunk, rows, stride=TEXT_PITCH), :]
            wa = meta[:, META_WA:META_WA + 1]
            wb = meta[:, META_WB:META_WB + 1]
            ha = _silu(_dot(t, wga_ref[0])) * _dot(t, wua_ref[0]) * wa
            hb = _silu(_dot(t, wgb_ref[0])) * _dot(t, wub_ref[0]) * wb
            out = _dot(ha.astype(BF16), wda_ref[0]) + _dot(hb.astype(BF16), wdb_ref[0])
            os_ = obuf.at[s]
            for c in range(nchunk):
                os_[pl.ds(c, rows, stride=OUT_PITCH), :] = out[:, c * LANES:(c + 1) * LANES]

            @pl.when(n_here == rows)
            def _():
                for r in range(rows):
                    scatter_copy(idx_ref[0, s, r], r, s).start(priority=r % 2)

            @pl.when(n_here < rows)
            def _():
                def body(r, carry):
                    scatter_copy(idx_ref[0, s, r], r, s).start()
                    return carry
                lax.fori_loop(0, n_here, body, 0)

    @pl.when(i == nsteps - 1)
    def _():
        @pl.when(nvalid_ref[2 * i + 1] > 0)
        def _():
            for r in range(rows):
                gather_copy(0, r, 0).wait()
        scatter_wait(nvalid_ref[2 * i], 0)
        scatter_wait(nvalid_ref[2 * i + 1], 1)


def _route_plan(keys, rows):
    t = keys.shape[0]
    ntiles = 2 * (((t + N_BUCKETS * (rows - 1)) // rows + 1) // 2)
    kk = jnp.arange(N_BUCKETS, dtype=jnp.int32)
    cnt = jnp.sum((keys[None, :] == kk[:, None]).astype(jnp.int32), axis=1)
    off = jnp.cumsum(cnt) - cnt
    tiles_k = (cnt + rows - 1) // rows
    tile_end = jnp.cumsum(tiles_k)
    tile_off = tile_end - tiles_k
    order = jnp.argsort(keys, stable=True).astype(jnp.int32)
    tile = jnp.arange(ntiles, dtype=jnp.int32)
    in_k = (tile[:, None] >= tile_off[None, :]) & ((tile[:, None] < tile_end[None, :]) | (kk[None, :] == N_BUCKETS - 1))
    in_k = in_k & ((tiles_k[None, :] > 0) | (kk[None, :] == N_BUCKETS - 1))
    pick = lambda v: jnp.sum(jnp.where(in_k, v[None, :], 0), axis=1)
    k_of_tile = pick(kk)
    tile_in_k = tile - pick(tile_off)
    nvalid = jnp.clip(pick(cnt) - tile_in_k * rows, 0, rows)
    nvalid = jnp.where(tile < tile_end[-1], nvalid, 0).astype(jnp.int32)
    j = jnp.arange(rows, dtype=jnp.int32)
    pos = pick(off)[:, None] + tile_in_k[:, None] * rows + j[None, :]
    tok = order[jnp.clip(pos, 0, t - 1)]
    valid = j[None, :] < nvalid[:, None]
    gidx = jnp.where(valid, tok, 0).astype(jnp.int32).reshape(ntiles // 2, 2, rows)
    g = k_of_tile // PAIRS_PER_GROUP
    pair = k_of_tile % PAIRS_PER_GROUP
    ge3 = (pair >= 3).astype(jnp.int32)
    ge5 = (pair >= 5).astype(jnp.int32)
    ea = (g * MOE_PER_GROUP + ge3 + ge5).astype(jnp.int32)
    eb = (g * MOE_PER_GROUP + pair + 1 - 2 * ge3 - ge5).astype(jnp.int32)
    return ea, eb, nvalid, gidx


def _moe_sparse(text, keys, wg, wu, wd):
    t = keys.shape[0]
    d, ff = wg.shape[1], wg.shape[2]
    rows = MOE_ROWS
    ea, eb, nvalid, gidx = _route_plan(keys, rows)
    nsteps = gidx.shape[0]
    cur = lambda i, *_: (i, 0, 0)
    nxt = lambda i, *_: (jnp.minimum(i + 1, nsteps - 1), 0, 0)
    idx_spec = lambda m: pl.BlockSpec((1, 2, rows), m, memory_space=pltpu.SMEM)

    def weight_specs(half):
        up = lambda e: pl.BlockSpec((1, d, ff), lambda i, ea, eb, nv: ((ea, eb)[e][2 * i + half], 0, 0))
        dn = lambda e: pl.BlockSpec((1, ff, d), lambda i, ea, eb, nv: ((ea, eb)[e][2 * i + half], 0, 0))
        return [up(0), up(0), dn(0), up(1), up(1), dn(1)]

    grid_spec = pltpu.PrefetchScalarGridSpec(
        num_scalar_prefetch=3,
        grid=(nsteps,),
        in_specs=[idx_spec(cur), idx_spec(nxt), pl.BlockSpec(memory_space=pl.ANY)]
                 + weight_specs(0) + weight_specs(1),
        out_specs=pl.BlockSpec(memory_space=pl.ANY),
        scratch_shapes=[pltpu.VMEM((2, rows * TEXT_PITCH, LANES), F32),
                        pltpu.VMEM((2, rows * OUT_PITCH, LANES), F32),
                        pltpu.SemaphoreType.DMA((2,)), pltpu.SemaphoreType.DMA((2,))])
    return pl.pallas_call(
        functools.partial(_moe_sparse_kernel, rows=rows, nsteps=nsteps, d=d),
        out_shape=jax.ShapeDtypeStruct((t * OUT_PITCH, LANES), F32),
        grid_spec=grid_spec,
        compiler_params=_cparams("arbitrary"),
        name="moe_sparse",
    )(ea, eb, nvalid, gidx, gidx, text, *([wg, wu, wd] * 4))


def _token_major_rows(ref, n, pitch, nchunk):
    return jnp.concatenate([ref[pl.ds(c, n, stride=pitch), :] for c in range(nchunk)], axis=1)


def _final_kernel(h_ref, moe_ref, gt_ref, g_ref, o_ref):
    tm, d = h_ref.shape
    h = h_ref[...] + gt_ref[0] * _token_major_rows(moe_ref, tm, OUT_PITCH, d // LANES)
    o_ref[...] = h * lax.rsqrt(jnp.mean(h * h, axis=-1, keepdims=True) + RMS_EPS) * g_ref[...]


def _final_norm(h, moe, gate, g, seq, tm):
    t, d = h.shape
    tpb = seq // tm
    tok = pl.BlockSpec((tm, d), lambda i: (i, 0))
    return pl.pallas_call(
        _final_kernel,
        out_shape=jax.ShapeDtypeStruct((t, d), F32),
        grid=(t // tm,),
        in_specs=[tok, pl.BlockSpec((tm * OUT_PITCH, LANES), lambda i: (i, 0)),
                  pl.BlockSpec((1, 1, d), lambda i: (i // tpb, 0, 0)),
                  pl.BlockSpec((1, d), lambda i: (0, 0))],
        out_specs=tok,
        compiler_params=_cparams("arbitrary"),
        name="final_norm",
    )(h, moe, gate, g)


def _pad_rows(w, n):
    return jnp.zeros((n, w.shape[1]), w.dtype).at[:w.shape[0]].set(w)


def kernel(x, c, ctx, c_ctx, w_ada, b_ada, g_mix, g_ffn, w_in, conv_w, conv_b, conv_ln_g, conv_ln_b,
           ssd_conv_w, ssd_conv_b, dt_bias, a_log, d_skip, ssd_norm_g, w_out, w_router_group,
           b_router_group, w_router_expert, b_router_expert, w_gate, w_up, w_down, g_final):
    bsz, seq, d = x.shape
    lc = ctx.shape[1]
    depth = w_in.shape[0]
    lo = 2 * D_CONV + D_SSD
    n_main = lo + D_XBC
    tm_x, tm_c = 512, lc
    consts = _ssd_consts()
    gw = (SSD_HEADS // SSD_GROUPS) * SSD_HEADDIM
    zero_state = jnp.zeros((bsz, 2, SSD_GROUPS, SSD_STATE, gw), F32)

    h_x = x.reshape(bsz * seq, d)
    h_c = ctx.reshape(bsz * lc, d)
    cvecs = jnp.zeros((SUBLANES, d), F32).at[:bsz].set(c).at[bsz].set(c_ctx)
    row = lambda v: v.reshape(1, -1)
    pending = None

    mods_all = _ada(cvecs, w_ada, b_ada)

    for l in range(depth):
        last = l == depth - 1
        mods = mods_all[l].reshape(SUBLANES, 6, d)
        mx = [mods[:bsz, j][:, None, :] for j in range(6)]
        mc = [jnp.broadcast_to(mods[bsz, j][None, None, :], (bsz, 1, d)) for j in range(6)]
        w_main = w_in[l][:, :n_main].astype(BF16)
        w_dt = jnp.zeros((d, LANES), F32).at[:, :2 * SSD_HEADS].set(w_in[l][:, n_main:]).astype(BF16)
        dtb_row = jnp.zeros((1, LANES), F32).at[0, :2 * SSD_HEADS].set(dt_bias[l].reshape(-1))
        a_row = jnp.zeros((1, LANES), F32).at[0, :2 * SSD_HEADS].set(-jnp.exp(a_log[l].astype(F32)).reshape(-1))
        dskip_row = jnp.repeat(d_skip[l], SSD_HEADDIM).reshape(1, D_SSD)
        convw_pad = _pad_rows(conv_w[l], 32)
        wout_b = w_out[l].astype(BF16)
        wr = jnp.zeros((d, LANES), F32).at[:, :MOE_GROUPS].set(w_router_group[l])
        wr = wr.at[:, ROUTER_LANE0:ROUTER_LANE0 + N_EXPERTS].set(w_router_expert[l]).astype(BF16)
        br = jnp.zeros((1, LANES), F32).at[0, :MOE_GROUPS].set(b_router_group[l])
        br = br.at[0, ROUTER_LANE0:ROUTER_LANE0 + N_EXPERTS].set(b_router_expert[l])
        wg_b, wu_b, wd_b = w_gate[l].astype(BF16), w_up[l].astype(BF16), w_down[l].astype(BF16)

        def out_proj(h, m, s, tm, yc, yr, yf, yb, z):
            return _out_proj(yc, yr, yf, yb, z, h, m[2], row(conv_ln_g[l]), row(conv_ln_b[l]),
                             row(ssd_norm_g[l]), wout_b, row(g_ffn[l]), m[3], m[4], wr, br, s, tm)

        in_w = (row(g_mix[l]), w_main, w_dt, ssd_conv_w[l], ssd_conv_b[l], dtb_row)
        v_c, z_c, xbc_c, dt_c = _in_proj(h_c, mc[0], mc[1], *in_w, lc, tm_c)
        yf_c, yb_c, h_ctx = _ssd(xbc_c, dt_c, a_row, dskip_row, zero_state, consts, bsz, lc)

        if pending is None:
            v, z, xbc, dt = _in_proj(h_x, mx[0], mx[1], *in_w, seq, tm_x)
        else:
            h_x, v, z, xbc, dt = _in_proj(h_x, mx[0], mx[1], *in_w, seq, tm_x, moe=pending[0], gate=pending[1])
        yf, yb, _ = _ssd(xbc, dt, a_row, dskip_row, h_ctx, consts, bsz, seq)
        half = D_CONV // 2
        yc = _phase_conv(v, convw_pad, row(conv_b[l]), half // LANES, 4 * tm_x)
        yr = _row_conv(v.reshape(bsz, seq, D_CONV), convw_pad, row(conv_b[l]), half // LANES,
                       half // LANES).reshape(bsz * seq, half)
        h_x, text, _, keys_t = out_proj(h_x, mx, seq, tm_x, yc, yr, yf, yb, z)
        moe_out = _moe_sparse(text, keys_t[META_KEY].astype(jnp.int32), wg_b, wu_b, wd_b)
        pending = (moe_out, mx[5])

        if not last:
            y_all = _seg_conv(v_c, convw_pad, row(conv_b[l]), lc, D_CONV, tm_c)
            h_c, text_c, comb_c, _ = out_proj(h_c, mc, lc, tm_c, y_all[:, :half], y_all[:, half:], yf_c, yb_c, z_c)
            t_c = text_c.reshape(bsz * lc, TEXT_PITCH, LANES)[:, :d // LANES].reshape(bsz * lc, d)
            h_c = _moe(t_c, comb_c, h_c, mc[5], wg_b, wu_b, wd_b, bsz * lc, bsz * lc)

    out = _final_norm(h_x, pending[0], pending[1], row(g_final), seq, tm_x)
    return out.reshape(bsz, seq, d).astype(x.dtype)
```

```python
import functools

import jax
import jax.numpy as jnp
from jax import lax
from jax.experimental import pallas as pl
from jax.experimental.pallas import tpu as pltpu

F32 = jnp.float32
BF16 = jnp.bfloat16

RMS_EPS = 1e-6
LN_EPS = 1e-5

GRID_W = 64
D_CONV = 512
D_SSD = 512
SSD_HEADS = 8
SSD_HEADDIM = 64
SSD_GROUPS = 2
SSD_STATE = 128
D_BC = SSD_GROUPS * SSD_STATE
D_XBC = D_SSD + 2 * D_BC
MOE_GROUPS = 4
MOE_PER_GROUP = 4
N_EXPERTS = MOE_GROUPS * MOE_PER_GROUP

LANES = 128
SUBLANES = 8
CONV_PAD = 16
SSD_CHUNK = 128
ROUTER_LANE0 = MOE_GROUPS
PAIRS_PER_GROUP = MOE_PER_GROUP * (MOE_PER_GROUP - 1) // 2
N_BUCKETS = MOE_GROUPS * PAIRS_PER_GROUP
META_WA, META_WB, META_KEY = 0, 1, 2
MOE_ROWS = 256
TOKEN_ROWS = 8
TEXT_PITCH = TOKEN_ROWS + 1
OUT_PITCH = TOKEN_ROWS
VMEM_LIMIT = 56 * 1024 * 1024


def _cparams(*sem):
    return pltpu.CompilerParams(dimension_semantics=sem, vmem_limit_bytes=VMEM_LIMIT)


def _dot(a, b):
    return jnp.dot(a, b, preferred_element_type=F32)


def _silu(x):
    return x * jax.nn.sigmoid(x)


def _split_bf16(x):
    hi = x.astype(BF16)
    lo = (x - hi.astype(F32)).astype(BF16)
    return hi, lo


def _ada_kernel(c_ref, w_ref, b_ref, o_ref):
    c = c_ref[...]
    o_ref[0] = jnp.dot(_silu(c), w_ref[0], preferred_element_type=F32,
                       precision=lax.Precision.HIGHEST) + b_ref[0]


def _ada(cvecs, w_ada, b_ada):
    depth, d, n = w_ada.shape
    bn = 1536
    rows = cvecs.shape[0]
    return pl.pallas_call(
        _ada_kernel,
        out_shape=jax.ShapeDtypeStruct((depth, rows, n), F32),
        grid=(depth, n // bn),
        in_specs=[pl.BlockSpec((rows, d), lambda l, j: (0, 0)),
                  pl.BlockSpec((1, d, bn), lambda l, j: (l, 0, j)),
                  pl.BlockSpec((1, 1, bn), lambda l, j: (l, 0, j))],
        out_specs=pl.BlockSpec((1, rows, bn), lambda l, j: (l, 0, j)),
        compiler_params=_cparams("arbitrary", "arbitrary"),
        name="ada",
    )(cvecs, w_ada, b_ada.reshape(depth, 1, n))


def _in_kernel(*refs, merge, tm, tpb, k):
    if merge:
        (xp_ref, x_ref, xn_ref, mp_ref, m_ref, mn_ref, gt_ref, sh_ref, sc_ref, g_ref, w_ref, wdt_ref,
         cw_ref, cb_ref, dtb_ref, h_ref, v_ref, z_ref, xbc_ref, dt_ref, scr, oscr) = refs
    else:
        (xp_ref, x_ref, xn_ref, sh_ref, sc_ref, g_ref, w_ref, wdt_ref,
         cw_ref, cb_ref, dtb_ref, v_ref, z_ref, xbc_ref, dt_ref, scr, oscr) = refs
        mp_ref = m_ref = mn_ref = None
    i = pl.program_id(0)
    first = (i % tpb) == 0
    last = (i % tpb) == tpb - 1
    nparts = scr.shape[0]
    n = tm // nparts
    nslab = D_XBC // LANES
    ng = n // SUBLANES

    def stream(xr, mr, r0, rows):
        x = xr[r0:r0 + rows, :]
        if merge:
            x = x + gt_ref[0] * _load_token_major(mr, rows, OUT_PITCH, F32, row0=r0 * OUT_PITCH)
        return x

    def rows_part(p):
        lo, hi = p * n, (p + 1) * n
        x_mid = stream(x_ref, m_ref, lo, n)
        if merge:
            h_ref[lo:hi, :] = x_mid
        before = stream(xp_ref, mp_ref, 0, SUBLANES) if p == 0 else stream(x_ref, m_ref, lo - SUBLANES, SUBLANES)
        after = (stream(xn_ref, mn_ref, 0, SUBLANES) if p == nparts - 1
                 else stream(x_ref, m_ref, hi, SUBLANES))
        x = jnp.concatenate([before, x_mid, after], axis=0)
        ms = jnp.mean(x * x, axis=-1, keepdims=True)
        h = x * lax.rsqrt(ms + RMS_EPS) * g_ref[...]
        h = (h * (1.0 + sc_ref[0]) + sh_ref[0]).astype(BF16)
        mid = slice(SUBLANES, SUBLANES + n)
        yield
        xe = _dot(h, w_ref[:, 2 * D_CONV + D_SSD:])
        for c in range(nslab):
            cs = slice(c * LANES, (c + 1) * LANES)
            head = xe[0:SUBLANES, cs]
            tail = xe[SUBLANES + n:, cs]
            scr[p, c, 0:SUBLANES, :] = jnp.where(first, 0.0, head) if p == 0 else head
            scr[p, c, mid, :] = xe[mid, cs]
            scr[p, c, SUBLANES + n:2 * SUBLANES + n, :] = jnp.where(last, 0.0, tail) if p == nparts - 1 else tail
        yield
        u = _dot(h, w_ref[:, 0:D_CONV])[mid]
        gate = _dot(h, w_ref[:, D_CONV:2 * D_CONV])[mid]
        v_ref[lo:hi, :] = u * jax.nn.sigmoid(gate)
        yield
        z_ref[lo:hi, :] = _dot(h, w_ref[:, 2 * D_CONV:2 * D_CONV + D_SSD])[mid]
        xdt = _dot(h, wdt_ref[...])[mid] + dtb_ref[...]
        dt_ref[lo:hi, :] = jnp.maximum(xdt, 0.0) + jnp.log1p(jnp.exp(-jnp.abs(xdt)))
        yield
        for c in range(nslab):
            cs = slice(c * LANES, (c + 1) * LANES)
            slab = scr.at[p, c]
            loaded = {}

            def phase(q, grp):
                if (q, grp) not in loaded:
                    loaded[q, grp] = slab[pl.ds(SUBLANES * (1 + grp) + q, ng, stride=SUBLANES), :]
                return loaded[q, grp]

            for q in range(SUBLANES):
                acc = jnp.broadcast_to(cb_ref[:, cs], (ng, LANES))
                for j in range(k):
                    off = q + j - k // 2
                    acc = acc + cw_ref[j:j + 1, cs] * phase(off % SUBLANES, off // SUBLANES)
                oscr[p, c, pl.ds(q, ng, stride=SUBLANES), :] = _silu(acc)
            xbc_ref[lo:hi, cs] = oscr[p, c]
            if c % 2 == 1:
                yield

    parts = [rows_part(p) for p in range(nparts)]
    while parts:
        parts = [g for g in parts if next(g, StopIteration) is not StopIteration]


def _in_proj(x, shift, scale, g, w_main, w_dt, conv_w, conv_b, dt_bias, seq, tm, moe=None, gate=None):
    t, d = x.shape
    tpb = seq // tm
    n_main = w_main.shape[1]
    k = conv_w.shape[0]
    merge = moe is not None
    nparts = 1
    nb8 = tm // SUBLANES
    last8 = t // SUBLANES - 1
    tok = lambda i: (i, 0)
    prev = lambda i: (jnp.maximum(i * nb8 - 1, 0), 0)
    nxt = lambda i: (jnp.minimum((i + 1) * nb8, last8), 0)
    per_batch = lambda i: (i // tpb, 0, 0)
    const = lambda i: (0, 0)
    tok_d = pl.BlockSpec((tm, d), tok)
    mod = pl.BlockSpec((1, 1, d), per_batch)
    halo = lambda m: pl.BlockSpec((SUBLANES, d), m)
    mhalo = lambda m: pl.BlockSpec((SUBLANES * OUT_PITCH, LANES), m)
    outs = (jax.ShapeDtypeStruct((t, D_CONV), F32), jax.ShapeDtypeStruct((t, D_SSD), F32),
            jax.ShapeDtypeStruct((t, D_XBC), F32), jax.ShapeDtypeStruct((t, LANES), F32))
    out_specs = (pl.BlockSpec((tm, D_CONV), tok), pl.BlockSpec((tm, D_SSD), tok),
                 pl.BlockSpec((tm, D_XBC), tok), pl.BlockSpec((tm, LANES), tok))
    args = (x, x, x)
    in_specs = [halo(prev), tok_d, halo(nxt)]
    if merge:
        args += (moe, moe, moe, gate)
        in_specs += [mhalo(prev), pl.BlockSpec((tm * OUT_PITCH, LANES), tok), mhalo(nxt), mod]
        outs = (jax.ShapeDtypeStruct((t, d), F32),) + outs
        out_specs = (tok_d,) + out_specs
    w_pad = jnp.zeros((SUBLANES, D_XBC), F32).at[:k].set(conv_w)
    return pl.pallas_call(
        functools.partial(_in_kernel, merge=merge, tm=tm, tpb=tpb, k=k),
        out_shape=outs,
        grid=(t // tm,),
        in_specs=in_specs + [mod, mod, pl.BlockSpec((1, d), const),
                             pl.BlockSpec((d, n_main), const), pl.BlockSpec((d, LANES), const),
                             pl.BlockSpec((SUBLANES, D_XBC), const), pl.BlockSpec((1, D_XBC), const),
                             pl.BlockSpec((1, LANES), const)],
        out_specs=out_specs,
        scratch_shapes=[pltpu.VMEM((nparts, D_XBC // LANES, tm // nparts + 2 * SUBLANES, LANES), F32),
                        pltpu.VMEM((nparts, D_XBC // LANES, tm // nparts, LANES), F32)],
        compiler_params=_cparams("arbitrary"),
        name="in_proj",
    )(*args, shift, scale, g, w_main, w_dt, w_pad, conv_b.reshape(1, D_XBC), dt_bias)


def _segconv_kernel(v_ref, w_ref, b_ref, o_ref, scr, *, seg, nseg, k, cb):
    zeros = jnp.zeros((CONV_PAD, cb), F32)
    scr[0:CONV_PAD, :] = zeros
    scr[CONV_PAD + seg:2 * CONV_PAD + seg, :] = zeros

    def body(s, carry):
        base = pl.multiple_of(s * seg, seg)
        scr[CONV_PAD:CONV_PAD + seg, :] = v_ref[pl.ds(base, seg), :]
        for c in range(cb // LANES):
            cs = slice(c * LANES, (c + 1) * LANES)
            acc = jnp.broadcast_to(b_ref[:, cs], (seg, LANES))
            for j in range(k):
                acc = acc + w_ref[j:j + 1, cs] * scr[pl.ds(CONV_PAD + j - k // 2, seg), cs]
            o_ref[pl.ds(base, seg), cs] = acc
        return carry

    lax.fori_loop(0, nseg, body, 0)


def _seg_conv(v, w_pad, b, seg, cb, tm):
    t = v.shape[0]
    k = 31
    tok = lambda i: (i, 0)
    const = lambda i: (0, 0)
    return pl.pallas_call(
        functools.partial(_segconv_kernel, seg=seg, nseg=tm // seg, k=k, cb=cb),
        out_shape=jax.ShapeDtypeStruct((t, cb), F32),
        grid=(t // tm,),
        in_specs=[pl.BlockSpec((tm, cb), tok), pl.BlockSpec((w_pad.shape[0], cb), const),
                  pl.BlockSpec((1, cb), const)],
        out_specs=pl.BlockSpec((tm, cb), tok),
        scratch_shapes=[pltpu.VMEM((seg + 2 * CONV_PAD, cb), F32)],
        compiler_params=_cparams("arbitrary"),
        name="seg_conv",
    )(v, w_pad, b)


def _phaseconv_kernel(v_ref, w_ref, b_ref, o_ref, *, nseg, k):
    seg = SUBLANES * SUBLANES
    sub = lax.broadcasted_iota(jnp.int32, (SUBLANES, LANES), 0)

    def body(s, carry):
        base = pl.multiple_of(s * seg, seg)
        x = [v_ref[pl.ds(base + j, SUBLANES, stride=SUBLANES), :] for j in range(SUBLANES)]
        shifted = {}

        def tap(q, c):
            if c == 0:
                return x[q]
            if (q, c) not in shifted:
                moved = pltpu.roll(x[q], (-c) % SUBLANES, axis=0)
                shifted[q, c] = jnp.where((sub + c >= 0) & (sub + c < SUBLANES), moved, 0.0)
            return shifted[q, c]

        for j in range(SUBLANES):
            acc = jnp.broadcast_to(b_ref[...], (SUBLANES, LANES))
            for t in range(k):
                off = j + t - k // 2
                acc = acc + w_ref[t:t + 1, :] * tap(off % SUBLANES, off // SUBLANES)
            o_ref[pl.ds(base + j, SUBLANES, stride=SUBLANES), :] = acc
        return carry

    lax.fori_loop(0, nseg, body, 0)


def _phase_conv(v, w_pad, b, nblk, tm):
    t = v.shape[0]
    seg = SUBLANES * SUBLANES
    return pl.pallas_call(
        functools.partial(_phaseconv_kernel, nseg=tm // seg, k=31),
        out_shape=jax.ShapeDtypeStruct((t, nblk * LANES), F32),
        grid=(t // tm, nblk),
        in_specs=[pl.BlockSpec((tm, LANES), lambda i, c: (i, c)),
                  pl.BlockSpec((w_pad.shape[0], LANES), lambda i, c: (0, c)),
                  pl.BlockSpec((1, LANES), lambda i, c: (0, c))],
        out_specs=pl.BlockSpec((tm, LANES), lambda i, c: (i, c)),
        compiler_params=_cparams("arbitrary", "arbitrary"),
        name="phase_conv",
    )(v, w_pad, b)


def _rowconv_kernel(v_ref, w_ref, b_ref, o_ref, scr, *, width, rows, k):
    half = k // 2
    pad = jnp.zeros((half * width, LANES), F32)
    scr[0:half * width, :] = pad
    scr[(half + rows) * width:(2 * half + rows) * width, :] = pad
    scr[half * width:(half + rows) * width, :] = v_ref[0]

    def body(r, carry):
        acc = jnp.broadcast_to(b_ref[...], (width, LANES))
        for j in range(k):
            start = pl.multiple_of((r + j) * width, width)
            acc = acc + w_ref[j:j + 1, :] * scr[pl.ds(start, width), :]
        o_ref[0, pl.ds(pl.multiple_of(r * width, width), width), :] = acc
        return carry

    lax.fori_loop(0, rows, body, 0)


def _row_conv(v3, w_pad, b, lane_block0, nblk):
    bsz, seq, _ = v3.shape
    rows = seq // GRID_W
    k = 31
    return pl.pallas_call(
        functools.partial(_rowconv_kernel, width=GRID_W, rows=rows, k=k),
        out_shape=jax.ShapeDtypeStruct((bsz, seq, nblk * LANES), F32),
        grid=(bsz, nblk),
        in_specs=[pl.BlockSpec((1, seq, LANES), lambda bi, c: (bi, 0, lane_block0 + c)),
                  pl.BlockSpec((w_pad.shape[0], LANES), lambda bi, c: (0, lane_block0 + c)),
                  pl.BlockSpec((1, LANES), lambda bi, c: (0, lane_block0 + c))],
        out_specs=pl.BlockSpec((1, seq, LANES), lambda bi, c: (bi, 0, c)),
        scratch_shapes=[pltpu.VMEM(((rows + 2 * (k // 2)) * GRID_W, LANES), F32)],
        compiler_params=_cparams("arbitrary", "arbitrary"),
        name="row_conv",
    )(v3, w_pad, b)


def _ssd_kernel(xf_ref, xb_ref, dtf_ref, dtb_ref, a_ref, dskip_ref, h0_ref, tril_ref, triu_ref, e_ref,
                yf_ref, yb_ref, hfin_ref, s_ref, *, chunk, nchunks, bsz):
    i = pl.program_id(0)

    @pl.when(i == 0)
    def _():
        s_ref[...] = h0_ref[...]

    L = chunk
    hpg = SSD_HEADS // SSD_GROUPS
    gw = hpg * SSD_HEADDIM
    lane = lax.broadcasted_iota(jnp.int32, (L, LANES), 1)
    row_i = lax.broadcasted_iota(jnp.int32, (L, L), 0)
    col_i = lax.broadcasted_iota(jnp.int32, (L, L), 1)
    head_of_lane = lax.broadcasted_iota(jnp.int32, (L, gw), 1) // SSD_HEADDIM

    def direction(b, d, x_ref, dt_ref, y_ref):
        xall = x_ref[b]
        x = xall[:, :D_SSD]
        bm = xall[:, D_SSD:D_SSD + D_BC]
        cm = xall[:, D_SSD + D_BC:]
        dt = dt_ref[b]
        dmask = (lane >= SSD_HEADS * d) & (lane < SSD_HEADS * (d + 1))
        da = jnp.where(dmask, dt * a_ref[...], 0.0)
        tri2 = tril_ref[...] if d == 0 else triu_ref[...]
        cum = _dot(tri2, jnp.concatenate(_split_bf16(da), axis=0))
        yield
        tot = cum[L - 1:L, :] if d == 0 else cum[0:1, :]
        e_cum = jnp.where(dmask, jnp.exp(cum), 0.0)
        w_dt = jnp.where(dmask, jnp.exp(tot - cum) * dt, 0.0)
        e_tot = jnp.where(dmask[0:1], jnp.exp(tot), 0.0)
        stack = jnp.concatenate([e_cum, w_dt, jnp.broadcast_to(e_tot, (SUBLANES, LANES))], axis=0)
        ex = _dot(jnp.concatenate(_split_bf16(stack), axis=1), e_ref[...])
        yield
        e_cum_x = ex[0:L]
        w_dt_x = ex[L:2 * L]
        e_tot_x = ex[2 * L:2 * L + 1]
        src_t = (cum - jnp.log(dt)).T
        xw = (x * w_dt_x).astype(BF16)
        bm_t = bm.T
        yield
        mask = (row_i >= col_i) if d == 0 else (row_i <= col_i)
        ys = []
        for g in range(SSD_GROUPS):
            cg = cm[:, g * SSD_STATE:(g + 1) * SSD_STATE].astype(BF16)
            bg = bm[:, g * SSD_STATE:(g + 1) * SSD_STATE].astype(BF16)
            cb = lax.dot_general(cg, bg, (((1,), (1,)), ((), ())), preferred_element_type=F32).astype(BF16)
            yield
            xg = x[:, g * gw:(g + 1) * gw].astype(BF16)
            gs, xs = [], []
            for hh in range(hpg):
                c = SSD_HEADS * d + hpg * g + hh
                seg = cum[:, c:c + 1] - src_t[c:c + 1, :]
                gs.append(jnp.exp(jnp.where(mask, seg, -jnp.inf)).astype(BF16) * cb)
                xs.append(jnp.where(head_of_lane == hh, xg, jnp.zeros_like(xg)))
            y_diag = _dot(jnp.concatenate(gs, axis=1), jnp.concatenate(xs, axis=0))
            yield
            sg = s_ref[b, d, g]
            y_off = _dot(cg, sg.astype(BF16)) * e_cum_x[:, g * gw:(g + 1) * gw]
            s_ref[b, d, g] = (sg * e_tot_x[:, g * gw:(g + 1) * gw]
                              + _dot(bm_t[g * SSD_STATE:(g + 1) * SSD_STATE, :].astype(BF16),
                                     xw[:, g * gw:(g + 1) * gw]))
            ys.append(y_diag + y_off)
        y = jnp.concatenate(ys, axis=1)
        if d == 0:
            y = y + dskip_ref[...] * x
        y_ref[b] = y

    scans = [direction(b, d, refs[0], refs[1], refs[2]) for b in range(bsz)
             for d, refs in enumerate(((xf_ref, dtf_ref, yf_ref), (xb_ref, dtb_ref, yb_ref)))]
    while scans:
        scans = [s for s in scans if next(s, StopIteration) is not StopIteration]

    @pl.when(i == nchunks - 1)
    def _():
        hfin_ref[...] = s_ref[...]


def _ssd(xbc, dt, a_row, dskip_row, h0, consts, bsz, seq):
    t = xbc.shape[0]
    L = SSD_CHUNK
    n = seq // L
    tril2, triu2, expand2 = consts
    xbc3 = xbc.reshape(bsz, seq, D_XBC)
    dt3 = dt.reshape(bsz, seq, LANES)
    fwd = lambda i: (0, i, 0)
    bwd = lambda i: (0, n - 1 - i, 0)
    const = lambda i: (0, 0)
    state = pl.BlockSpec(h0.shape, lambda i: (0, 0, 0, 0, 0))
    y_f, y_b, h_fin = pl.pallas_call(
        functools.partial(_ssd_kernel, chunk=L, nchunks=n, bsz=bsz),
        out_shape=(jax.ShapeDtypeStruct((bsz, seq, D_SSD), F32), jax.ShapeDtypeStruct((bsz, seq, D_SSD), F32),
                   jax.ShapeDtypeStruct(h0.shape, F32)),
        grid=(n,),
        in_specs=[pl.BlockSpec((bsz, L, D_XBC), fwd), pl.BlockSpec((bsz, L, D_XBC), bwd),
                  pl.BlockSpec((bsz, L, LANES), fwd), pl.BlockSpec((bsz, L, LANES), bwd),
                  pl.BlockSpec((1, LANES), const), pl.BlockSpec((1, D_SSD), const), state,
                  pl.BlockSpec((L, 2 * L), const), pl.BlockSpec((L, 2 * L), const),
                  pl.BlockSpec((2 * LANES, D_SSD), const)],
        out_specs=(pl.BlockSpec((bsz, L, D_SSD), fwd), pl.BlockSpec((bsz, L, D_SSD), bwd), state),
        scratch_shapes=[pltpu.VMEM(h0.shape, F32)],
        compiler_params=_cparams("arbitrary"),
        name="ssd_scan",
    )(xbc3, xbc3, dt3, dt3, a_row, dskip_row, h0, tril2, triu2, expand2)
    return y_f.reshape(t, D_SSD), y_b.reshape(t, D_SSD), h_fin


def _ssd_consts():
    L = SSD_CHUNK
    r = jnp.arange(L)
    tril = (r[:, None] >= r[None, :]).astype(BF16)
    triu = (r[:, None] <= r[None, :]).astype(BF16)
    c = jnp.arange(LANES)[:, None]
    col = jnp.arange(D_SSD)[None, :]
    expand = ((c < 2 * SSD_HEADS) & ((c % SSD_HEADS) == col // SSD_HEADDIM)).astype(BF16)
    return (jnp.concatenate([tril, tril], axis=1), jnp.concatenate([triu, triu], axis=1),
            jnp.concatenate([expand, expand], axis=0))


def _out_kernel(yc_ref, yr_ref, yf_ref, yb_ref, z_ref, hx_ref, gt1_ref, lng_ref, lnb_ref, ng_ref,
                wout_ref, gffn_ref, sh2_ref, sc2_ref, wr_ref, br_ref, hnew_ref, text_ref, comb_ref, keys_ref,
                *, cw, nparts):
    n = hx_ref.shape[0] // nparts
    parts = [_out_rows(slice(p * n, (p + 1) * n), p * n, n, cw,
                       yc_ref, yr_ref, yf_ref, yb_ref, z_ref, hx_ref, gt1_ref, lng_ref, lnb_ref, ng_ref,
                       wout_ref, gffn_ref, sh2_ref, sc2_ref, wr_ref, br_ref, hnew_ref, text_ref, comb_ref, keys_ref)
             for p in range(nparts)]
    while parts:
        parts = [p for p in parts if next(p, StopIteration) is not StopIteration]


def _out_rows(rs, r0, n, cw, yc_ref, yr_ref, yf_ref, yb_ref, z_ref, hx_ref, gt1_ref, lng_ref, lnb_ref, ng_ref,
              wout_ref, gffn_ref, sh2_ref, sc2_ref, wr_ref, br_ref, hnew_ref, text_ref, comb_ref, keys_ref):
    yc = yc_ref[rs, :]
    yr = yr_ref[rs, :]
    nconv = yc.shape[1] + yr.shape[1]
    mu = (jnp.sum(yc, axis=-1, keepdims=True) + jnp.sum(yr, axis=-1, keepdims=True)) / nconv
    dc = yc - mu
    dr = yr - mu
    var = (jnp.sum(dc * dc, axis=-1, keepdims=True) + jnp.sum(dr * dr, axis=-1, keepdims=True)) / nconv
    inv = lax.rsqrt(var + LN_EPS)
    a_c = _silu(dc * inv * lng_ref[:, 0:cw] + lnb_ref[:, 0:cw])
    a_r = _silu(dr * inv * lng_ref[:, cw:] + lnb_ref[:, cw:])
    y = (yf_ref[rs, :] + yb_ref[rs, :]) * _silu(z_ref[rs, :])
    ssd = y * lax.rsqrt(jnp.mean(y * y, axis=-1, keepdims=True) + RMS_EPS) * ng_ref[...]
    yield
    mix = (_dot(a_c.astype(BF16), wout_ref[0:cw, :]) + _dot(a_r.astype(BF16), wout_ref[cw:D_CONV, :])
           + _dot(ssd.astype(BF16), wout_ref[D_CONV:, :]))
    yield
    h = hx_ref[rs, :] + gt1_ref[0] * mix
    hnew_ref[rs, :] = h
    t = h * lax.rsqrt(jnp.mean(h * h, axis=-1, keepdims=True) + RMS_EPS) * gffn_ref[...]
    t = t * (1.0 + sc2_ref[0]) + sh2_ref[0]
    tb = t.astype(BF16)

    logits = _dot(tb, wr_ref[...]) + br_ref[...]
    yield
    lane = lax.broadcasted_iota(jnp.int32, logits.shape, 1).astype(F32)
    big = float(LANES)
    ninf = -jnp.inf
    gl = jnp.where(lane < MOE_GROUPS, logits, ninf)
    gmax = jnp.max(gl, axis=-1, keepdims=True)
    gsel = jnp.min(jnp.where(gl == gmax, lane, big), axis=-1, keepdims=True)
    gprob = 1.0 / jnp.sum(jnp.exp(gl - gmax), axis=-1, keepdims=True)
    lo = ROUTER_LANE0 + gsel * MOE_PER_GROUP
    emask = (lane >= lo) & (lane < lo + MOE_PER_GROUP)
    el = jnp.where(emask, logits, ninf)
    e1 = jnp.max(el, axis=-1, keepdims=True)
    i1 = jnp.min(jnp.where(el == e1, lane, big), axis=-1, keepdims=True)
    el2 = jnp.where(lane == i1, ninf, el)
    e2 = jnp.max(el2, axis=-1, keepdims=True)
    i2 = jnp.min(jnp.where(el2 == e2, lane, big), axis=-1, keepdims=True)
    p2 = jnp.exp(e2 - e1)
    w1 = gprob / (1.0 + p2)
    w2 = gprob * p2 / (1.0 + p2)
    comb_ref[rs, :] = jnp.where(lane == i1, w1, 0.0) + jnp.where(lane == i2, w2, 0.0)
    first = i1 < i2
    l_lo = jnp.minimum(i1, i2) - lo
    l_hi = jnp.maximum(i1, i2) - lo
    key = gsel * PAIRS_PER_GROUP + l_lo * (7.0 - l_lo) * 0.5 + (l_hi - l_lo - 1.0)
    meta = (jnp.where(lane == META_WA, jnp.where(first, w1, w2), 0.0)
            + jnp.where(lane == META_WB, jnp.where(first, w2, w1), 0.0)
            + jnp.where(lane == META_KEY, key, 0.0))
    _store_token_major(text_ref, r0 * TEXT_PITCH, n, TEXT_PITCH, t)
    text_ref[pl.ds(r0 * TEXT_PITCH + TOKEN_ROWS, n, stride=TEXT_PITCH), :] = meta
    keys_ref[:, rs] = meta.T[0:SUBLANES, :]


def _out_proj(yc, yr, yf, yb, z, hx, gt1, lng, lnb, ng, wout, gffn, sh2, sc2, wr, br, seq, tm):
    t, d = hx.shape
    cw = yc.shape[1]
    tpb = seq // tm
    tok = lambda i: (i, 0)
    per_batch = lambda i: (i // tpb, 0, 0)
    const = lambda i: (0, 0)
    row = lambda n: pl.BlockSpec((1, n), const)
    return pl.pallas_call(
        functools.partial(_out_kernel, cw=cw, nparts=max(1, tm // 256)),
        out_shape=(jax.ShapeDtypeStruct((t, d), F32), jax.ShapeDtypeStruct((t * TEXT_PITCH, LANES), F32),
                   jax.ShapeDtypeStruct((t, LANES), F32), jax.ShapeDtypeStruct((SUBLANES, t), F32)),
        grid=(t // tm,),
        in_specs=[pl.BlockSpec((tm, cw), tok), pl.BlockSpec((tm, D_CONV - cw), tok),
                  pl.BlockSpec((tm, D_SSD), tok), pl.BlockSpec((tm, D_SSD), tok), pl.BlockSpec((tm, D_SSD), tok),
                  pl.BlockSpec((tm, d), tok), pl.BlockSpec((1, 1, d), per_batch),
                  row(D_CONV), row(D_CONV), row(D_SSD),
                  pl.BlockSpec((D_CONV + D_SSD, d), const), row(d),
                  pl.BlockSpec((1, 1, d), per_batch), pl.BlockSpec((1, 1, d), per_batch),
                  pl.BlockSpec((d, LANES), const), row(LANES)],
        out_specs=(pl.BlockSpec((tm, d), tok), pl.BlockSpec((tm * TEXT_PITCH, LANES), tok),
                   pl.BlockSpec((tm, LANES), tok), pl.BlockSpec((SUBLANES, tm), lambda i: (0, i))),
        compiler_params=_cparams("arbitrary"),
        name="out_proj",
    )(yc, yr, yf, yb, z, hx, gt1, lng, lnb, ng, wout, gffn, sh2, sc2, wr, br)


def _moe_kernel(t_ref, comb_ref, hx_ref, gt2_ref, wg_ref, wu_ref, wd_ref, o_ref, acc_ref):
    e = pl.program_id(1)

    @pl.when(e == 0)
    def _():
        acc_ref[...] = jnp.zeros_like(acc_ref)

    t = t_ref[...].astype(BF16)
    comb = comb_ref[...]
    lane = lax.broadcasted_iota(jnp.int32, comb.shape, 1)
    cw = jnp.sum(jnp.where(lane == e + ROUTER_LANE0, comb, 0.0), axis=-1, keepdims=True)
    hid = _silu(_dot(t, wg_ref[0])) * _dot(t, wu_ref[0])
    acc_ref[...] += _dot((hid * cw).astype(BF16), wd_ref[0])

    @pl.when(e == N_EXPERTS - 1)
    def _():
        o_ref[...] = hx_ref[...] + gt2_ref[0] * acc_ref[...]


def _moe(t_in, comb, hx, gt2, wg, wu, wd, seq, tm):
    t, d = hx.shape
    ff = wg.shape[2]
    tpb = seq // tm
    tok = lambda i, e: (i, 0)
    per_batch = lambda i, e: (i // tpb, 0, 0)
    return pl.pallas_call(
        _moe_kernel,
        out_shape=jax.ShapeDtypeStruct((t, d), F32),
        grid=(t // tm, N_EXPERTS),
        in_specs=[pl.BlockSpec((tm, d), tok), pl.BlockSpec((tm, LANES), tok), pl.BlockSpec((tm, d), tok),
                  pl.BlockSpec((1, 1, d), per_batch),
                  pl.BlockSpec((1, d, ff), lambda i, e: (e, 0, 0)),
                  pl.BlockSpec((1, d, ff), lambda i, e: (e, 0, 0)),
                  pl.BlockSpec((1, ff, d), lambda i, e: (e, 0, 0))],
        out_specs=pl.BlockSpec((tm, d), tok),
        scratch_shapes=[pltpu.VMEM((tm, d), F32)],
        compiler_params=_cparams("arbitrary", "arbitrary"),
        name="moe_dense",
    )(t_in, comb, hx, gt2, wg, wu, wd)


def _moe_sparse_kernel(ea_ref, eb_ref, nvalid_ref, idx_ref, idx_next_ref, text_hbm, *rest, rows, nsteps):
    weights = (rest[0:6], rest[6:12])
    out_hbm, gbuf, obuf, gsem, ssem = rest[12:]
    i = pl.program_id(0)

    def gather_copy(tok, r, s):
        return pltpu.make_async_copy(text_hbm.at[pl.ds(tok * TEXT_PITCH, TEXT_PITCH), :],
                                     gbuf.at[s, pl.ds(r * TEXT_PITCH, TEXT_PITCH), :], gsem.at[s])

    def scatter_copy(tok, r, s):
        return pltpu.make_async_copy(obuf.at[s, pl.ds(r * OUT_PITCH, OUT_PITCH), :],
                                     out_hbm.at[pl.ds(tok * OUT_PITCH, OUT_PITCH), :], ssem.at[s])

    def scatter_wait(n, s):
        @pl.when(n == rows)
        def _():
            for r in range(rows):
                scatter_copy(0, r, s).wait()

        @pl.when(jnp.logical_and(n > 0, n < rows))
        def _():
            def body(r, carry):
                scatter_copy(0, r, s).wait()
                return carry
            lax.fori_loop(0, n, body, 0)

    @pl.when(i == 0)
    def _():
        for r in range(rows):
            gather_copy(idx_ref[0, 0, r], r, 0).start(priority=r % 2)

    for s in range(2):
        tile = 2 * i + s
        wga_ref, wua_ref, wda_ref, wgb_ref, wub_ref, wdb_ref = weights[s]
        nxt_ref, nxt_half = (idx_ref, 1) if s == 0 else (idx_next_ref, 0)
        n_here = nvalid_ref[tile]

        @pl.when(jnp.logical_or(tile == 0, nvalid_ref[jnp.maximum(tile - 1, 0)] > 0))
        def _():
            for r in range(rows):
                gather_copy(0, r, s).wait()

        @pl.when(tile >= 2)
        def _():
            scatter_wait(nvalid_ref[jnp.maximum(tile - 2, 0)], s)

        @pl.when(n_here > 0)
        def _():
            for r in range(rows):
                gather_copy(nxt_ref[0, nxt_half, r], r, 1 - s).start(priority=r % 2)
            gs = gbuf.at[s]
            t = _load_token_major(gs, rows, TEXT_PITCH, BF16)
            meta = gs[pl.ds(TOKEN_ROWS, rows, stride=TEXT_PITCH), :]
            wa = meta[:, META_WA:META_WA + 1]
            wb = meta[:, META_WB:META_WB + 1]
            ha = _silu(_dot(t, wga_ref[0])) * _dot(t, wua_ref[0]) * wa
            hb = _silu(_dot(t, wgb_ref[0])) * _dot(t, wub_ref[0]) * wb
            out = _dot(ha.astype(BF16), wda_ref[0]) + _dot(hb.astype(BF16), wdb_ref[0])
            _store_token_major(obuf.at[s], 0, rows, OUT_PITCH, out)

            @pl.when(n_here == rows)
            def _():
                for r in range(rows):
                    scatter_copy(idx_ref[0, s, r], r, s).start(priority=r % 2)

            @pl.when(n_here < rows)
            def _():
                def body(r, carry):
                    scatter_copy(idx_ref[0, s, r], r, s).start()
                    return carry
                lax.fori_loop(0, n_here, body, 0)

    @pl.when(i == nsteps - 1)
    def _():
        @pl.when(nvalid_ref[2 * i + 1] > 0)
        def _():
            for r in range(rows):
                gather_copy(0, r, 0).wait()
        scatter_wait(nvalid_ref[2 * i], 0)
        scatter_wait(nvalid_ref[2 * i + 1], 1)


def _route_plan(keys, rows):
    t = keys.shape[0]
    ntiles = 2 * (((t + N_BUCKETS * (rows - 1)) // rows + 1) // 2)
    kk = jnp.arange(N_BUCKETS, dtype=jnp.int32)
    cnt = jnp.sum((keys[None, :] == kk[:, None]).astype(jnp.int32), axis=1)
    off = jnp.cumsum(cnt) - cnt
    tiles_k = (cnt + rows - 1) // rows
    tile_end = jnp.cumsum(tiles_k)
    tile_off = tile_end - tiles_k
    order = jnp.argsort(keys, stable=True).astype(jnp.int32)
    tile = jnp.arange(ntiles, dtype=jnp.int32)
    in_k = (tile[:, None] >= tile_off[None, :]) & ((tile[:, None] < tile_end[None, :]) | (kk[None, :] == N_BUCKETS - 1))
    in_k = in_k & ((tiles_k[None, :] > 0) | (kk[None, :] == N_BUCKETS - 1))
    pick = lambda v: jnp.sum(jnp.where(in_k, v[None, :], 0), axis=1)
    k_of_tile = pick(kk)
    tile_in_k = tile - pick(tile_off)
    nvalid = jnp.clip(pick(cnt) - tile_in_k * rows, 0, rows)
    nvalid = jnp.where(tile < tile_end[-1], nvalid, 0).astype(jnp.int32)
    j = jnp.arange(rows, dtype=jnp.int32)
    pos = pick(off)[:, None] + tile_in_k[:, None] * rows + j[None, :]
    tok = order[jnp.clip(pos, 0, t - 1)]
    valid = j[None, :] < nvalid[:, None]
    gidx = jnp.where(valid, tok, 0).astype(jnp.int32).reshape(ntiles // 2, 2, rows)
    g = k_of_tile // PAIRS_PER_GROUP
    pair = k_of_tile % PAIRS_PER_GROUP
    ge3 = (pair >= 3).astype(jnp.int32)
    ge5 = (pair >= 5).astype(jnp.int32)
    ea = (g * MOE_PER_GROUP + ge3 + ge5).astype(jnp.int32)
    eb = (g * MOE_PER_GROUP + pair + 1 - 2 * ge3 - ge5).astype(jnp.int32)
    return ea, eb, nvalid, gidx


def _moe_sparse(text, keys, wg, wu, wd):
    t = keys.shape[0]
    d, ff = wg.shape[1], wg.shape[2]
    rows = MOE_ROWS
    ea, eb, nvalid, gidx = _route_plan(keys, rows)
    nsteps = gidx.shape[0]
    cur = lambda i, *_: (i, 0, 0)
    nxt = lambda i, *_: (jnp.minimum(i + 1, nsteps - 1), 0, 0)
    idx_spec = lambda m: pl.BlockSpec((1, 2, rows), m, memory_space=pltpu.SMEM)

    def weight_specs(half):
        up = lambda e: pl.BlockSpec((1, d, ff), lambda i, ea, eb, nv: ((ea, eb)[e][2 * i + half], 0, 0))
        dn = lambda e: pl.BlockSpec((1, ff, d), lambda i, ea, eb, nv: ((ea, eb)[e][2 * i + half], 0, 0))
        return [up(0), up(0), dn(0), up(1), up(1), dn(1)]

    grid_spec = pltpu.PrefetchScalarGridSpec(
        num_scalar_prefetch=3,
        grid=(nsteps,),
        in_specs=[idx_spec(cur), idx_spec(nxt), pl.BlockSpec(memory_space=pl.ANY)]
                 + weight_specs(0) + weight_specs(1),
        out_specs=pl.BlockSpec(memory_space=pl.ANY),
        scratch_shapes=[pltpu.VMEM((2, rows * TEXT_PITCH, LANES), F32),
                        pltpu.VMEM((2, rows * OUT_PITCH, LANES), F32),
                        pltpu.SemaphoreType.DMA((2,)), pltpu.SemaphoreType.DMA((2,))])
    return pl.pallas_call(
        functools.partial(_moe_sparse_kernel, rows=rows, nsteps=nsteps),
        out_shape=jax.ShapeDtypeStruct((t * OUT_PITCH, LANES), F32),
        grid_spec=grid_spec,
        compiler_params=_cparams("arbitrary"),
        name="moe_sparse",
    )(ea, eb, nvalid, gidx, gidx, text, *([wg, wu, wd] * 4))


def _store_token_major(ref, row0, n, pitch, x):
    for c in range(TOKEN_ROWS):
        ref[pl.ds(row0 + c, n, stride=pitch), :] = x[:, c * LANES:(c + 1) * LANES]


def _load_token_major(ref, n, pitch, dtype, row0=0):
    return jnp.concatenate([ref[pl.ds(row0 + c, n, stride=pitch), :].astype(dtype) for c in range(TOKEN_ROWS)],
                           axis=1)


def _final_kernel(h_ref, moe_ref, gt_ref, g_ref, o_ref):
    h = h_ref[...] + gt_ref[0] * _load_token_major(moe_ref, h_ref.shape[0], OUT_PITCH, F32)
    o_ref[...] = h * lax.rsqrt(jnp.mean(h * h, axis=-1, keepdims=True) + RMS_EPS) * g_ref[...]


def _final_norm(h, moe, gate, g, seq, tm):
    t, d = h.shape
    tpb = seq // tm
    tok = pl.BlockSpec((tm, d), lambda i: (i, 0))
    return pl.pallas_call(
        _final_kernel,
        out_shape=jax.ShapeDtypeStruct((t, d), F32),
        grid=(t // tm,),
        in_specs=[tok, pl.BlockSpec((tm * OUT_PITCH, LANES), lambda i: (i, 0)),
                  pl.BlockSpec((1, 1, d), lambda i: (i // tpb, 0, 0)),
                  pl.BlockSpec((1, d), lambda i: (0, 0))],
        out_specs=tok,
        compiler_params=_cparams("arbitrary"),
        name="final_norm",
    )(h, moe, gate, g)


def _pad_rows(w, n):
    return jnp.zeros((n, w.shape[1]), w.dtype).at[:w.shape[0]].set(w)


def kernel(x, c, ctx, c_ctx, w_ada, b_ada, g_mix, g_ffn, w_in, conv_w, conv_b, conv_ln_g, conv_ln_b,
           ssd_conv_w, ssd_conv_b, dt_bias, a_log, d_skip, ssd_norm_g, w_out, w_router_group,
           b_router_group, w_router_expert, b_router_expert, w_gate, w_up, w_down, g_final):
    bsz, seq, d = x.shape
    lc = ctx.shape[1]
    depth = w_in.shape[0]
    lo = 2 * D_CONV + D_SSD
    n_main = lo + D_XBC
    tm_x, tm_c = 512, lc
    consts = _ssd_consts()
    gw = (SSD_HEADS // SSD_GROUPS) * SSD_HEADDIM
    zero_state = jnp.zeros((bsz, 2, SSD_GROUPS, SSD_STATE, gw), F32)

    h_x = x.reshape(bsz * seq, d)
    h_c = ctx.reshape(bsz * lc, d)
    cvecs = jnp.zeros((SUBLANES, d), F32).at[:bsz].set(c).at[bsz].set(c_ctx)
    row = lambda v: v.reshape(1, -1)
    pending = None

    mods_all = _ada(cvecs, w_ada, b_ada)

    for l in range(depth):
        last = l == depth - 1
        mods = mods_all[l].reshape(SUBLANES, 6, d)
        mx = [mods[:bsz, j][:, None, :] for j in range(6)]
        mc = [jnp.broadcast_to(mods[bsz, j][None, None, :], (bsz, 1, d)) for j in range(6)]
        w_main = w_in[l][:, :n_main].astype(BF16)
        w_dt = jnp.zeros((d, LANES), F32).at[:, :2 * SSD_HEADS].set(w_in[l][:, n_main:]).astype(BF16)
        dtb_row = jnp.zeros((1, LANES), F32).at[0, :2 * SSD_HEADS].set(dt_bias[l].reshape(-1))
        a_row = jnp.zeros((1, LANES), F32).at[0, :2 * SSD_HEADS].set(-jnp.exp(a_log[l].astype(F32)).reshape(-1))
        dskip_row = jnp.repeat(d_skip[l], SSD_HEADDIM).reshape(1, D_SSD)
        convw_pad = _pad_rows(conv_w[l], 32)
        wout_b = w_out[l].astype(BF16)
        wr = jnp.zeros((d, LANES), F32).at[:, :MOE_GROUPS].set(w_router_group[l])
        wr = wr.at[:, ROUTER_LANE0:ROUTER_LANE0 + N_EXPERTS].set(w_router_expert[l]).astype(BF16)
        br = jnp.zeros((1, LANES), F32).at[0, :MOE_GROUPS].set(b_router_group[l])
        br = br.at[0, ROUTER_LANE0:ROUTER_LANE0 + N_EXPERTS].set(b_router_expert[l])
        wg_b, wu_b, wd_b = w_gate[l].astype(BF16), w_up[l].astype(BF16), w_down[l].astype(BF16)

        def out_proj(h, m, s, tm, yc, yr, yf, yb, z):
            return _out_proj(yc, yr, yf, yb, z, h, m[2], row(conv_ln_g[l]), row(conv_ln_b[l]),
                             row(ssd_norm_g[l]), wout_b, row(g_ffn[l]), m[3], m[4], wr, br, s, tm)

        in_w = (row(g_mix[l]), w_main, w_dt, ssd_conv_w[l], ssd_conv_b[l], dtb_row)
        v_c, z_c, xbc_c, dt_c = _in_proj(h_c, mc[0], mc[1], *in_w, lc, tm_c)
        yf_c, yb_c, h_ctx = _ssd(xbc_c, dt_c, a_row, dskip_row, zero_state, consts, bsz, lc)

        if pending is None:
            v, z, xbc, dt = _in_proj(h_x, mx[0], mx[1], *in_w, seq, tm_x)
        else:
            h_x, v, z, xbc, dt = _in_proj(h_x, mx[0], mx[1], *in_w, seq, tm_x, moe=pending[0], gate=pending[1])
        yf, yb, _ = _ssd(xbc, dt, a_row, dskip_row, h_ctx, consts, bsz, seq)
        half = D_CONV // 2
        yc = _phase_conv(v, convw_pad, row(conv_b[l]), half // LANES, 4 * tm_x)
        yr = _row_conv(v.reshape(bsz, seq, D_CONV), convw_pad, row(conv_b[l]), half // LANES,
                       half // LANES).reshape(bsz * seq, half)
        h_x, text, _, keys_t = out_proj(h_x, mx, seq, tm_x, yc, yr, yf, yb, z)
        moe_out = _moe_sparse(text, keys_t[META_KEY].astype(jnp.int32), wg_b, wu_b, wd_b)
        pending = (moe_out, mx[5])

        if not last:
            y_all = _seg_conv(v_c, convw_pad, row(conv_b[l]), lc, D_CONV, tm_c)
            h_c, text_c, comb_c, _ = out_proj(h_c, mc, lc, tm_c, y_all[:, :half], y_all[:, half:], yf_c, yb_c, z_c)
            t_c = text_c.reshape(bsz * lc, TEXT_PITCH, LANES)[:, :TOKEN_ROWS].reshape(bsz * lc, d)
            h_c = _moe(t_c, comb_c, h_c, mc[5], wg_b, wu_b, wd_b, bsz * lc, bsz * lc)

    out = _final_norm(h_x, pending[0], pending[1], row(g_final), seq, tm_x)
    return out.reshape(bsz, seq, d).astype(x.dtype)
```

```python
import functools

import jax
import jax.numpy as jnp
from jax import lax
from jax.experimental import pallas as pl
from jax.experimental.pallas import tpu as pltpu

F32 = jnp.float32
BF16 = jnp.bfloat16

RMS_EPS = 1e-6
LN_EPS = 1e-5

GRID_W = 64
D_CONV = 512
D_SSD = 512
SSD_HEADS = 8
SSD_HEADDIM = 64
SSD_GROUPS = 2
SSD_STATE = 128
D_BC = SSD_GROUPS * SSD_STATE
D_XBC = D_SSD + 2 * D_BC
MOE_GROUPS = 4
MOE_PER_GROUP = 4
N_EXPERTS = MOE_GROUPS * MOE_PER_GROUP

LANES = 128
SUBLANES = 8
CONV_PAD = 16
SSD_CHUNK = 128
ROUTER_LANE0 = MOE_GROUPS
PAIRS_PER_GROUP = MOE_PER_GROUP * (MOE_PER_GROUP - 1) // 2
N_BUCKETS = MOE_GROUPS * PAIRS_PER_GROUP
META_WA, META_WB, META_KEY = 0, 1, 2
MOE_ROWS = 256
TOKEN_ROWS = 8
TEXT_PITCH = TOKEN_ROWS + 1
OUT_PITCH = TOKEN_ROWS
VMEM_LIMIT = 56 * 1024 * 1024


def _cparams(*sem):
    return pltpu.CompilerParams(dimension_semantics=sem, vmem_limit_bytes=VMEM_LIMIT)


def _dot(a, b):
    return jnp.dot(a, b, preferred_element_type=F32)


def _silu(x):
    return x * jax.nn.sigmoid(x)


def _split_bf16(x):
    hi = x.astype(BF16)
    lo = (x - hi.astype(F32)).astype(BF16)
    return hi, lo


def _ada_kernel(c_ref, w_ref, b_ref, o_ref):
    c = c_ref[...]
    o_ref[0] = jnp.dot(_silu(c), w_ref[0], preferred_element_type=F32,
                       precision=lax.Precision.HIGHEST) + b_ref[0]


def _ada(cvecs, w_ada, b_ada):
    depth, d, n = w_ada.shape
    bn = 1536
    rows = cvecs.shape[0]
    return pl.pallas_call(
        _ada_kernel,
        out_shape=jax.ShapeDtypeStruct((depth, rows, n), F32),
        grid=(depth, n // bn),
        in_specs=[pl.BlockSpec((rows, d), lambda l, j: (0, 0)),
                  pl.BlockSpec((1, d, bn), lambda l, j: (l, 0, j)),
                  pl.BlockSpec((1, 1, bn), lambda l, j: (l, 0, j))],
        out_specs=pl.BlockSpec((1, rows, bn), lambda l, j: (l, 0, j)),
        compiler_params=_cparams("arbitrary", "arbitrary"),
        name="ada",
    )(cvecs, w_ada, b_ada.reshape(depth, 1, n))


def _in_kernel(*refs, merge, tm, tpb, k):
    if merge:
        (xp_ref, x_ref, xn_ref, mp_ref, m_ref, mn_ref, gt_ref, sh_ref, sc_ref, g_ref, w_ref, wdt_ref,
         cw_ref, cb_ref, dtb_ref, h_ref, v_ref, z_ref, xbc_ref, dt_ref, scr, oscr) = refs
    else:
        (xp_ref, x_ref, xn_ref, sh_ref, sc_ref, g_ref, w_ref, wdt_ref,
         cw_ref, cb_ref, dtb_ref, v_ref, z_ref, xbc_ref, dt_ref, scr, oscr) = refs
        mp_ref = m_ref = mn_ref = None
    i = pl.program_id(0)
    first = (i % tpb) == 0
    last = (i % tpb) == tpb - 1
    nparts = scr.shape[0]
    n = tm // nparts
    nslab = D_XBC // LANES
    ng = n // SUBLANES

    def stream(xr, mr, r0, rows):
        x = xr[r0:r0 + rows, :]
        if merge:
            x = x + gt_ref[0] * _load_token_major(mr, rows, OUT_PITCH, F32, row0=r0 * OUT_PITCH)
        return x

    def rows_part(p):
        lo, hi = p * n, (p + 1) * n
        x_mid = stream(x_ref, m_ref, lo, n)
        if merge:
            h_ref[lo:hi, :] = x_mid
        before = stream(xp_ref, mp_ref, 0, SUBLANES) if p == 0 else stream(x_ref, m_ref, lo - SUBLANES, SUBLANES)
        after = (stream(xn_ref, mn_ref, 0, SUBLANES) if p == nparts - 1
                 else stream(x_ref, m_ref, hi, SUBLANES))
        x = jnp.concatenate([before, x_mid, after], axis=0)
        ms = jnp.mean(x * x, axis=-1, keepdims=True)
        h = x * lax.rsqrt(ms + RMS_EPS) * g_ref[...]
        h = (h * (1.0 + sc_ref[0]) + sh_ref[0]).astype(BF16)
        mid = slice(SUBLANES, SUBLANES + n)
        yield
        xbc0 = 2 * D_CONV + D_SSD

        def project_xbc(c2):
            xe = _dot(h, w_ref[:, xbc0 + 2 * c2 * LANES:xbc0 + (2 * c2 + 2) * LANES])
            for c in (2 * c2, 2 * c2 + 1):
                cs = slice((c - 2 * c2) * LANES, (c - 2 * c2 + 1) * LANES)
                head = xe[0:SUBLANES, cs]
                tail = xe[SUBLANES + n:, cs]
                scr[p, c, 0:SUBLANES, :] = jnp.where(first, 0.0, head) if p == 0 else head
                scr[p, c, mid, :] = xe[mid, cs]
                scr[p, c, SUBLANES + n:2 * SUBLANES + n, :] = (jnp.where(last, 0.0, tail) if p == nparts - 1
                                                                else tail)

        def conv_slab(c):
            cs = slice(c * LANES, (c + 1) * LANES)
            slab = scr.at[p, c]
            nb = 2 if ng % (2 * SUBLANES) == 0 else 1
            gb = ng // nb
            for blk in range(nb):
                loaded = {}

                def phase(q, grp):
                    if (q, grp) not in loaded:
                        start = SUBLANES * (1 + grp + blk * gb) + q
                        loaded[q, grp] = slab[pl.ds(start, gb, stride=SUBLANES), :]
                    return loaded[q, grp]

                for q in range(SUBLANES):
                    acc = jnp.broadcast_to(cb_ref[:, cs], (gb, LANES))
                    for j in range(k):
                        off = q + j - k // 2
                        acc = acc + cw_ref[j:j + 1, cs] * phase(off % SUBLANES, off // SUBLANES)
                    oscr[p, c, pl.ds(SUBLANES * blk * gb + q, gb, stride=SUBLANES), :] = _silu(acc)
            xbc_ref[lo:hi, cs] = oscr[p, c]

        def glu():
            u = _dot(h, w_ref[:, 0:D_CONV])[mid]
            gate = _dot(h, w_ref[:, D_CONV:2 * D_CONV])[mid]
            v_ref[lo:hi, :] = u * jax.nn.sigmoid(gate)

        def z_and_dt():
            z_ref[lo:hi, :] = _dot(h, w_ref[:, 2 * D_CONV:2 * D_CONV + D_SSD])[mid]
            xdt = _dot(h, wdt_ref[...])[mid] + dtb_ref[...]
            dt_ref[lo:hi, :] = jnp.maximum(xdt, 0.0) + jnp.log1p(jnp.exp(-jnp.abs(xdt)))

        project_xbc(0)
        for c2 in range(1, nslab // 2):
            project_xbc(c2)
            conv_slab(2 * c2 - 2)
            conv_slab(2 * c2 - 1)
            yield
        glu()
        conv_slab(nslab - 2)
        yield
        z_and_dt()
        conv_slab(nslab - 1)

    parts = [rows_part(p) for p in range(nparts)]
    while parts:
        parts = [g for g in parts if next(g, StopIteration) is not StopIteration]


def _in_proj(x, shift, scale, g, w_main, w_dt, conv_w, conv_b, dt_bias, seq, tm, moe=None, gate=None):
    t, d = x.shape
    tpb = seq // tm
    n_main = w_main.shape[1]
    k = conv_w.shape[0]
    merge = moe is not None
    nparts = 1
    nb8 = tm // SUBLANES
    last8 = t // SUBLANES - 1
    tok = lambda i: (i, 0)
    prev = lambda i: (jnp.maximum(i * nb8 - 1, 0), 0)
    nxt = lambda i: (jnp.minimum((i + 1) * nb8, last8), 0)
    per_batch = lambda i: (i // tpb, 0, 0)
    const = lambda i: (0, 0)
    tok_d = pl.BlockSpec((tm, d), tok)
    mod = pl.BlockSpec((1, 1, d), per_batch)
    halo = lambda m: pl.BlockSpec((SUBLANES, d), m)
    mhalo = lambda m: pl.BlockSpec((SUBLANES * OUT_PITCH, LANES), m)
    outs = (jax.ShapeDtypeStruct((t, D_CONV), F32), jax.ShapeDtypeStruct((t, D_SSD), F32),
            jax.ShapeDtypeStruct((t, D_XBC), F32), jax.ShapeDtypeStruct((t, LANES), F32))
    out_specs = (pl.BlockSpec((tm, D_CONV), tok), pl.BlockSpec((tm, D_SSD), tok),
                 pl.BlockSpec((tm, D_XBC), tok), pl.BlockSpec((tm, LANES), tok))
    args = (x, x, x)
    in_specs = [halo(prev), tok_d, halo(nxt)]
    if merge:
        args += (moe, moe, moe, gate)
        in_specs += [mhalo(prev), pl.BlockSpec((tm * OUT_PITCH, LANES), tok), mhalo(nxt), mod]
        outs = (jax.ShapeDtypeStruct((t, d), F32),) + outs
        out_specs = (tok_d,) + out_specs
    w_pad = jnp.zeros((SUBLANES, D_XBC), F32).at[:k].set(conv_w)
    return pl.pallas_call(
        functools.partial(_in_kernel, merge=merge, tm=tm, tpb=tpb, k=k),
        out_shape=outs,
        grid=(t // tm,),
        in_specs=in_specs + [mod, mod, pl.BlockSpec((1, d), const),
                             pl.BlockSpec((d, n_main), const), pl.BlockSpec((d, LANES), const),
                             pl.BlockSpec((SUBLANES, D_XBC), const), pl.BlockSpec((1, D_XBC), const),
                             pl.BlockSpec((1, LANES), const)],
        out_specs=out_specs,
        scratch_shapes=[pltpu.VMEM((nparts, D_XBC // LANES, tm // nparts + 2 * SUBLANES, LANES), F32),
                        pltpu.VMEM((nparts, D_XBC // LANES, tm // nparts, LANES), F32)],
        compiler_params=_cparams("arbitrary"),
        name="in_proj",
    )(*args, shift, scale, g, w_main, w_dt, w_pad, conv_b.reshape(1, D_XBC), dt_bias)


def _segconv_kernel(v_ref, w_ref, b_ref, o_ref, scr, *, seg, nseg, k, cb):
    zeros = jnp.zeros((CONV_PAD, cb), F32)
    scr[0:CONV_PAD, :] = zeros
    scr[CONV_PAD + seg:2 * CONV_PAD + seg, :] = zeros

    def body(s, carry):
        base = pl.multiple_of(s * seg, seg)
        scr[CONV_PAD:CONV_PAD + seg, :] = v_ref[pl.ds(base, seg), :]
        for c in range(cb // LANES):
            cs = slice(c * LANES, (c + 1) * LANES)
            acc = jnp.broadcast_to(b_ref[:, cs], (seg, LANES))
            for j in range(k):
                acc = acc + w_ref[j:j + 1, cs] * scr[pl.ds(CONV_PAD + j - k // 2, seg), cs]
            o_ref[pl.ds(base, seg), cs] = acc
        return carry

    lax.fori_loop(0, nseg, body, 0)


def _seg_conv(v, w_pad, b, seg, cb, tm):
    t = v.shape[0]
    k = 31
    tok = lambda i: (i, 0)
    const = lambda i: (0, 0)
    return pl.pallas_call(
        functools.partial(_segconv_kernel, seg=seg, nseg=tm // seg, k=k, cb=cb),
        out_shape=jax.ShapeDtypeStruct((t, cb), F32),
        grid=(t // tm,),
        in_specs=[pl.BlockSpec((tm, cb), tok), pl.BlockSpec((w_pad.shape[0], cb), const),
                  pl.BlockSpec((1, cb), const)],
        out_specs=pl.BlockSpec((tm, cb), tok),
        scratch_shapes=[pltpu.VMEM((seg + 2 * CONV_PAD, cb), F32)],
        compiler_params=_cparams("arbitrary"),
        name="seg_conv",
    )(v, w_pad, b)


def _phase_conv_segments(v_ref, w_ref, b_ref, o_ref, seg0, nseg, k, cs=slice(0, LANES), unroll=False):
    seg = SUBLANES * SUBLANES
    sub = lax.broadcasted_iota(jnp.int32, (SUBLANES, LANES), 0)

    def body(s, carry):
        base = s * seg if unroll else pl.multiple_of(s * seg, seg)
        x = [v_ref[pl.ds(base + j, SUBLANES, stride=SUBLANES), :] for j in range(SUBLANES)]
        shifted = {}

        def tap(q, c):
            if c == 0:
                return x[q]
            if (q, c) not in shifted:
                moved = pltpu.roll(x[q], (-c) % SUBLANES, axis=0)
                shifted[q, c] = jnp.where((sub + c >= 0) & (sub + c < SUBLANES), moved, 0.0)
            return shifted[q, c]

        for j in range(SUBLANES):
            acc = jnp.broadcast_to(b_ref[:, cs], (SUBLANES, LANES))
            for t in range(k):
                off = j + t - k // 2
                acc = acc + w_ref[t:t + 1, cs] * tap(off % SUBLANES, off // SUBLANES)
            o_ref[pl.ds(base + j, SUBLANES, stride=SUBLANES), :] = acc
        return carry

    if unroll:
        for s in range(seg0, seg0 + nseg):
            body(s, 0)
    else:
        lax.fori_loop(seg0, seg0 + nseg, body, 0)


def _rowconv_kernel(v_ref, w_ref, b_ref, o_ref, scr, *, width, rows, k):
    half = k // 2
    pad = jnp.zeros((half * width, LANES), F32)
    scr[0:half * width, :] = pad
    scr[(half + rows) * width:(2 * half + rows) * width, :] = pad
    scr[half * width:(half + rows) * width, :] = v_ref[0]

    def body(r, carry):
        acc = jnp.broadcast_to(b_ref[...], (width, LANES))
        for j in range(k):
            start = pl.multiple_of((r + j) * width, width)
            acc = acc + w_ref[j:j + 1, :] * scr[pl.ds(start, width), :]
        o_ref[0, pl.ds(pl.multiple_of(r * width, width), width), :] = acc
        return carry

    lax.fori_loop(0, rows, body, 0)


def _row_conv(v3, w_pad, b, lane_block0, nblk):
    bsz, seq, _ = v3.shape
    rows = seq // GRID_W
    k = 31
    return pl.pallas_call(
        functools.partial(_rowconv_kernel, width=GRID_W, rows=rows, k=k),
        out_shape=jax.ShapeDtypeStruct((bsz, seq, nblk * LANES), F32),
        grid=(bsz, nblk),
        in_specs=[pl.BlockSpec((1, seq, LANES), lambda bi, c: (bi, 0, lane_block0 + c)),
                  pl.BlockSpec((w_pad.shape[0], LANES), lambda bi, c: (0, lane_block0 + c)),
                  pl.BlockSpec((1, LANES), lambda bi, c: (0, lane_block0 + c))],
        out_specs=pl.BlockSpec((1, seq, LANES), lambda bi, c: (bi, 0, c)),
        scratch_shapes=[pltpu.VMEM(((rows + 2 * (k // 2)) * GRID_W, LANES), F32)],
        compiler_params=_cparams("arbitrary", "arbitrary"),
        name="row_conv",
    )(v3, w_pad, b)


def _ssd_kernel(xf_ref, xb_ref, dtf_ref, dtb_ref, a_ref, dskip_ref, h0_ref, tril_ref, triu_ref, e_ref,
                yf_ref, yb_ref, hfin_ref, s_ref, *, chunk, nchunks, bsz):
    i = pl.program_id(0)

    @pl.when(i == 0)
    def _():
        s_ref[...] = h0_ref[...]

    L = chunk
    hpg = SSD_HEADS // SSD_GROUPS
    gw = hpg * SSD_HEADDIM
    lane = lax.broadcasted_iota(jnp.int32, (L, LANES), 1)
    row_i = lax.broadcasted_iota(jnp.int32, (L, L), 0)
    col_i = lax.broadcasted_iota(jnp.int32, (L, L), 1)
    head_of_lane = lax.broadcasted_iota(jnp.int32, (L, gw), 1) // SSD_HEADDIM

    def direction(b, d, x_ref, dt_ref, y_ref):
        xall = x_ref[b]
        x = xall[:, :D_SSD]
        bm = xall[:, D_SSD:D_SSD + D_BC]
        cm = xall[:, D_SSD + D_BC:]
        dt = dt_ref[b]
        dmask = (lane >= SSD_HEADS * d) & (lane < SSD_HEADS * (d + 1))
        da = jnp.where(dmask, dt * a_ref[...], 0.0)
        tri2 = tril_ref[...] if d == 0 else triu_ref[...]
        cum = _dot(tri2, jnp.concatenate(_split_bf16(da), axis=0))
        yield
        tot = cum[L - 1:L, :] if d == 0 else cum[0:1, :]
        e_cum = jnp.where(dmask, jnp.exp(cum), 0.0)
        w_dt = jnp.where(dmask, jnp.exp(tot - cum) * dt, 0.0)
        e_tot = jnp.where(dmask[0:1], jnp.exp(tot), 0.0)
        stack = jnp.concatenate([e_cum, w_dt, jnp.broadcast_to(e_tot, (SUBLANES, LANES))], axis=0)
        ex = _dot(jnp.concatenate(_split_bf16(stack), axis=1), e_ref[...])
        yield
        e_cum_x = ex[0:L]
        w_dt_x = ex[L:2 * L]
        e_tot_x = ex[2 * L:2 * L + 1]
        src_t = (cum - jnp.log(dt)).T
        xw = (x * w_dt_x).astype(BF16)
        bm_t = bm.T
        yield
        mask = (row_i >= col_i) if d == 0 else (row_i <= col_i)
        ys = []
        for g in range(SSD_GROUPS):
            cg = cm[:, g * SSD_STATE:(g + 1) * SSD_STATE].astype(BF16)
            bg = bm[:, g * SSD_STATE:(g + 1) * SSD_STATE].astype(BF16)
            cb = lax.dot_general(cg, bg, (((1,), (1,)), ((), ())), preferred_element_type=F32).astype(BF16)
            yield
            xg = x[:, g * gw:(g + 1) * gw].astype(BF16)
            gs, xs = [], []
            for hh in range(hpg):
                c = SSD_HEADS * d + hpg * g + hh
                seg = cum[:, c:c + 1] - src_t[c:c + 1, :]
                gs.append(jnp.exp(jnp.where(mask, seg, -jnp.inf)).astype(BF16) * cb)
                xs.append(jnp.where(head_of_lane == hh, xg, jnp.zeros_like(xg)))
            y_diag = _dot(jnp.concatenate(gs, axis=1), jnp.concatenate(xs, axis=0))
            yield
            sg = s_ref[b, d, g]
            y_off = _dot(cg, sg.astype(BF16)) * e_cum_x[:, g * gw:(g + 1) * gw]
            s_ref[b, d, g] = (sg * e_tot_x[:, g * gw:(g + 1) * gw]
                              + _dot(bm_t[g * SSD_STATE:(g + 1) * SSD_STATE, :].astype(BF16),
                                     xw[:, g * gw:(g + 1) * gw]))
            ys.append(y_diag + y_off)
        y = jnp.concatenate(ys, axis=1)
        if d == 0:
            y = y + dskip_ref[...] * x
        y_ref[b] = y

    scans = [direction(b, d, refs[0], refs[1], refs[2]) for b in range(bsz)
             for d, refs in enumerate(((xf_ref, dtf_ref, yf_ref), (xb_ref, dtb_ref, yb_ref)))]
    while scans:
        scans = [s for s in scans if next(s, StopIteration) is not StopIteration]

    @pl.when(i == nchunks - 1)
    def _():
        hfin_ref[...] = s_ref[...]


def _ssd(xbc, dt, a_row, dskip_row, h0, consts, bsz, seq):
    t = xbc.shape[0]
    L = SSD_CHUNK
    n = seq // L
    tril2, triu2, expand2 = consts
    xbc3 = xbc.reshape(bsz, seq, D_XBC)
    dt3 = dt.reshape(bsz, seq, LANES)
    fwd = lambda i: (0, i, 0)
    bwd = lambda i: (0, n - 1 - i, 0)
    const = lambda i: (0, 0)
    state = pl.BlockSpec(h0.shape, lambda i: (0, 0, 0, 0, 0))
    y_f, y_b, h_fin = pl.pallas_call(
        functools.partial(_ssd_kernel, chunk=L, nchunks=n, bsz=bsz),
        out_shape=(jax.ShapeDtypeStruct((bsz, seq, D_SSD), F32), jax.ShapeDtypeStruct((bsz, seq, D_SSD), F32),
                   jax.ShapeDtypeStruct(h0.shape, F32)),
        grid=(n,),
        in_specs=[pl.BlockSpec((bsz, L, D_XBC), fwd), pl.BlockSpec((bsz, L, D_XBC), bwd),
                  pl.BlockSpec((bsz, L, LANES), fwd), pl.BlockSpec((bsz, L, LANES), bwd),
                  pl.BlockSpec((1, LANES), const), pl.BlockSpec((1, D_SSD), const), state,
                  pl.BlockSpec((L, 2 * L), const), pl.BlockSpec((L, 2 * L), const),
                  pl.BlockSpec((2 * LANES, D_SSD), const)],
        out_specs=(pl.BlockSpec((bsz, L, D_SSD), fwd), pl.BlockSpec((bsz, L, D_SSD), bwd), state),
        scratch_shapes=[pltpu.VMEM(h0.shape, F32)],
        compiler_params=_cparams("arbitrary"),
        name="ssd_scan",
    )(xbc3, xbc3, dt3, dt3, a_row, dskip_row, h0, tril2, triu2, expand2)
    return y_f.reshape(t, D_SSD), y_b.reshape(t, D_SSD), h_fin


def _ssd_consts():
    L = SSD_CHUNK
    r = jnp.arange(L)
    tril = (r[:, None] >= r[None, :]).astype(BF16)
    triu = (r[:, None] <= r[None, :]).astype(BF16)
    c = jnp.arange(LANES)[:, None]
    col = jnp.arange(D_SSD)[None, :]
    expand = ((c < 2 * SSD_HEADS) & ((c % SSD_HEADS) == col // SSD_HEADDIM)).astype(BF16)
    return (jnp.concatenate([tril, tril], axis=1), jnp.concatenate([triu, triu], axis=1),
            jnp.concatenate([expand, expand], axis=0))


def _out_kernel(*refs, cw, nparts, fuse_conv):
    if fuse_conv:
        nblk = cw // LANES
        v_refs, (cvw_ref, cvb_ref), rest, ycs = refs[:nblk], refs[nblk:nblk + 2], refs[nblk + 2:-1], refs[-1]
        seg = SUBLANES * SUBLANES

        def load_yc(rs):
            for c in range(nblk):
                _phase_conv_segments(v_refs[c], cvw_ref, cvb_ref, ycs.at[c], rs.start // seg,
                                     (rs.stop - rs.start) // seg, 31, cs=slice(c * LANES, (c + 1) * LANES),
                                     unroll=True)
            return jnp.concatenate([ycs[c, rs, :] for c in range(nblk)], axis=1)
    else:
        yc_ref, rest = refs[0], refs[1:]
        load_yc = lambda rs: yc_ref[rs, :]
    n = rest[4].shape[0] // nparts
    parts = [_out_rows(slice(p * n, (p + 1) * n), p * n, n, cw, load_yc, *rest) for p in range(nparts)]
    while parts:
        parts = [p for p in parts if next(p, StopIteration) is not StopIteration]


def _out_rows(rs, r0, n, cw, load_yc, yr_ref, yf_ref, yb_ref, z_ref, hx_ref, gt1_ref, lng_ref, lnb_ref, ng_ref,
              wout_ref, gffn_ref, sh2_ref, sc2_ref, wr_ref, br_ref, hnew_ref, text_ref, comb_ref, keys_ref):
    yc = load_yc(rs)
    yield
    yr = yr_ref[rs, :]
    nconv = yc.shape[1] + yr.shape[1]
    mu = (jnp.sum(yc, axis=-1, keepdims=True) + jnp.sum(yr, axis=-1, keepdims=True)) / nconv
    dc = yc - mu
    dr = yr - mu
    var = (jnp.sum(dc * dc, axis=-1, keepdims=True) + jnp.sum(dr * dr, axis=-1, keepdims=True)) / nconv
    inv = lax.rsqrt(var + LN_EPS)
    a_c = _silu(dc * inv * lng_ref[:, 0:cw] + lnb_ref[:, 0:cw])
    a_r = _silu(dr * inv * lng_ref[:, cw:] + lnb_ref[:, cw:])
    y = (yf_ref[rs, :] + yb_ref[rs, :]) * _silu(z_ref[rs, :])
    ssd = y * lax.rsqrt(jnp.mean(y * y, axis=-1, keepdims=True) + RMS_EPS) * ng_ref[...]
    yield
    mix = (_dot(a_c.astype(BF16), wout_ref[0:cw, :]) + _dot(a_r.astype(BF16), wout_ref[cw:D_CONV, :])
           + _dot(ssd.astype(BF16), wout_ref[D_CONV:, :]))
    yield
    h = hx_ref[rs, :] + gt1_ref[0] * mix
    hnew_ref[rs, :] = h
    t = h * lax.rsqrt(jnp.mean(h * h, axis=-1, keepdims=True) + RMS_EPS) * gffn_ref[...]
    t = t * (1.0 + sc2_ref[0]) + sh2_ref[0]
    tb = t.astype(BF16)

    logits = _dot(tb, wr_ref[...]) + br_ref[...]
    yield
    lane = lax.broadcasted_iota(jnp.int32, logits.shape, 1).astype(F32)
    big = float(LANES)
    ninf = -jnp.inf
    gl = jnp.where(lane < MOE_GROUPS, logits, ninf)
    gmax = jnp.max(gl, axis=-1, keepdims=True)
    gsel = jnp.min(jnp.where(gl == gmax, lane, big), axis=-1, keepdims=True)
    gprob = 1.0 / jnp.sum(jnp.exp(gl - gmax), axis=-1, keepdims=True)
    lo = ROUTER_LANE0 + gsel * MOE_PER_GROUP
    emask = (lane >= lo) & (lane < lo + MOE_PER_GROUP)
    el = jnp.where(emask, logits, ninf)
    e1 = jnp.max(el, axis=-1, keepdims=True)
    i1 = jnp.min(jnp.where(el == e1, lane, big), axis=-1, keepdims=True)
    el2 = jnp.where(lane == i1, ninf, el)
    e2 = jnp.max(el2, axis=-1, keepdims=True)
    i2 = jnp.min(jnp.where(el2 == e2, lane, big), axis=-1, keepdims=True)
    p2 = jnp.exp(e2 - e1)
    w1 = gprob / (1.0 + p2)
    w2 = gprob * p2 / (1.0 + p2)
    comb_ref[rs, :] = jnp.where(lane == i1, w1, 0.0) + jnp.where(lane == i2, w2, 0.0)
    first = i1 < i2
    l_lo = jnp.minimum(i1, i2) - lo
    l_hi = jnp.maximum(i1, i2) - lo
    key = gsel * PAIRS_PER_GROUP + l_lo * (7.0 - l_lo) * 0.5 + (l_hi - l_lo - 1.0)
    meta = (jnp.where(lane == META_WA, jnp.where(first, w1, w2), 0.0)
            + jnp.where(lane == META_WB, jnp.where(first, w2, w1), 0.0)
            + jnp.where(lane == META_KEY, key, 0.0))
    _store_token_major(text_ref, r0 * TEXT_PITCH, n, TEXT_PITCH, t)
    text_ref[pl.ds(r0 * TEXT_PITCH + TOKEN_ROWS, n, stride=TEXT_PITCH), :] = meta
    keys_ref[:, rs] = meta.T[0:SUBLANES, :]


def _out_proj(yc, yr, yf, yb, z, hx, gt1, lng, lnb, ng, wout, gffn, sh2, sc2, wr, br, seq, tm, col_conv=None):
    t, d = hx.shape
    tpb = seq // tm
    tok = lambda i: (i, 0)
    per_batch = lambda i: (i // tpb, 0, 0)
    const = lambda i: (0, 0)
    row = lambda n: pl.BlockSpec((1, n), const)
    if col_conv is None:
        cw = yc.shape[1]
        conv_args, conv_specs, scratch = (yc,), [pl.BlockSpec((tm, cw), tok)], []
    else:
        v, taps, bias = col_conv
        cw = D_CONV - yr.shape[1]
        nblk = cw // LANES
        conv_args = (v,) * nblk + (taps, bias)
        conv_specs = ([pl.BlockSpec((tm, LANES), lambda i, c=c: (i, c)) for c in range(nblk)]
                      + [pl.BlockSpec((taps.shape[0], cw), const), row(cw)])
        scratch = [pltpu.VMEM((nblk, tm, LANES), F32)]
    return pl.pallas_call(
        functools.partial(_out_kernel, cw=cw, nparts=max(1, tm // 256), fuse_conv=col_conv is not None),
        out_shape=(jax.ShapeDtypeStruct((t, d), F32), jax.ShapeDtypeStruct((t * TEXT_PITCH, LANES), F32),
                   jax.ShapeDtypeStruct((t, LANES), F32), jax.ShapeDtypeStruct((SUBLANES, t), F32)),
        grid=(t // tm,),
        scratch_shapes=scratch,
        in_specs=conv_specs + [pl.BlockSpec((tm, D_CONV - cw), tok),
                  pl.BlockSpec((tm, D_SSD), tok), pl.BlockSpec((tm, D_SSD), tok), pl.BlockSpec((tm, D_SSD), tok),
                  pl.BlockSpec((tm, d), tok), pl.BlockSpec((1, 1, d), per_batch),
                  row(D_CONV), row(D_CONV), row(D_SSD),
                  pl.BlockSpec((D_CONV + D_SSD, d), const), row(d),
                  pl.BlockSpec((1, 1, d), per_batch), pl.BlockSpec((1, 1, d), per_batch),
                  pl.BlockSpec((d, LANES), const), row(LANES)],
        out_specs=(pl.BlockSpec((tm, d), tok), pl.BlockSpec((tm * TEXT_PITCH, LANES), tok),
                   pl.BlockSpec((tm, LANES), tok), pl.BlockSpec((SUBLANES, tm), lambda i: (0, i))),
        compiler_params=_cparams("arbitrary"),
        name="out_proj",
    )(*conv_args, yr, yf, yb, z, hx, gt1, lng, lnb, ng, wout, gffn, sh2, sc2, wr, br)


def _moe_kernel(t_ref, comb_ref, hx_ref, gt2_ref, wg_ref, wu_ref, wd_ref, o_ref, acc_ref):
    e = pl.program_id(1)

    @pl.when(e == 0)
    def _():
        acc_ref[...] = jnp.zeros_like(acc_ref)

    t = t_ref[...].astype(BF16)
    comb = comb_ref[...]
    lane = lax.broadcasted_iota(jnp.int32, comb.shape, 1)
    cw = jnp.sum(jnp.where(lane == e + ROUTER_LANE0, comb, 0.0), axis=-1, keepdims=True)
    hid = _silu(_dot(t, wg_ref[0])) * _dot(t, wu_ref[0])
    acc_ref[...] += _dot((hid * cw).astype(BF16), wd_ref[0])

    @pl.when(e == N_EXPERTS - 1)
    def _():
        o_ref[...] = hx_ref[...] + gt2_ref[0] * acc_ref[...]


def _moe(t_in, comb, hx, gt2, wg, wu, wd, seq, tm):
    t, d = hx.shape
    ff = wg.shape[2]
    tpb = seq // tm
    tok = lambda i, e: (i, 0)
    per_batch = lambda i, e: (i // tpb, 0, 0)
    return pl.pallas_call(
        _moe_kernel,
        out_shape=jax.ShapeDtypeStruct((t, d), F32),
        grid=(t // tm, N_EXPERTS),
        in_specs=[pl.BlockSpec((tm, d), tok), pl.BlockSpec((tm, LANES), tok), pl.BlockSpec((tm, d), tok),
                  pl.BlockSpec((1, 1, d), per_batch),
                  pl.BlockSpec((1, d, ff), lambda i, e: (e, 0, 0)),
                  pl.BlockSpec((1, d, ff), lambda i, e: (e, 0, 0)),
                  pl.BlockSpec((1, ff, d), lambda i, e: (e, 0, 0))],
        out_specs=pl.BlockSpec((tm, d), tok),
        scratch_shapes=[pltpu.VMEM((tm, d), F32)],
        compiler_params=_cparams("arbitrary", "arbitrary"),
        name="moe_dense",
    )(t_in, comb, hx, gt2, wg, wu, wd)


def _moe_sparse_kernel(ea_ref, eb_ref, nvalid_ref, idx_ref, idx_next_ref, text_hbm, *rest, rows, nsteps):
    weights = (rest[0:6], rest[6:12])
    out_hbm, gbuf, obuf, gsem, ssem = rest[12:]
    i = pl.program_id(0)

    def gather_copy(tok, r, s):
        return pltpu.make_async_copy(text_hbm.at[pl.ds(tok * TEXT_PITCH, TEXT_PITCH), :],
                                     gbuf.at[s, pl.ds(r * TEXT_PITCH, TEXT_PITCH), :], gsem.at[s])

    def scatter_copy(tok, r, s):
        return pltpu.make_async_copy(obuf.at[s, pl.ds(r * OUT_PITCH, OUT_PITCH), :],
                                     out_hbm.at[pl.ds(tok * OUT_PITCH, OUT_PITCH), :], ssem.at[s])

    def scatter_wait(n, s):
        @pl.when(n == rows)
        def _():
            for r in range(rows):
                scatter_copy(0, r, s).wait()

        @pl.when(jnp.logical_and(n > 0, n < rows))
        def _():
            def body(r, carry):
                scatter_copy(0, r, s).wait()
                return carry
            lax.fori_loop(0, n, body, 0)

    @pl.when(i == 0)
    def _():
        for r in range(rows):
            gather_copy(idx_ref[0, 0, r], r, 0).start(priority=r % 2)

    for s in range(2):
        tile = 2 * i + s
        wga_ref, wua_ref, wda_ref, wgb_ref, wub_ref, wdb_ref = weights[s]
        nxt_ref, nxt_half = (idx_ref, 1) if s == 0 else (idx_next_ref, 0)
        n_here = nvalid_ref[tile]

        @pl.when(jnp.logical_or(tile == 0, nvalid_ref[jnp.maximum(tile - 1, 0)] > 0))
        def _():
            for r in range(rows):
                gather_copy(0, r, s).wait()

        @pl.when(tile >= 2)
        def _():
            scatter_wait(nvalid_ref[jnp.maximum(tile - 2, 0)], s)

        @pl.when(n_here > 0)
        def _():
            for r in range(rows):
                gather_copy(nxt_ref[0, nxt_half, r], r, 1 - s).start(priority=r % 2)
            gs = gbuf.at[s]
            t = _load_token_major(gs, rows, TEXT_PITCH, BF16)
            meta = gs[pl.ds(TOKEN_ROWS, rows, stride=TEXT_PITCH), :]
            wa = meta[:, META_WA:META_WA + 1]
            wb = meta[:, META_WB:META_WB + 1]
            ha = _silu(_dot(t, wga_ref[0])) * _dot(t, wua_ref[0]) * wa
            hb = _silu(_dot(t, wgb_ref[0])) * _dot(t, wub_ref[0]) * wb
            out = _dot(ha.astype(BF16), wda_ref[0]) + _dot(hb.astype(BF16), wdb_ref[0])
            _store_token_major(obuf.at[s], 0, rows, OUT_PITCH, out)

            @pl.when(n_here == rows)
            def _():
                for r in range(rows):
                    scatter_copy(idx_ref[0, s, r], r, s).start(priority=r % 2)

            @pl.when(n_here < rows)
            def _():
                def body(r, carry):
                    scatter_copy(idx_ref[0, s, r], r, s).start()
                    return carry
                lax.fori_loop(0, n_here, body, 0)

    @pl.when(i == nsteps - 1)
    def _():
        @pl.when(nvalid_ref[2 * i + 1] > 0)
        def _():
            for r in range(rows):
                gather_copy(0, r, 0).wait()
        scatter_wait(nvalid_ref[2 * i], 0)
        scatter_wait(nvalid_ref[2 * i + 1], 1)


def _route_plan(keys, rows):
    t = keys.shape[0]
    ntiles = 2 * (((t + N_BUCKETS * (rows - 1)) // rows + 1) // 2)
    kk = jnp.arange(N_BUCKETS, dtype=jnp.int32)
    cnt = jnp.sum((keys[None, :] == kk[:, None]).astype(jnp.int32), axis=1)
    off = jnp.cumsum(cnt) - cnt
    tiles_k = (cnt + rows - 1) // rows
    tile_end = jnp.cumsum(tiles_k)
    tile_off = tile_end - tiles_k
    order = jnp.argsort(keys, stable=True).astype(jnp.int32)
    tile = jnp.arange(ntiles, dtype=jnp.int32)
    in_k = (tile[:, None] >= tile_off[None, :]) & ((tile[:, None] < tile_end[None, :]) | (kk[None, :] == N_BUCKETS - 1))
    in_k = in_k & ((tiles_k[None, :] > 0) | (kk[None, :] == N_BUCKETS - 1))
    pick = lambda v: jnp.sum(jnp.where(in_k, v[None, :], 0), axis=1)
    k_of_tile = pick(kk)
    tile_in_k = tile - pick(tile_off)
    nvalid = jnp.clip(pick(cnt) - tile_in_k * rows, 0, rows)
    nvalid = jnp.where(tile < tile_end[-1], nvalid, 0).astype(jnp.int32)
    j = jnp.arange(rows, dtype=jnp.int32)
    pos = pick(off)[:, None] + tile_in_k[:, None] * rows + j[None, :]
    tok = order[jnp.clip(pos, 0, t - 1)]
    valid = j[None, :] < nvalid[:, None]
    gidx = jnp.where(valid, tok, 0).astype(jnp.int32).reshape(ntiles // 2, 2, rows)
    g = k_of_tile // PAIRS_PER_GROUP
    pair = k_of_tile % PAIRS_PER_GROUP
    ge3 = (pair >= 3).astype(jnp.int32)
    ge5 = (pair >= 5).astype(jnp.int32)
    ea = (g * MOE_PER_GROUP + ge3 + ge5).astype(jnp.int32)
    eb = (g * MOE_PER_GROUP + pair + 1 - 2 * ge3 - ge5).astype(jnp.int32)
    return ea, eb, nvalid, gidx


def _moe_sparse(text, keys, wg, wu, wd):
    t = keys.shape[0]
    d, ff = wg.shape[1], wg.shape[2]
    rows = MOE_ROWS
    ea, eb, nvalid, gidx = _route_plan(keys, rows)
    nsteps = gidx.shape[0]
    cur = lambda i, *_: (i, 0, 0)
    nxt = lambda i, *_: (jnp.minimum(i + 1, nsteps - 1), 0, 0)
    idx_spec = lambda m: pl.BlockSpec((1, 2, rows), m, memory_space=pltpu.SMEM)

    def weight_specs(half):
        up = lambda e: pl.BlockSpec((1, d, ff), lambda i, ea, eb, nv: ((ea, eb)[e][2 * i + half], 0, 0))
        dn = lambda e: pl.BlockSpec((1, ff, d), lambda i, ea, eb, nv: ((ea, eb)[e][2 * i + half], 0, 0))
        return [up(0), up(0), dn(0), up(1), up(1), dn(1)]

    grid_spec = pltpu.PrefetchScalarGridSpec(
        num_scalar_prefetch=3,
        grid=(nsteps,),
        in_specs=[idx_spec(cur), idx_spec(nxt), pl.BlockSpec(memory_space=pl.ANY)]
                 + weight_specs(0) + weight_specs(1),
        out_specs=pl.BlockSpec(memory_space=pl.ANY),
        scratch_shapes=[pltpu.VMEM((2, rows * TEXT_PITCH, LANES), F32),
                        pltpu.VMEM((2, rows * OUT_PITCH, LANES), F32),
                        pltpu.SemaphoreType.DMA((2,)), pltpu.SemaphoreType.DMA((2,))])
    return pl.pallas_call(
        functools.partial(_moe_sparse_kernel, rows=rows, nsteps=nsteps),
        out_shape=jax.ShapeDtypeStruct((t * OUT_PITCH, LANES), F32),
        grid_spec=grid_spec,
        compiler_params=_cparams("arbitrary"),
        name="moe_sparse",
    )(ea, eb, nvalid, gidx, gidx, text, *([wg, wu, wd] * 4))


def _store_token_major(ref, row0, n, pitch, x):
    for c in range(TOKEN_ROWS):
        ref[pl.ds(row0 + c, n, stride=pitch), :] = x[:, c * LANES:(c + 1) * LANES]


def _load_token_major(ref, n, pitch, dtype, row0=0):
    return jnp.concatenate([ref[pl.ds(row0 + c, n, stride=pitch), :].astype(dtype) for c in range(TOKEN_ROWS)],
                           axis=1)


def _final_kernel(h_ref, moe_ref, gt_ref, g_ref, o_ref):
    h = h_ref[...] + gt_ref[0] * _load_token_major(moe_ref, h_ref.shape[0], OUT_PITCH, F32)
    o_ref[...] = h * lax.rsqrt(jnp.mean(h * h, axis=-1, keepdims=True) + RMS_EPS) * g_ref[...]


def _final_norm(h, moe, gate, g, seq, tm):
    t, d = h.shape
    tpb = seq // tm
    tok = pl.BlockSpec((tm, d), lambda i: (i, 0))
    return pl.pallas_call(
        _final_kernel,
        out_shape=jax.ShapeDtypeStruct((t, d), F32),
        grid=(t // tm,),
        in_specs=[tok, pl.BlockSpec((tm * OUT_PITCH, LANES), lambda i: (i, 0)),
                  pl.BlockSpec((1, 1, d), lambda i: (i // tpb, 0, 0)),
                  pl.BlockSpec((1, d), lambda i: (0, 0))],
        out_specs=tok,
        compiler_params=_cparams("arbitrary"),
        name="final_norm",
    )(h, moe, gate, g)


def _pad_rows(w, n):
    return jnp.zeros((n, w.shape[1]), w.dtype).at[:w.shape[0]].set(w)


def kernel(x, c, ctx, c_ctx, w_ada, b_ada, g_mix, g_ffn, w_in, conv_w, conv_b, conv_ln_g, conv_ln_b,
           ssd_conv_w, ssd_conv_b, dt_bias, a_log, d_skip, ssd_norm_g, w_out, w_router_group,
           b_router_group, w_router_expert, b_router_expert, w_gate, w_up, w_down, g_final):
    bsz, seq, d = x.shape
    lc = ctx.shape[1]
    depth = w_in.shape[0]
    lo = 2 * D_CONV + D_SSD
    n_main = lo + D_XBC
    tm_x, tm_c = 512, lc
    consts = _ssd_consts()
    gw = (SSD_HEADS // SSD_GROUPS) * SSD_HEADDIM
    zero_state = jnp.zeros((bsz, 2, SSD_GROUPS, SSD_STATE, gw), F32)

    h_x = x.reshape(bsz * seq, d)
    h_c = ctx.reshape(bsz * lc, d)
    cvecs = jnp.zeros((SUBLANES, d), F32).at[:bsz].set(c).at[bsz].set(c_ctx)
    row = lambda v: v.reshape(1, -1)
    pending = None

    mods_all = _ada(cvecs, w_ada, b_ada)

    for l in range(depth):
        last = l == depth - 1
        mods = mods_all[l].reshape(SUBLANES, 6, d)
        mx = [mods[:bsz, j][:, None, :] for j in range(6)]
        mc = [jnp.broadcast_to(mods[bsz, j][None, None, :], (bsz, 1, d)) for j in range(6)]
        w_main = w_in[l][:, :n_main].astype(BF16)
        w_dt = jnp.zeros((d, LANES), F32).at[:, :2 * SSD_HEADS].set(w_in[l][:, n_main:]).astype(BF16)
        dtb_row = jnp.zeros((1, LANES), F32).at[0, :2 * SSD_HEADS].set(dt_bias[l].reshape(-1))
        a_row = jnp.zeros((1, LANES), F32).at[0, :2 * SSD_HEADS].set(-jnp.exp(a_log[l].astype(F32)).reshape(-1))
        dskip_row = jnp.repeat(d_skip[l], SSD_HEADDIM).reshape(1, D_SSD)
        convw_pad = _pad_rows(conv_w[l], 32)
        wout_b = w_out[l].astype(BF16)
        wr = jnp.zeros((d, LANES), F32).at[:, :MOE_GROUPS].set(w_router_group[l])
        wr = wr.at[:, ROUTER_LANE0:ROUTER_LANE0 + N_EXPERTS].set(w_router_expert[l]).astype(BF16)
        br = jnp.zeros((1, LANES), F32).at[0, :MOE_GROUPS].set(b_router_group[l])
        br = br.at[0, ROUTER_LANE0:ROUTER_LANE0 + N_EXPERTS].set(b_router_expert[l])
        wg_b, wu_b, wd_b = w_gate[l].astype(BF16), w_up[l].astype(BF16), w_down[l].astype(BF16)

        def out_proj(h, m, s, tm, yc, yr, yf, yb, z, col_conv=None):
            return _out_proj(yc, yr, yf, yb, z, h, m[2], row(conv_ln_g[l]), row(conv_ln_b[l]),
                             row(ssd_norm_g[l]), wout_b, row(g_ffn[l]), m[3], m[4], wr, br, s, tm,
                             col_conv=col_conv)

        in_w = (row(g_mix[l]), w_main, w_dt, ssd_conv_w[l], ssd_conv_b[l], dtb_row)
        v_c, z_c, xbc_c, dt_c = _in_proj(h_c, mc[0], mc[1], *in_w, lc, tm_c)
        yf_c, yb_c, h_ctx = _ssd(xbc_c, dt_c, a_row, dskip_row, zero_state, consts, bsz, lc)

        if pending is None:
            v, z, xbc, dt = _in_proj(h_x, mx[0], mx[1], *in_w, seq, tm_x)
        else:
            h_x, v, z, xbc, dt = _in_proj(h_x, mx[0], mx[1], *in_w, seq, tm_x, moe=pending[0], gate=pending[1])
        yf, yb, _ = _ssd(xbc, dt, a_row, dskip_row, h_ctx, consts, bsz, seq)
        half = D_CONV // 2
        yr = _row_conv(v.reshape(bsz, seq, D_CONV), convw_pad, row(conv_b[l]), half // LANES,
                       half // LANES).reshape(bsz * seq, half)
        h_x, text, _, keys_t = out_proj(h_x, mx, seq, tm_x, None, yr, yf, yb, z,
                                        col_conv=(v, convw_pad, row(conv_b[l])))
        moe_out = _moe_sparse(text, keys_t[META_KEY].astype(jnp.int32), wg_b, wu_b, wd_b)
        pending = (moe_out, mx[5])

        if not last:
            y_all = _seg_conv(v_c, convw_pad, row(conv_b[l]), lc, D_CONV, tm_c)
            h_c, text_c, comb_c, _ = out_proj(h_c, mc, lc, tm_c, y_all[:, :half], y_all[:, half:], yf_c, yb_c, z_c)
            t_c = text_c.reshape(bsz * lc, TEXT_PITCH, LANES)[:, :TOKEN_ROWS].reshape(bsz * lc, d)
            h_c = _moe(t_c, comb_c, h_c, mc[5], wg_b, wu_b, wd_b, bsz * lc, bsz * lc)

    out = _final_norm(h_x, pending[0], pending[1], row(g_final), seq, tm_x)
    return out.reshape(bsz, seq, d).astype(x.dtype)
```

```python
import functools

import jax
import jax.numpy as jnp
from jax import lax
from jax.experimental import pallas as pl
from jax.experimental.pallas import tpu as pltpu

F32 = jnp.float32
BF16 = jnp.bfloat16

RMS_EPS = 1e-6
LN_EPS = 1e-5

GRID_W = 64
D_CONV = 512
D_SSD = 512
SSD_HEADS = 8
SSD_HEADDIM = 64
SSD_GROUPS = 2
SSD_STATE = 128
D_BC = SSD_GROUPS * SSD_STATE
D_XBC = D_SSD + 2 * D_BC
MOE_GROUPS = 4
MOE_PER_GROUP = 4
N_EXPERTS = MOE_GROUPS * MOE_PER_GROUP

LANES = 128
SUBLANES = 8
CONV_PAD = 16
SSD_CHUNK = 128
ROUTER_LANE0 = MOE_GROUPS
PAIRS_PER_GROUP = MOE_PER_GROUP * (MOE_PER_GROUP - 1) // 2
N_BUCKETS = MOE_GROUPS * PAIRS_PER_GROUP
META_WA, META_WB, META_KEY = 0, 1, 2
MOE_ROWS = 256
ROWS_PER_ITER = 4
TOKEN_ROWS = 8
TEXT_PITCH = TOKEN_ROWS + 1
OUT_PITCH = TOKEN_ROWS
VMEM_LIMIT = 56 * 1024 * 1024


def _cparams(*sem):
    return pltpu.CompilerParams(dimension_semantics=sem, vmem_limit_bytes=VMEM_LIMIT)


def _dot(a, b):
    return jnp.dot(a, b, preferred_element_type=F32)


def _silu(x):
    return x * jax.nn.sigmoid(x)


def _split_bf16(x):
    hi = x.astype(BF16)
    lo = (x - hi.astype(F32)).astype(BF16)
    return hi, lo


def _ada_kernel(c_ref, w_ref, b_ref, o_ref):
    c = c_ref[...]
    o_ref[0] = jnp.dot(_silu(c), w_ref[0], preferred_element_type=F32,
                       precision=lax.Precision.HIGHEST) + b_ref[0]


def _ada(cvecs, w_ada, b_ada):
    depth, d, n = w_ada.shape
    bn = 1536
    rows = cvecs.shape[0]
    return pl.pallas_call(
        _ada_kernel,
        out_shape=jax.ShapeDtypeStruct((depth, rows, n), F32),
        grid=(depth, n // bn),
        in_specs=[pl.BlockSpec((rows, d), lambda l, j: (0, 0)),
                  pl.BlockSpec((1, d, bn), lambda l, j: (l, 0, j)),
                  pl.BlockSpec((1, 1, bn), lambda l, j: (l, 0, j))],
        out_specs=pl.BlockSpec((1, rows, bn), lambda l, j: (l, 0, j)),
        compiler_params=_cparams("arbitrary", "arbitrary"),
        name="ada",
    )(cvecs, w_ada, b_ada.reshape(depth, 1, n))


def _in_kernel(*refs, merge, tm, tpb, k):
    if merge:
        (xp_ref, x_ref, xn_ref, mp_ref, m_ref, mn_ref, gt_ref, sh_ref, sc_ref, g_ref, w_ref, wdt_ref,
         cw_ref, cb_ref, dtb_ref, h_ref, v_ref, z_ref, xbc_ref, dt_ref, scr, oscr) = refs
    else:
        (xp_ref, x_ref, xn_ref, sh_ref, sc_ref, g_ref, w_ref, wdt_ref,
         cw_ref, cb_ref, dtb_ref, v_ref, z_ref, xbc_ref, dt_ref, scr, oscr) = refs
        mp_ref = m_ref = mn_ref = None
    i = pl.program_id(0)
    first = (i % tpb) == 0
    last = (i % tpb) == tpb - 1
    nparts = scr.shape[0]
    n = tm // nparts
    nslab = D_XBC // LANES
    ng = n // SUBLANES

    def stream(xr, mr, r0, rows):
        x = xr[r0:r0 + rows, :]
        if merge:
            x = x + gt_ref[0] * _load_token_major(mr, rows, OUT_PITCH, F32, row0=r0 * OUT_PITCH)
        return x

    def rows_part(p):
        lo, hi = p * n, (p + 1) * n
        x_mid = stream(x_ref, m_ref, lo, n)
        if merge:
            h_ref[lo:hi, :] = x_mid
        before = stream(xp_ref, mp_ref, 0, SUBLANES) if p == 0 else stream(x_ref, m_ref, lo - SUBLANES, SUBLANES)
        after = (stream(xn_ref, mn_ref, 0, SUBLANES) if p == nparts - 1
                 else stream(x_ref, m_ref, hi, SUBLANES))
        x = jnp.concatenate([before, x_mid, after], axis=0)
        ms = jnp.mean(x * x, axis=-1, keepdims=True)
        h = x * lax.rsqrt(ms + RMS_EPS) * g_ref[...]
        h = (h * (1.0 + sc_ref[0]) + sh_ref[0]).astype(BF16)
        mid = slice(SUBLANES, SUBLANES + n)
        yield
        xbc0 = 2 * D_CONV + D_SSD

        def project_xbc(c2):
            xe = _dot(h, w_ref[:, xbc0 + 2 * c2 * LANES:xbc0 + (2 * c2 + 2) * LANES])
            for c in (2 * c2, 2 * c2 + 1):
                cs = slice((c - 2 * c2) * LANES, (c - 2 * c2 + 1) * LANES)
                head = xe[0:SUBLANES, cs]
                tail = xe[SUBLANES + n:, cs]
                scr[p, c, 0:SUBLANES, :] = jnp.where(first, 0.0, head) if p == 0 else head
                scr[p, c, mid, :] = xe[mid, cs]
                scr[p, c, SUBLANES + n:2 * SUBLANES + n, :] = (jnp.where(last, 0.0, tail) if p == nparts - 1
                                                                else tail)

        def conv_slab(c):
            cs = slice(c * LANES, (c + 1) * LANES)
            slab = scr.at[p, c]
            nb = 2 if ng % (2 * SUBLANES) == 0 else 1
            gb = ng // nb
            for blk in range(nb):
                loaded = {}

                def phase(q, grp):
                    if (q, grp) not in loaded:
                        start = SUBLANES * (1 + grp + blk * gb) + q
                        loaded[q, grp] = slab[pl.ds(start, gb, stride=SUBLANES), :]
                    return loaded[q, grp]

                for q in range(SUBLANES):
                    acc = jnp.broadcast_to(cb_ref[:, cs], (gb, LANES))
                    for j in range(k):
                        off = q + j - k // 2
                        acc = acc + cw_ref[j:j + 1, cs] * phase(off % SUBLANES, off // SUBLANES)
                    oscr[p, c, pl.ds(SUBLANES * blk * gb + q, gb, stride=SUBLANES), :] = _silu(acc)
            xbc_ref[lo:hi, cs] = oscr[p, c]

        def glu():
            u = _dot(h, w_ref[:, 0:D_CONV])[mid]
            gate = _dot(h, w_ref[:, D_CONV:2 * D_CONV])[mid]
            v_ref[lo:hi, :] = u * jax.nn.sigmoid(gate)

        def z_and_dt():
            z_ref[lo:hi, :] = _dot(h, w_ref[:, 2 * D_CONV:2 * D_CONV + D_SSD])[mid]
            xdt = _dot(h, wdt_ref[...])[mid] + dtb_ref[...]
            dt_ref[lo:hi, :] = jnp.maximum(xdt, 0.0) + jnp.log1p(jnp.exp(-jnp.abs(xdt)))

        project_xbc(0)
        for c2 in range(1, nslab // 2):
            project_xbc(c2)
            conv_slab(2 * c2 - 2)
            conv_slab(2 * c2 - 1)
            yield
        glu()
        conv_slab(nslab - 2)
        yield
        z_and_dt()
        conv_slab(nslab - 1)

    parts = [rows_part(p) for p in range(nparts)]
    while parts:
        parts = [g for g in parts if next(g, StopIteration) is not StopIteration]


def _in_proj(x, shift, scale, g, w_main, w_dt, conv_taps, k, conv_b, dt_bias, seq, tm, moe=None, gate=None):
    t, d = x.shape
    tpb = seq // tm
    n_main = w_main.shape[1]
    merge = moe is not None
    nparts = 1
    nb8 = tm // SUBLANES
    last8 = t // SUBLANES - 1
    tok = lambda i: (i, 0)
    prev = lambda i: (jnp.maximum(i * nb8 - 1, 0), 0)
    nxt = lambda i: (jnp.minimum((i + 1) * nb8, last8), 0)
    per_batch = lambda i: (i // tpb, 0, 0)
    const = lambda i: (0, 0)
    tok_d = pl.BlockSpec((tm, d), tok)
    mod = pl.BlockSpec((1, 1, d), per_batch)
    halo = lambda m: pl.BlockSpec((SUBLANES, d), m)
    mhalo = lambda m: pl.BlockSpec((SUBLANES * OUT_PITCH, LANES), m)
    outs = (jax.ShapeDtypeStruct((t, D_CONV), F32), jax.ShapeDtypeStruct((t, D_SSD), F32),
            jax.ShapeDtypeStruct((t, D_XBC), F32), jax.ShapeDtypeStruct((t, LANES), F32))
    out_specs = (pl.BlockSpec((tm, D_CONV), tok), pl.BlockSpec((tm, D_SSD), tok),
                 pl.BlockSpec((tm, D_XBC), tok), pl.BlockSpec((tm, LANES), tok))
    args = (x, x, x)
    in_specs = [halo(prev), tok_d, halo(nxt)]
    if merge:
        args += (moe, moe, moe, gate)
        in_specs += [mhalo(prev), pl.BlockSpec((tm * OUT_PITCH, LANES), tok), mhalo(nxt), mod]
        outs = (jax.ShapeDtypeStruct((t, d), F32),) + outs
        out_specs = (tok_d,) + out_specs
    return pl.pallas_call(
        functools.partial(_in_kernel, merge=merge, tm=tm, tpb=tpb, k=k),
        out_shape=outs,
        grid=(t // tm,),
        in_specs=in_specs + [mod, mod, pl.BlockSpec((1, d), const),
                             pl.BlockSpec((d, n_main), const), pl.BlockSpec((d, LANES), const),
                             pl.BlockSpec((SUBLANES, D_XBC), const), pl.BlockSpec((1, D_XBC), const),
                             pl.BlockSpec((1, LANES), const)],
        out_specs=out_specs,
        scratch_shapes=[pltpu.VMEM((nparts, D_XBC // LANES, tm // nparts + 2 * SUBLANES, LANES), F32),
                        pltpu.VMEM((nparts, D_XBC // LANES, tm // nparts, LANES), F32)],
        compiler_params=_cparams("arbitrary"),
        name="in_proj",
    )(*args, shift, scale, g, w_main, w_dt, conv_taps, conv_b.reshape(1, D_XBC), dt_bias)


def _segconv_kernel(v_ref, w_ref, b_ref, o_ref, scr, *, seg, nseg, k, cb):
    zeros = jnp.zeros((CONV_PAD, cb), F32)
    scr[0:CONV_PAD, :] = zeros
    scr[CONV_PAD + seg:2 * CONV_PAD + seg, :] = zeros

    def body(s, carry):
        base = pl.multiple_of(s * seg, seg)
        scr[CONV_PAD:CONV_PAD + seg, :] = v_ref[pl.ds(base, seg), :]
        for c in range(cb // LANES):
            cs = slice(c * LANES, (c + 1) * LANES)
            acc = jnp.broadcast_to(b_ref[:, cs], (seg, LANES))
            for j in range(k):
                acc = acc + w_ref[j:j + 1, cs] * scr[pl.ds(CONV_PAD + j - k // 2, seg), cs]
            o_ref[pl.ds(base, seg), cs] = acc
        return carry

    lax.fori_loop(0, nseg, body, 0)


def _seg_conv(v, w_pad, b, seg, cb, tm):
    t = v.shape[0]
    k = 31
    tok = lambda i: (i, 0)
    const = lambda i: (0, 0)
    return pl.pallas_call(
        functools.partial(_segconv_kernel, seg=seg, nseg=tm // seg, k=k, cb=cb),
        out_shape=jax.ShapeDtypeStruct((t, cb), F32),
        grid=(t // tm,),
        in_specs=[pl.BlockSpec((tm, cb), tok), pl.BlockSpec((w_pad.shape[0], cb), const),
                  pl.BlockSpec((1, cb), const)],
        out_specs=pl.BlockSpec((tm, cb), tok),
        scratch_shapes=[pltpu.VMEM((seg + 2 * CONV_PAD, cb), F32)],
        compiler_params=_cparams("arbitrary"),
        name="seg_conv",
    )(v, w_pad, b)


def _phase_conv_segments(v_ref, w_ref, b_ref, o_ref, seg0, nseg, k, cs=slice(0, LANES), unroll=False):
    seg = SUBLANES * SUBLANES
    sub = lax.broadcasted_iota(jnp.int32, (SUBLANES, LANES), 0)

    def body(s, carry):
        base = s * seg if unroll else pl.multiple_of(s * seg, seg)
        x = [v_ref[pl.ds(base + j, SUBLANES, stride=SUBLANES), :] for j in range(SUBLANES)]
        shifted = {}

        def tap(q, c):
            if c == 0:
                return x[q]
            if (q, c) not in shifted:
                moved = pltpu.roll(x[q], (-c) % SUBLANES, axis=0)
                shifted[q, c] = jnp.where((sub + c >= 0) & (sub + c < SUBLANES), moved, 0.0)
            return shifted[q, c]

        for j in range(SUBLANES):
            acc = jnp.broadcast_to(b_ref[:, cs], (SUBLANES, LANES))
            for t in range(k):
                off = j + t - k // 2
                acc = acc + w_ref[t:t + 1, cs] * tap(off % SUBLANES, off // SUBLANES)
            o_ref[pl.ds(base + j, SUBLANES, stride=SUBLANES), :] = acc
        return carry

    if unroll:
        for s in range(seg0, seg0 + nseg):
            body(s, 0)
    else:
        lax.fori_loop(seg0, seg0 + nseg, body, 0)


def _rowconv_kernel(v_ref, w_ref, b_ref, o_ref, scr, *, width, rows, k):
    half = k // 2
    pad = jnp.zeros((half * width, LANES), F32)
    scr[0:half * width, :] = pad
    scr[(half + rows) * width:(2 * half + rows) * width, :] = pad
    scr[half * width:(half + rows) * width, :] = v_ref[0]

    def body(i, carry):
        r = i * ROWS_PER_ITER
        accs = [jnp.broadcast_to(b_ref[...], (width, LANES)) for _ in range(ROWS_PER_ITER)]
        for j in range(k + ROWS_PER_ITER - 1):
            start = pl.multiple_of((r + j) * width, width)
            window = scr[pl.ds(start, width), :]
            for o in range(ROWS_PER_ITER):
                if 0 <= j - o < k:
                    accs[o] = accs[o] + w_ref[j - o:j - o + 1, :] * window
        for o in range(ROWS_PER_ITER):
            o_ref[0, pl.ds(pl.multiple_of((r + o) * width, width), width), :] = accs[o]
        return carry

    lax.fori_loop(0, rows // ROWS_PER_ITER, body, 0)


def _row_conv(v3, w_pad, b, lane_block0, nblk):
    bsz, seq, _ = v3.shape
    rows = seq // GRID_W
    k = 31
    return pl.pallas_call(
        functools.partial(_rowconv_kernel, width=GRID_W, rows=rows, k=k),
        out_shape=jax.ShapeDtypeStruct((bsz, seq, nblk * LANES), F32),
        grid=(bsz, nblk),
        in_specs=[pl.BlockSpec((1, seq, LANES), lambda bi, c: (bi, 0, lane_block0 + c)),
                  pl.BlockSpec((w_pad.shape[0], LANES), lambda bi, c: (0, lane_block0 + c)),
                  pl.BlockSpec((1, LANES), lambda bi, c: (0, lane_block0 + c))],
        out_specs=pl.BlockSpec((1, seq, LANES), lambda bi, c: (bi, 0, c)),
        scratch_shapes=[pltpu.VMEM(((rows + 2 * (k // 2)) * GRID_W, LANES), F32)],
        compiler_params=_cparams("arbitrary", "arbitrary"),
        name="row_conv",
    )(v3, w_pad, b)


def _ssd_kernel(xf_ref, xb_ref, dtf_ref, dtb_ref, a_ref, dskip_ref, h0_ref, tril_ref, triu_ref, e_ref,
                yf_ref, yb_ref, hfin_ref, s_ref, *, chunk, nchunks, bsz):
    i = pl.program_id(0)

    @pl.when(i == 0)
    def _():
        s_ref[...] = h0_ref[...]

    L = chunk
    hpg = SSD_HEADS // SSD_GROUPS
    gw = hpg * SSD_HEADDIM
    lane = lax.broadcasted_iota(jnp.int32, (L, LANES), 1)
    row_i = lax.broadcasted_iota(jnp.int32, (L, L), 0)
    col_i = lax.broadcasted_iota(jnp.int32, (L, L), 1)
    head_of_lane = lax.broadcasted_iota(jnp.int32, (L, gw), 1) // SSD_HEADDIM

    def direction(b, d, x_ref, dt_ref, y_ref):
        xall = x_ref[b]
        x = xall[:, :D_SSD]
        bm = xall[:, D_SSD:D_SSD + D_BC]
        cm = xall[:, D_SSD + D_BC:]
        dt = dt_ref[b]
        dmask = (lane >= SSD_HEADS * d) & (lane < SSD_HEADS * (d + 1))
        da = jnp.where(dmask, dt * a_ref[...], 0.0)
        tri2 = tril_ref[...] if d == 0 else triu_ref[...]
        cum = _dot(tri2, jnp.concatenate(_split_bf16(da), axis=0))
        yield
        tot = cum[L - 1:L, :] if d == 0 else cum[0:1, :]
        e_cum = jnp.where(dmask, jnp.exp(cum), 0.0)
        w_dt = jnp.where(dmask, jnp.exp(tot - cum) * dt, 0.0)
        e_tot = jnp.where(dmask[0:1], jnp.exp(tot), 0.0)
        stack = jnp.concatenate([e_cum, w_dt, jnp.broadcast_to(e_tot, (SUBLANES, LANES))], axis=0)
        ex = _dot(jnp.concatenate(_split_bf16(stack), axis=1), e_ref[...])
        yield
        e_cum_x = ex[0:L]
        w_dt_x = ex[L:2 * L]
        e_tot_x = ex[2 * L:2 * L + 1]
        src_t = (cum - jnp.log(dt)).T
        xw = (x * w_dt_x).astype(BF16)
        bm_t = bm.T
        yield
        mask = (row_i >= col_i) if d == 0 else (row_i <= col_i)
        ys = []
        for g in range(SSD_GROUPS):
            cg = cm[:, g * SSD_STATE:(g + 1) * SSD_STATE].astype(BF16)
            bg = bm[:, g * SSD_STATE:(g + 1) * SSD_STATE].astype(BF16)
            cb = lax.dot_general(cg, bg, (((1,), (1,)), ((), ())), preferred_element_type=F32).astype(BF16)
            yield
            xg = x[:, g * gw:(g + 1) * gw].astype(BF16)
            gs, xs = [], []
            for hh in range(hpg):
                c = SSD_HEADS * d + hpg * g + hh
                seg = cum[:, c:c + 1] - src_t[c:c + 1, :]
                gs.append(jnp.exp(jnp.where(mask, seg, -jnp.inf)).astype(BF16) * cb)
                xs.append(jnp.where(head_of_lane == hh, xg, jnp.zeros_like(xg)))
            y_diag = _dot(jnp.concatenate(gs, axis=1), jnp.concatenate(xs, axis=0))
            yield
            sg = s_ref[b, d, g]
            y_off = _dot(cg, sg.astype(BF16)) * e_cum_x[:, g * gw:(g + 1) * gw]
            s_ref[b, d, g] = (sg * e_tot_x[:, g * gw:(g + 1) * gw]
                              + _dot(bm_t[g * SSD_STATE:(g + 1) * SSD_STATE, :].astype(BF16),
                                     xw[:, g * gw:(g + 1) * gw]))
            ys.append(y_diag + y_off)
        y = jnp.concatenate(ys, axis=1)
        if d == 0:
            y = y + dskip_ref[...] * x
        y_ref[b] = y

    scans = [direction(b, d, refs[0], refs[1], refs[2]) for b in range(bsz)
             for d, refs in enumerate(((xf_ref, dtf_ref, yf_ref), (xb_ref, dtb_ref, yb_ref)))]
    while scans:
        scans = [s for s in scans if next(s, StopIteration) is not StopIteration]

    @pl.when(i == nchunks - 1)
    def _():
        hfin_ref[...] = s_ref[...]


def _ssd(xbc, dt, a_row, dskip_row, h0, consts, bsz, seq):
    t = xbc.shape[0]
    L = SSD_CHUNK
    n = seq // L
    tril2, triu2, expand2 = consts
    xbc3 = xbc.reshape(bsz, seq, D_XBC)
    dt3 = dt.reshape(bsz, seq, LANES)
    fwd = lambda i: (0, i, 0)
    bwd = lambda i: (0, n - 1 - i, 0)
    const = lambda i: (0, 0)
    state = pl.BlockSpec(h0.shape, lambda i: (0, 0, 0, 0, 0))
    y_f, y_b, h_fin = pl.pallas_call(
        functools.partial(_ssd_kernel, chunk=L, nchunks=n, bsz=bsz),
        out_shape=(jax.ShapeDtypeStruct((bsz, seq, D_SSD), F32), jax.ShapeDtypeStruct((bsz, seq, D_SSD), F32),
                   jax.ShapeDtypeStruct(h0.shape, F32)),
        grid=(n,),
        in_specs=[pl.BlockSpec((bsz, L, D_XBC), fwd), pl.BlockSpec((bsz, L, D_XBC), bwd),
                  pl.BlockSpec((bsz, L, LANES), fwd), pl.BlockSpec((bsz, L, LANES), bwd),
                  pl.BlockSpec((1, LANES), const), pl.BlockSpec((1, D_SSD), const), state,
                  pl.BlockSpec((L, 2 * L), const), pl.BlockSpec((L, 2 * L), const),
                  pl.BlockSpec((2 * LANES, D_SSD), const)],
        out_specs=(pl.BlockSpec((bsz, L, D_SSD), fwd), pl.BlockSpec((bsz, L, D_SSD), bwd), state),
        scratch_shapes=[pltpu.VMEM(h0.shape, F32)],
        compiler_params=_cparams("arbitrary"),
        name="ssd_scan",
    )(xbc3, xbc3, dt3, dt3, a_row, dskip_row, h0, tril2, triu2, expand2)
    return y_f.reshape(t, D_SSD), y_b.reshape(t, D_SSD), h_fin


def _ssd_consts():
    L = SSD_CHUNK
    r = jnp.arange(L)
    tril = (r[:, None] >= r[None, :]).astype(BF16)
    triu = (r[:, None] <= r[None, :]).astype(BF16)
    c = jnp.arange(LANES)[:, None]
    col = jnp.arange(D_SSD)[None, :]
    expand = ((c < 2 * SSD_HEADS) & ((c % SSD_HEADS) == col // SSD_HEADDIM)).astype(BF16)
    return (jnp.concatenate([tril, tril], axis=1), jnp.concatenate([triu, triu], axis=1),
            jnp.concatenate([expand, expand], axis=0))


def _out_kernel(*refs, cw, nparts, fuse_conv):
    if fuse_conv:
        nblk = cw // LANES
        v_refs, (cvw_ref, cvb_ref), rest, ycs = refs[:nblk], refs[nblk:nblk + 2], refs[nblk + 2:-1], refs[-1]
        seg = SUBLANES * SUBLANES

        def load_yc(rs):
            for c in range(nblk):
                _phase_conv_segments(v_refs[c], cvw_ref, cvb_ref, ycs.at[c], rs.start // seg,
                                     (rs.stop - rs.start) // seg, 31, cs=slice(c * LANES, (c + 1) * LANES),
                                     unroll=True)
            return jnp.concatenate([ycs[c, rs, :] for c in range(nblk)], axis=1)
    else:
        yc_ref, rest = refs[0], refs[1:]
        load_yc = lambda rs: yc_ref[rs, :]
    n = rest[4].shape[0] // nparts
    parts = [_out_rows(slice(p * n, (p + 1) * n), p * n, n, cw, load_yc, *rest) for p in range(nparts)]
    while parts:
        parts = [p for p in parts if next(p, StopIteration) is not StopIteration]


def _out_rows(rs, r0, n, cw, load_yc, yr_ref, yf_ref, yb_ref, z_ref, hx_ref, gt1_ref, lng_ref, lnb_ref, ng_ref,
              wout_ref, gffn_ref, sh2_ref, sc2_ref, wr_ref, br_ref, hnew_ref, text_ref, comb_ref, keys_ref):
    yc = load_yc(rs)
    yield
    yr = yr_ref[rs, :]
    nconv = yc.shape[1] + yr.shape[1]
    mu = (jnp.sum(yc, axis=-1, keepdims=True) + jnp.sum(yr, axis=-1, keepdims=True)) / nconv
    dc = yc - mu
    dr = yr - mu
    var = (jnp.sum(dc * dc, axis=-1, keepdims=True) + jnp.sum(dr * dr, axis=-1, keepdims=True)) / nconv
    inv = lax.rsqrt(var + LN_EPS)
    a_c = _silu(dc * inv * lng_ref[:, 0:cw] + lnb_ref[:, 0:cw])
    a_r = _silu(dr * inv * lng_ref[:, cw:] + lnb_ref[:, cw:])
    y = (yf_ref[rs, :] + yb_ref[rs, :]) * _silu(z_ref[rs, :])
    ssd = y * lax.rsqrt(jnp.mean(y * y, axis=-1, keepdims=True) + RMS_EPS) * ng_ref[...]
    yield
    mix = (_dot(a_c.astype(BF16), wout_ref[0:cw, :]) + _dot(a_r.astype(BF16), wout_ref[cw:D_CONV, :])
           + _dot(ssd.astype(BF16), wout_ref[D_CONV:, :]))
    yield
    h = hx_ref[rs, :] + gt1_ref[0] * mix
    hnew_ref[rs, :] = h
    t = h * lax.rsqrt(jnp.mean(h * h, axis=-1, keepdims=True) + RMS_EPS) * gffn_ref[...]
    t = t * (1.0 + sc2_ref[0]) + sh2_ref[0]
    tb = t.astype(BF16)

    logits = _dot(tb, wr_ref[...]) + br_ref[...]
    yield
    lane = lax.broadcasted_iota(jnp.int32, logits.shape, 1).astype(F32)
    big = float(LANES)
    ninf = -jnp.inf
    gl = jnp.where(lane < MOE_GROUPS, logits, ninf)
    gmax = jnp.max(gl, axis=-1, keepdims=True)
    gsel = jnp.min(jnp.where(gl == gmax, lane, big), axis=-1, keepdims=True)
    gprob = 1.0 / jnp.sum(jnp.exp(gl - gmax), axis=-1, keepdims=True)
    lo = ROUTER_LANE0 + gsel * MOE_PER_GROUP
    emask = (lane >= lo) & (lane < lo + MOE_PER_GROUP)
    el = jnp.where(emask, logits, ninf)
    e1 = jnp.max(el, axis=-1, keepdims=True)
    i1 = jnp.min(jnp.where(el == e1, lane, big), axis=-1, keepdims=True)
    el2 = jnp.where(lane == i1, ninf, el)
    e2 = jnp.max(el2, axis=-1, keepdims=True)
    i2 = jnp.min(jnp.where(el2 == e2, lane, big), axis=-1, keepdims=True)
    p2 = jnp.exp(e2 - e1)
    w1 = gprob / (1.0 + p2)
    w2 = gprob * p2 / (1.0 + p2)
    comb_ref[rs, :] = jnp.where(lane == i1, w1, 0.0) + jnp.where(lane == i2, w2, 0.0)
    first = i1 < i2
    l_lo = jnp.minimum(i1, i2) - lo
    l_hi = jnp.maximum(i1, i2) - lo
    key = gsel * PAIRS_PER_GROUP + l_lo * (7.0 - l_lo) * 0.5 + (l_hi - l_lo - 1.0)
    meta = (jnp.where(lane == META_WA, jnp.where(first, w1, w2), 0.0)
            + jnp.where(lane == META_WB, jnp.where(first, w2, w1), 0.0)
            + jnp.where(lane == META_KEY, key, 0.0))
    _store_token_major(text_ref, r0 * TEXT_PITCH, n, TEXT_PITCH, t)
    text_ref[pl.ds(r0 * TEXT_PITCH + TOKEN_ROWS, n, stride=TEXT_PITCH), :] = meta
    keys_ref[:, rs] = meta.T[0:SUBLANES, :]


def _out_proj(yc, yr, yf, yb, z, hx, gt1, lng, lnb, ng, wout, gffn, sh2, sc2, wr, br, seq, tm, col_conv=None):
    t, d = hx.shape
    tpb = seq // tm
    tok = lambda i: (i, 0)
    per_batch = lambda i: (i // tpb, 0, 0)
    const = lambda i: (0, 0)
    row = lambda n: pl.BlockSpec((1, n), const)
    if col_conv is None:
        cw = yc.shape[1]
        conv_args, conv_specs, scratch = (yc,), [pl.BlockSpec((tm, cw), tok)], []
    else:
        v, taps, bias = col_conv
        cw = D_CONV - yr.shape[1]
        nblk = cw // LANES
        conv_args = (v,) * nblk + (taps, bias)
        conv_specs = ([pl.BlockSpec((tm, LANES), lambda i, c=c: (i, c)) for c in range(nblk)]
                      + [pl.BlockSpec((taps.shape[0], cw), const), row(cw)])
        scratch = [pltpu.VMEM((nblk, tm, LANES), F32)]
    return pl.pallas_call(
        functools.partial(_out_kernel, cw=cw, nparts=max(1, tm // 256), fuse_conv=col_conv is not None),
        out_shape=(jax.ShapeDtypeStruct((t, d), F32), jax.ShapeDtypeStruct((t * TEXT_PITCH, LANES), F32),
                   jax.ShapeDtypeStruct((t, LANES), F32), jax.ShapeDtypeStruct((SUBLANES, t), F32)),
        grid=(t // tm,),
        scratch_shapes=scratch,
        in_specs=conv_specs + [pl.BlockSpec((tm, D_CONV - cw), tok),
                  pl.BlockSpec((tm, D_SSD), tok), pl.BlockSpec((tm, D_SSD), tok), pl.BlockSpec((tm, D_SSD), tok),
                  pl.BlockSpec((tm, d), tok), pl.BlockSpec((1, 1, d), per_batch),
                  row(D_CONV), row(D_CONV), row(D_SSD),
                  pl.BlockSpec((D_CONV + D_SSD, d), const), row(d),
                  pl.BlockSpec((1, 1, d), per_batch), pl.BlockSpec((1, 1, d), per_batch),
                  pl.BlockSpec((d, LANES), const), row(LANES)],
        out_specs=(pl.BlockSpec((tm, d), tok), pl.BlockSpec((tm * TEXT_PITCH, LANES), tok),
                   pl.BlockSpec((tm, LANES), tok), pl.BlockSpec((SUBLANES, tm), lambda i: (0, i))),
        compiler_params=_cparams("arbitrary"),
        name="out_proj",
    )(*conv_args, yr, yf, yb, z, hx, gt1, lng, lnb, ng, wout, gffn, sh2, sc2, wr, br)


def _moe_kernel(t_ref, comb_ref, hx_ref, gt2_ref, wg_ref, wu_ref, wd_ref, o_ref, acc_ref):
    e = pl.program_id(1)

    @pl.when(e == 0)
    def _():
        acc_ref[...] = jnp.zeros_like(acc_ref)

    t = t_ref[...].astype(BF16)
    comb = comb_ref[...]
    lane = lax.broadcasted_iota(jnp.int32, comb.shape, 1)
    cw = jnp.sum(jnp.where(lane == e + ROUTER_LANE0, comb, 0.0), axis=-1, keepdims=True)
    hid = _silu(_dot(t, wg_ref[0])) * _dot(t, wu_ref[0])
    acc_ref[...] += _dot((hid * cw).astype(BF16), wd_ref[0])

    @pl.when(e == N_EXPERTS - 1)
    def _():
        o_ref[...] = hx_ref[...] + gt2_ref[0] * acc_ref[...]


def _moe(t_in, comb, hx, gt2, wg, wu, wd, seq, tm):
    t, d = hx.shape
    ff = wg.shape[2]
    tpb = seq // tm
    tok = lambda i, e: (i, 0)
    per_batch = lambda i, e: (i // tpb, 0, 0)
    return pl.pallas_call(
        _moe_kernel,
        out_shape=jax.ShapeDtypeStruct((t, d), F32),
        grid=(t // tm, N_EXPERTS),
        in_specs=[pl.BlockSpec((tm, d), tok), pl.BlockSpec((tm, LANES), tok), pl.BlockSpec((tm, d), tok),
                  pl.BlockSpec((1, 1, d), per_batch),
                  pl.BlockSpec((1, d, ff), lambda i, e: (e, 0, 0)),
                  pl.BlockSpec((1, d, ff), lambda i, e: (e, 0, 0)),
                  pl.BlockSpec((1, ff, d), lambda i, e: (e, 0, 0))],
        out_specs=pl.BlockSpec((tm, d), tok),
        scratch_shapes=[pltpu.VMEM((tm, d), F32)],
        compiler_params=_cparams("arbitrary", "arbitrary"),
        name="moe_dense",
    )(t_in, comb, hx, gt2, wg, wu, wd)


def _moe_sparse_kernel(ea_ref, eb_ref, nvalid_ref, idx_ref, idx_next_ref, text_hbm, *rest, rows, nsteps):
    weights = (rest[0:6], rest[6:12])
    out_hbm, gbuf, obuf, gsem, ssem = rest[12:]
    i = pl.program_id(0)

    def gather_copy(tok, r, s):
        return pltpu.make_async_copy(text_hbm.at[pl.ds(tok * TEXT_PITCH, TEXT_PITCH), :],
                                     gbuf.at[s, pl.ds(r * TEXT_PITCH, TEXT_PITCH), :], gsem.at[s])

    def scatter_copy(tok, r, s):
        return pltpu.make_async_copy(obuf.at[s, pl.ds(r * OUT_PITCH, OUT_PITCH), :],
                                     out_hbm.at[pl.ds(tok * OUT_PITCH, OUT_PITCH), :], ssem.at[s])

    def scatter_wait(n, s):
        @pl.when(n == rows)
        def _():
            for r in range(rows):
                scatter_copy(0, r, s).wait()

        @pl.when(jnp.logical_and(n > 0, n < rows))
        def _():
            def body(r, carry):
                scatter_copy(0, r, s).wait()
                return carry
            lax.fori_loop(0, n, body, 0)

    @pl.when(i == 0)
    def _():
        for r in range(rows):
            gather_copy(idx_ref[0, 0, r], r, 0).start(priority=r % 2)

    for s in range(2):
        tile = 2 * i + s
        wga_ref, wua_ref, wda_ref, wgb_ref, wub_ref, wdb_ref = weights[s]
        nxt_ref, nxt_half = (idx_ref, 1) if s == 0 else (idx_next_ref, 0)
        n_here = nvalid_ref[tile]

        @pl.when(jnp.logical_or(tile == 0, nvalid_ref[jnp.maximum(tile - 1, 0)] > 0))
        def _():
            for r in range(rows):
                gather_copy(0, r, s).wait()

        @pl.when(tile >= 2)
        def _():
            scatter_wait(nvalid_ref[jnp.maximum(tile - 2, 0)], s)

        @pl.when(n_here > 0)
        def _():
            for r in range(rows):
                gather_copy(nxt_ref[0, nxt_half, r], r, 1 - s).start(priority=r % 2)
            gs = gbuf.at[s]
            t = _load_token_major(gs, rows, TEXT_PITCH, BF16)
            meta = gs[pl.ds(TOKEN_ROWS, rows, stride=TEXT_PITCH), :]
            wa = meta[:, META_WA:META_WA + 1]
            wb = meta[:, META_WB:META_WB + 1]
            ha = _silu(_dot(t, wga_ref[0])) * _dot(t, wua_ref[0]) * wa
            hb = _silu(_dot(t, wgb_ref[0])) * _dot(t, wub_ref[0]) * wb
            out = _dot(ha.astype(BF16), wda_ref[0]) + _dot(hb.astype(BF16), wdb_ref[0])
            _store_token_major(obuf.at[s], 0, rows, OUT_PITCH, out)

            @pl.when(n_here == rows)
            def _():
                for r in range(rows):
                    scatter_copy(idx_ref[0, s, r], r, s).start(priority=r % 2)

            @pl.when(n_here < rows)
            def _():
                def body(r, carry):
                    scatter_copy(idx_ref[0, s, r], r, s).start()
                    return carry
                lax.fori_loop(0, n_here, body, 0)

    @pl.when(i == nsteps - 1)
    def _():
        @pl.when(nvalid_ref[2 * i + 1] > 0)
        def _():
            for r in range(rows):
                gather_copy(0, r, 0).wait()
        scatter_wait(nvalid_ref[2 * i], 0)
        scatter_wait(nvalid_ref[2 * i + 1], 1)


def _route_plan(keys, rows):
    t = keys.shape[0]
    ntiles = 2 * (((t + N_BUCKETS * (rows - 1)) // rows + 1) // 2)
    kk = jnp.arange(N_BUCKETS, dtype=jnp.int32)
    cnt = jnp.sum((keys[None, :] == kk[:, None]).astype(jnp.int32), axis=1)
    off = jnp.cumsum(cnt) - cnt
    tiles_k = (cnt + rows - 1) // rows
    tile_end = jnp.cumsum(tiles_k)
    tile_off = tile_end - tiles_k
    order = jnp.argsort(keys, stable=True).astype(jnp.int32)
    tile = jnp.arange(ntiles, dtype=jnp.int32)
    in_k = (tile[:, None] >= tile_off[None, :]) & ((tile[:, None] < tile_end[None, :]) | (kk[None, :] == N_BUCKETS - 1))
    in_k = in_k & ((tiles_k[None, :] > 0) | (kk[None, :] == N_BUCKETS - 1))
    pick = lambda v: jnp.sum(jnp.where(in_k, v[None, :], 0), axis=1)
    k_of_tile = pick(kk)
    tile_in_k = tile - pick(tile_off)
    nvalid = jnp.clip(pick(cnt) - tile_in_k * rows, 0, rows)
    nvalid = jnp.where(tile < tile_end[-1], nvalid, 0).astype(jnp.int32)
    j = jnp.arange(rows, dtype=jnp.int32)
    pos = pick(off)[:, None] + tile_in_k[:, None] * rows + j[None, :]
    tok = order[jnp.clip(pos, 0, t - 1)]
    valid = j[None, :] < nvalid[:, None]
    gidx = jnp.where(valid, tok, 0).astype(jnp.int32).reshape(ntiles // 2, 2, rows)
    g = k_of_tile // PAIRS_PER_GROUP
    pair = k_of_tile % PAIRS_PER_GROUP
    ge3 = (pair >= 3).astype(jnp.int32)
    ge5 = (pair >= 5).astype(jnp.int32)
    ea = (g * MOE_PER_GROUP + ge3 + ge5).astype(jnp.int32)
    eb = (g * MOE_PER_GROUP + pair + 1 - 2 * ge3 - ge5).astype(jnp.int32)
    return ea, eb, nvalid, gidx


def _moe_sparse(text, keys, wg, wu, wd):
    t = keys.shape[0]
    d, ff = wg.shape[1], wg.shape[2]
    rows = MOE_ROWS
    ea, eb, nvalid, gidx = _route_plan(keys, rows)
    nsteps = gidx.shape[0]
    cur = lambda i, *_: (i, 0, 0)
    nxt = lambda i, *_: (jnp.minimum(i + 1, nsteps - 1), 0, 0)
    idx_spec = lambda m: pl.BlockSpec((1, 2, rows), m, memory_space=pltpu.SMEM)

    def weight_specs(half):
        up = lambda e: pl.BlockSpec((1, d, ff), lambda i, ea, eb, nv: ((ea, eb)[e][2 * i + half], 0, 0))
        dn = lambda e: pl.BlockSpec((1, ff, d), lambda i, ea, eb, nv: ((ea, eb)[e][2 * i + half], 0, 0))
        return [up(0), up(0), dn(0), up(1), up(1), dn(1)]

    grid_spec = pltpu.PrefetchScalarGridSpec(
        num_scalar_prefetch=3,
        grid=(nsteps,),
        in_specs=[idx_spec(cur), idx_spec(nxt), pl.BlockSpec(memory_space=pl.ANY)]
                 + weight_specs(0) + weight_specs(1),
        out_specs=pl.BlockSpec(memory_space=pl.ANY),
        scratch_shapes=[pltpu.VMEM((2, rows * TEXT_PITCH, LANES), F32),
                        pltpu.VMEM((2, rows * OUT_PITCH, LANES), F32),
                        pltpu.SemaphoreType.DMA((2,)), pltpu.SemaphoreType.DMA((2,))])
    return pl.pallas_call(
        functools.partial(_moe_sparse_kernel, rows=rows, nsteps=nsteps),
        out_shape=jax.ShapeDtypeStruct((t * OUT_PITCH, LANES), F32),
        grid_spec=grid_spec,
        compiler_params=_cparams("arbitrary"),
        name="moe_sparse",
    )(ea, eb, nvalid, gidx, gidx, text, *([wg, wu, wd] * 4))


def _store_token_major(ref, row0, n, pitch, x):
    for c in range(TOKEN_ROWS):
        ref[pl.ds(row0 + c, n, stride=pitch), :] = x[:, c * LANES:(c + 1) * LANES]


def _load_token_major(ref, n, pitch, dtype, row0=0):
    return jnp.concatenate([ref[pl.ds(row0 + c, n, stride=pitch), :].astype(dtype) for c in range(TOKEN_ROWS)],
                           axis=1)


def _final_kernel(h_ref, moe_ref, gt_ref, g_ref, o_ref):
    h = h_ref[...] + gt_ref[0] * _load_token_major(moe_ref, h_ref.shape[0], OUT_PITCH, F32)
    o_ref[...] = h * lax.rsqrt(jnp.mean(h * h, axis=-1, keepdims=True) + RMS_EPS) * g_ref[...]


def _final_norm(h, moe, gate, g, seq, tm):
    t, d = h.shape
    tpb = seq // tm
    tok = pl.BlockSpec((tm, d), lambda i: (i, 0))
    return pl.pallas_call(
        _final_kernel,
        out_shape=jax.ShapeDtypeStruct((t, d), F32),
        grid=(t // tm,),
        in_specs=[tok, pl.BlockSpec((tm * OUT_PITCH, LANES), lambda i: (i, 0)),
                  pl.BlockSpec((1, 1, d), lambda i: (i // tpb, 0, 0)),
                  pl.BlockSpec((1, d), lambda i: (0, 0))],
        out_specs=tok,
        compiler_params=_cparams("arbitrary"),
        name="final_norm",
    )(h, moe, gate, g)


def kernel(x, c, ctx, c_ctx, w_ada, b_ada, g_mix, g_ffn, w_in, conv_w, conv_b, conv_ln_g, conv_ln_b,
           ssd_conv_w, ssd_conv_b, dt_bias, a_log, d_skip, ssd_norm_g, w_out, w_router_group,
           b_router_group, w_router_expert, b_router_expert, w_gate, w_up, w_down, g_final):
    bsz, seq, d = x.shape
    lc = ctx.shape[1]
    depth = w_in.shape[0]
    lo = 2 * D_CONV + D_SSD
    n_main = lo + D_XBC
    tm_x, tm_c = 512, lc
    consts = _ssd_consts()
    gw = (SSD_HEADS // SSD_GROUPS) * SSD_HEADDIM
    zero_state = jnp.zeros((bsz, 2, SSD_GROUPS, SSD_STATE, gw), F32)

    h_x = x.reshape(bsz * seq, d)
    h_c = ctx.reshape(bsz * lc, d)
    cvecs = jnp.zeros((SUBLANES, d), F32).at[:bsz].set(c).at[bsz].set(c_ctx)
    row = lambda v: v.reshape(1, -1)
    pending = None

    mods_all = _ada(cvecs, w_ada, b_ada).reshape(depth, SUBLANES, 6, d)

    pad_lanes = lambda a, n: jnp.pad(a, [(0, 0)] * (a.ndim - 1) + [(0, n - a.shape[-1])])
    nh2 = 2 * SSD_HEADS
    w_dt_all = pad_lanes(w_in[:, :, n_main:], LANES).astype(BF16)
    dtb_all = pad_lanes(dt_bias.reshape(depth, 1, nh2), LANES)
    a_all = pad_lanes(-jnp.exp(a_log.astype(F32)).reshape(depth, 1, nh2), LANES)
    dskip_all = jnp.repeat(d_skip, SSD_HEADDIM, axis=1).reshape(depth, 1, D_SSD)
    convw_all = jnp.pad(conv_w, ((0, 0), (0, 32 - conv_w.shape[1]), (0, 0)))
    sconvw_all = jnp.pad(ssd_conv_w, ((0, 0), (0, SUBLANES - ssd_conv_w.shape[1]), (0, 0)))
    wr_all = pad_lanes(jnp.concatenate([w_router_group, w_router_expert], axis=2), LANES).astype(BF16)
    br_all = pad_lanes(jnp.concatenate([b_router_group, b_router_expert], axis=1), LANES).reshape(depth, 1, LANES)

    for l in range(depth):
        last = l == depth - 1
        mods = mods_all[l]
        mx = [mods[:bsz, j][:, None, :] for j in range(6)]
        mc = [jnp.broadcast_to(mods[bsz, j][None, None, :], (bsz, 1, d)) for j in range(6)]
        w_main = w_in[l][:, :n_main].astype(BF16)
        w_dt, dtb_row, a_row, dskip_row = w_dt_all[l], dtb_all[l], a_all[l], dskip_all[l]
        convw_pad, wr, br = convw_all[l], wr_all[l], br_all[l]
        wout_b = w_out[l].astype(BF16)
        wg_b, wu_b, wd_b = w_gate[l].astype(BF16), w_up[l].astype(BF16), w_down[l].astype(BF16)

        def out_proj(h, m, s, tm, yc, yr, yf, yb, z, col_conv=None):
            return _out_proj(yc, yr, yf, yb, z, h, m[2], row(conv_ln_g[l]), row(conv_ln_b[l]),
                             row(ssd_norm_g[l]), wout_b, row(g_ffn[l]), m[3], m[4], wr, br, s, tm,
                             col_conv=col_conv)

        in_w = (row(g_mix[l]), w_main, w_dt, sconvw_all[l], ssd_conv_w.shape[1], ssd_conv_b[l], dtb_row)
        v_c, z_c, xbc_c, dt_c = _in_proj(h_c, mc[0], mc[1], *in_w, lc, tm_c)
        yf_c, yb_c, h_ctx = _ssd(xbc_c, dt_c, a_row, dskip_row, zero_state, consts, bsz, lc)

        if pending is None:
            v, z, xbc, dt = _in_proj(h_x, mx[0], mx[1], *in_w, seq, tm_x)
        else:
            h_x, v, z, xbc, dt = _in_proj(h_x, mx[0], mx[1], *in_w, seq, tm_x, moe=pending[0], gate=pending[1])
        yf, yb, _ = _ssd(xbc, dt, a_row, dskip_row, h_ctx, consts, bsz, seq)
        half = D_CONV // 2
        yr = _row_conv(v.reshape(bsz, seq, D_CONV), convw_pad, row(conv_b[l]), half // LANES,
                       half // LANES).reshape(bsz * seq, half)
        h_x, text, _, keys_t = out_proj(h_x, mx, seq, tm_x, None, yr, yf, yb, z,
                                        col_conv=(v, convw_pad, row(conv_b[l])))
        moe_out = _moe_sparse(text, keys_t[META_KEY].astype(jnp.int32), wg_b, wu_b, wd_b)
        pending = (moe_out, mx[5])

        if not last:
            y_all = _seg_conv(v_c, convw_pad, row(conv_b[l]), lc, D_CONV, tm_c)
            h_c, text_c, comb_c, _ = out_proj(h_c, mc, lc, tm_c, y_all[:, :half], y_all[:, half:], yf_c, yb_c, z_c)
            t_c = text_c.reshape(bsz * lc, TEXT_PITCH, LANES)[:, :TOKEN_ROWS].reshape(bsz * lc, d)
            h_c = _moe(t_c, comb_c, h_c, mc[5], wg_b, wu_b, wd_b, bsz * lc, bsz * lc)

    out = _final_norm(h_x, pending[0], pending[1], row(g_final), seq, tm_x)
    return out.reshape(bsz, seq, d).astype(x.dtype)
```

```python
import functools

import jax
import jax.numpy as jnp
from jax import lax
from jax.experimental import pallas as pl
from jax.experimental.pallas import tpu as pltpu

F32 = jnp.float32
BF16 = jnp.bfloat16

RMS_EPS = 1e-6
LN_EPS = 1e-5

GRID_W = 64
D_CONV = 512
D_SSD = 512
SSD_HEADS = 8
SSD_HEADDIM = 64
SSD_GROUPS = 2
SSD_STATE = 128
D_BC = SSD_GROUPS * SSD_STATE
D_XBC = D_SSD + 2 * D_BC
MOE_GROUPS = 4
MOE_PER_GROUP = 4
N_EXPERTS = MOE_GROUPS * MOE_PER_GROUP

LANES = 128
SUBLANES = 8
CONV_PAD = 16
SSD_CHUNK = 128
ROUTER_LANE0 = MOE_GROUPS
PAIRS_PER_GROUP = MOE_PER_GROUP * (MOE_PER_GROUP - 1) // 2
N_BUCKETS = MOE_GROUPS * PAIRS_PER_GROUP
META_WA, META_WB, META_KEY = 0, 1, 2
MOE_ROWS = 256
ROWS_PER_ITER = 4
TOKEN_ROWS = 8
TEXT_PITCH = TOKEN_ROWS + 1
OUT_PITCH = TOKEN_ROWS
VMEM_LIMIT = 56 * 1024 * 1024


def _cparams(*sem):
    return pltpu.CompilerParams(dimension_semantics=sem, vmem_limit_bytes=VMEM_LIMIT)


def _dot(a, b):
    return jnp.dot(a, b, preferred_element_type=F32)


def _silu(x):
    return x * jax.nn.sigmoid(x)


def _split_bf16(x):
    hi = x.astype(BF16)
    lo = (x - hi.astype(F32)).astype(BF16)
    return hi, lo


def _ada_kernel(c_ref, w_ref, b_ref, o_ref):
    c = c_ref[...]
    o_ref[0] = jnp.dot(_silu(c), w_ref[0], preferred_element_type=F32,
                       precision=lax.Precision.HIGHEST) + b_ref[0]


def _ada(cvecs, w_ada, b_ada):
    depth, d, n = w_ada.shape
    bn = 1536
    rows = cvecs.shape[0]
    return pl.pallas_call(
        _ada_kernel,
        out_shape=jax.ShapeDtypeStruct((depth, rows, n), F32),
        grid=(depth, n // bn),
        in_specs=[pl.BlockSpec((rows, d), lambda l, j: (0, 0)),
                  pl.BlockSpec((1, d, bn), lambda l, j: (l, 0, j)),
                  pl.BlockSpec((1, 1, bn), lambda l, j: (l, 0, j))],
        out_specs=pl.BlockSpec((1, rows, bn), lambda l, j: (l, 0, j)),
        compiler_params=_cparams("arbitrary", "arbitrary"),
        name="ada",
    )(cvecs, w_ada, b_ada.reshape(depth, 1, n))


def _in_kernel(*refs, merge, tm, tpb, k):
    if merge:
        (xp_ref, x_ref, xn_ref, mp_ref, m_ref, mn_ref, gt_ref, sh_ref, sc_ref, g_ref, w_ref, wdt_ref,
         cw_ref, cb_ref, dtb_ref, h_ref, v_ref, z_ref, xbc_ref, dt_ref, scr, oscr) = refs
    else:
        (xp_ref, x_ref, xn_ref, sh_ref, sc_ref, g_ref, w_ref, wdt_ref,
         cw_ref, cb_ref, dtb_ref, v_ref, z_ref, xbc_ref, dt_ref, scr, oscr) = refs
        mp_ref = m_ref = mn_ref = None
    i = pl.program_id(0)
    first = (i % tpb) == 0
    last = (i % tpb) == tpb - 1
    nparts = scr.shape[0]
    n = tm // nparts
    nslab = D_XBC // LANES
    ng = n // SUBLANES

    def stream(xr, mr, r0, rows):
        x = xr[r0:r0 + rows, :]
        if merge:
            x = x + gt_ref[0] * _load_token_major(mr, rows, OUT_PITCH, F32, row0=r0 * OUT_PITCH)
        return x

    def rows_part(p):
        lo, hi = p * n, (p + 1) * n
        x_mid = stream(x_ref, m_ref, lo, n)
        if merge:
            h_ref[lo:hi, :] = x_mid
        before = stream(xp_ref, mp_ref, 0, SUBLANES) if p == 0 else stream(x_ref, m_ref, lo - SUBLANES, SUBLANES)
        after = (stream(xn_ref, mn_ref, 0, SUBLANES) if p == nparts - 1
                 else stream(x_ref, m_ref, hi, SUBLANES))
        x = jnp.concatenate([before, x_mid, after], axis=0)
        ms = jnp.mean(x * x, axis=-1, keepdims=True)
        h = x * lax.rsqrt(ms + RMS_EPS) * g_ref[...]
        h = (h * (1.0 + sc_ref[0]) + sh_ref[0]).astype(BF16)
        mid = slice(SUBLANES, SUBLANES + n)
        yield
        xbc0 = 2 * D_CONV + D_SSD

        def project_xbc(c2):
            xe = _dot(h, w_ref[:, xbc0 + 2 * c2 * LANES:xbc0 + (2 * c2 + 2) * LANES])
            for c in (2 * c2, 2 * c2 + 1):
                cs = slice((c - 2 * c2) * LANES, (c - 2 * c2 + 1) * LANES)
                head = xe[0:SUBLANES, cs]
                tail = xe[SUBLANES + n:, cs]
                scr[p, c, 0:SUBLANES, :] = jnp.where(first, 0.0, head) if p == 0 else head
                scr[p, c, mid, :] = xe[mid, cs]
                scr[p, c, SUBLANES + n:2 * SUBLANES + n, :] = (jnp.where(last, 0.0, tail) if p == nparts - 1
                                                                else tail)

        def conv_slab(c):
            cs = slice(c * LANES, (c + 1) * LANES)
            slab = scr.at[p, c]
            loaded = {}

            def phase(q, grp):
                if (q, grp) not in loaded:
                    loaded[q, grp] = slab[pl.ds(SUBLANES * (1 + grp) + q, ng, stride=SUBLANES), :]
                return loaded[q, grp]

            for q in range(SUBLANES):
                acc = jnp.broadcast_to(cb_ref[:, cs], (ng, LANES))
                for j in range(k):
                    off = q + j - k // 2
                    acc = acc + cw_ref[j:j + 1, cs] * phase(off % SUBLANES, off // SUBLANES)
                oscr[p, c, pl.ds(q, ng, stride=SUBLANES), :] = _silu(acc)
            xbc_ref[lo:hi, cs] = oscr[p, c]

        def glu():
            u = _dot(h, w_ref[:, 0:D_CONV])[mid]
            gate = _dot(h, w_ref[:, D_CONV:2 * D_CONV])[mid]
            v_ref[lo:hi, :] = u * jax.nn.sigmoid(gate)

        def z_and_dt():
            z_ref[lo:hi, :] = _dot(h, w_ref[:, 2 * D_CONV:2 * D_CONV + D_SSD])[mid]
            xdt = _dot(h, wdt_ref[...])[mid] + dtb_ref[...]
            dt_ref[lo:hi, :] = jnp.maximum(xdt, 0.0) + jnp.log1p(jnp.exp(-jnp.abs(xdt)))

        project_xbc(0)
        for c2 in range(1, nslab // 2):
            project_xbc(c2)
            conv_slab(2 * c2 - 2)
            conv_slab(2 * c2 - 1)
            yield
        glu()
        conv_slab(nslab - 2)
        yield
        z_and_dt()
        conv_slab(nslab - 1)

    parts = [rows_part(p) for p in range(nparts)]
    while parts:
        parts = [g for g in parts if next(g, StopIteration) is not StopIteration]


def _in_proj(x, shift, scale, g, w_main, w_dt, conv_taps, k, conv_b, dt_bias, seq, tm, moe=None, gate=None):
    t, d = x.shape
    tpb = seq // tm
    n_main = w_main.shape[1]
    merge = moe is not None
    nparts = 1
    nb8 = tm // SUBLANES
    last8 = t // SUBLANES - 1
    tok = lambda i: (i, 0)
    prev = lambda i: (jnp.maximum(i * nb8 - 1, 0), 0)
    nxt = lambda i: (jnp.minimum((i + 1) * nb8, last8), 0)
    per_batch = lambda i: (i // tpb, 0, 0)
    const = lambda i: (0, 0)
    tok_d = pl.BlockSpec((tm, d), tok)
    mod = pl.BlockSpec((1, 1, d), per_batch)
    halo = lambda m: pl.BlockSpec((SUBLANES, d), m)
    mhalo = lambda m: pl.BlockSpec((SUBLANES * OUT_PITCH, LANES), m)
    outs = (jax.ShapeDtypeStruct((t, D_CONV), F32), jax.ShapeDtypeStruct((t, D_SSD), F32),
            jax.ShapeDtypeStruct((t, D_XBC), F32), jax.ShapeDtypeStruct((t, LANES), F32))
    out_specs = (pl.BlockSpec((tm, D_CONV), tok), pl.BlockSpec((tm, D_SSD), tok),
                 pl.BlockSpec((tm, D_XBC), tok), pl.BlockSpec((tm, LANES), tok))
    args = (x, x, x)
    in_specs = [halo(prev), tok_d, halo(nxt)]
    if merge:
        args += (moe, moe, moe, gate)
        in_specs += [mhalo(prev), pl.BlockSpec((tm * OUT_PITCH, LANES), tok), mhalo(nxt), mod]
        outs = (jax.ShapeDtypeStruct((t, d), F32),) + outs
        out_specs = (tok_d,) + out_specs
    return pl.pallas_call(
        functools.partial(_in_kernel, merge=merge, tm=tm, tpb=tpb, k=k),
        out_shape=outs,
        grid=(t // tm,),
        in_specs=in_specs + [mod, mod, pl.BlockSpec((1, d), const),
                             pl.BlockSpec((d, n_main), const), pl.BlockSpec((d, LANES), const),
                             pl.BlockSpec((SUBLANES, D_XBC), const), pl.BlockSpec((1, D_XBC), const),
                             pl.BlockSpec((1, LANES), const)],
        out_specs=out_specs,
        scratch_shapes=[pltpu.VMEM((nparts, D_XBC // LANES, tm // nparts + 2 * SUBLANES, LANES), F32),
                        pltpu.VMEM((nparts, D_XBC // LANES, tm // nparts, LANES), F32)],
        compiler_params=_cparams("arbitrary"),
        name="in_proj",
    )(*args, shift, scale, g, w_main, w_dt, conv_taps, conv_b.reshape(1, D_XBC), dt_bias)


def _segconv_kernel(v_ref, w_ref, b_ref, o_ref, scr, *, seg, nseg, k, cb):
    zeros = jnp.zeros((CONV_PAD, cb), F32)
    scr[0:CONV_PAD, :] = zeros
    scr[CONV_PAD + seg:2 * CONV_PAD + seg, :] = zeros

    def body(s, carry):
        base = pl.multiple_of(s * seg, seg)
        scr[CONV_PAD:CONV_PAD + seg, :] = v_ref[pl.ds(base, seg), :]
        for c in range(cb // LANES):
            cs = slice(c * LANES, (c + 1) * LANES)
            acc = jnp.broadcast_to(b_ref[:, cs], (seg, LANES))
            for j in range(k):
                acc = acc + w_ref[j:j + 1, cs] * scr[pl.ds(CONV_PAD + j - k // 2, seg), cs]
            o_ref[pl.ds(base, seg), cs] = acc
        return carry

    lax.fori_loop(0, nseg, body, 0)


def _seg_conv(v, w_pad, b, seg, cb, tm):
    t = v.shape[0]
    k = 31
    tok = lambda i: (i, 0)
    const = lambda i: (0, 0)
    return pl.pallas_call(
        functools.partial(_segconv_kernel, seg=seg, nseg=tm // seg, k=k, cb=cb),
        out_shape=jax.ShapeDtypeStruct((t, cb), F32),
        grid=(t // tm,),
        in_specs=[pl.BlockSpec((tm, cb), tok), pl.BlockSpec((w_pad.shape[0], cb), const),
                  pl.BlockSpec((1, cb), const)],
        out_specs=pl.BlockSpec((tm, cb), tok),
        scratch_shapes=[pltpu.VMEM((seg + 2 * CONV_PAD, cb), F32)],
        compiler_params=_cparams("arbitrary"),
        name="seg_conv",
    )(v, w_pad, b)


def _phase_conv_segments(v_ref, w_ref, b_ref, o_ref, seg0, nseg, k, cs=slice(0, LANES), unroll=False):
    seg = SUBLANES * SUBLANES
    sub = lax.broadcasted_iota(jnp.int32, (SUBLANES, LANES), 0)

    def body(s, carry):
        base = s * seg if unroll else pl.multiple_of(s * seg, seg)
        x = [v_ref[pl.ds(base + j, SUBLANES, stride=SUBLANES), :] for j in range(SUBLANES)]
        shifted = {}

        def tap(q, c):
            if c == 0:
                return x[q]
            if (q, c) not in shifted:
                moved = pltpu.roll(x[q], (-c) % SUBLANES, axis=0)
                shifted[q, c] = jnp.where((sub + c >= 0) & (sub + c < SUBLANES), moved, 0.0)
            return shifted[q, c]

        for j in range(SUBLANES):
            acc = jnp.broadcast_to(b_ref[:, cs], (SUBLANES, LANES))
            for t in range(k):
                off = j + t - k // 2
                acc = acc + w_ref[t:t + 1, cs] * tap(off % SUBLANES, off // SUBLANES)
            o_ref[pl.ds(base + j, SUBLANES, stride=SUBLANES), :] = acc
        return carry

    if unroll:
        for s in range(seg0, seg0 + nseg):
            body(s, 0)
    else:
        lax.fori_loop(seg0, seg0 + nseg, body, 0)


def _rowconv_kernel(v_ref, w_ref, b_ref, o_ref, scr, *, width, rows, k):
    half = k // 2
    pad = jnp.zeros((half * width, LANES), F32)
    scr[0:half * width, :] = pad
    scr[(half + rows) * width:(2 * half + rows) * width, :] = pad
    scr[half * width:(half + rows) * width, :] = v_ref[0]

    def body(i, carry):
        r = i * ROWS_PER_ITER
        accs = [jnp.broadcast_to(b_ref[...], (width, LANES)) for _ in range(ROWS_PER_ITER)]
        for j in range(k + ROWS_PER_ITER - 1):
            start = pl.multiple_of((r + j) * width, width)
            window = scr[pl.ds(start, width), :]
            for o in range(ROWS_PER_ITER):
                if 0 <= j - o < k:
                    accs[o] = accs[o] + w_ref[j - o:j - o + 1, :] * window
        for o in range(ROWS_PER_ITER):
            o_ref[0, pl.ds(pl.multiple_of((r + o) * width, width), width), :] = accs[o]
        return carry

    lax.fori_loop(0, rows // ROWS_PER_ITER, body, 0)


def _row_conv(v3, w_pad, b, lane_block0, nblk):
    bsz, seq, _ = v3.shape
    rows = seq // GRID_W
    k = 31
    return pl.pallas_call(
        functools.partial(_rowconv_kernel, width=GRID_W, rows=rows, k=k),
        out_shape=jax.ShapeDtypeStruct((bsz, seq, nblk * LANES), F32),
        grid=(bsz, nblk),
        in_specs=[pl.BlockSpec((1, seq, LANES), lambda bi, c: (bi, 0, lane_block0 + c)),
                  pl.BlockSpec((w_pad.shape[0], LANES), lambda bi, c: (0, lane_block0 + c)),
                  pl.BlockSpec((1, LANES), lambda bi, c: (0, lane_block0 + c))],
        out_specs=pl.BlockSpec((1, seq, LANES), lambda bi, c: (bi, 0, c)),
        scratch_shapes=[pltpu.VMEM(((rows + 2 * (k // 2)) * GRID_W, LANES), F32)],
        compiler_params=_cparams("arbitrary", "arbitrary"),
        name="row_conv",
    )(v3, w_pad, b)


def _ssd_kernel(xf_ref, xb_ref, dtf_ref, dtb_ref, a_ref, dskip_ref, h0_ref, tril_ref, triu_ref, e_ref,
                yf_ref, yb_ref, hfin_ref, s_ref, *, chunk, nchunks, bsz):
    i = pl.program_id(0)

    @pl.when(i == 0)
    def _():
        s_ref[...] = h0_ref[...]

    L = chunk
    hpg = SSD_HEADS // SSD_GROUPS
    gw = hpg * SSD_HEADDIM
    lane = lax.broadcasted_iota(jnp.int32, (L, LANES), 1)
    row_i = lax.broadcasted_iota(jnp.int32, (L, L), 0)
    col_i = lax.broadcasted_iota(jnp.int32, (L, L), 1)
    head_of_lane = lax.broadcasted_iota(jnp.int32, (L, gw), 1) // SSD_HEADDIM

    def direction(b, d, x_ref, dt_ref, y_ref):
        xall = x_ref[b]
        x = xall[:, :D_SSD]
        bm = xall[:, D_SSD:D_SSD + D_BC]
        cm = xall[:, D_SSD + D_BC:]
        dt = dt_ref[b]
        dmask = (lane >= SSD_HEADS * d) & (lane < SSD_HEADS * (d + 1))
        da = jnp.where(dmask, dt * a_ref[...], 0.0)
        tri2 = tril_ref[...] if d == 0 else triu_ref[...]
        cum = _dot(tri2, jnp.concatenate(_split_bf16(da), axis=0))
        yield
        tot = cum[L - 1:L, :] if d == 0 else cum[0:1, :]
        e_cum = jnp.where(dmask, jnp.exp(cum), 0.0)
        w_dt = jnp.where(dmask, jnp.exp(tot - cum) * dt, 0.0)
        e_tot = jnp.where(dmask[0:1], jnp.exp(tot), 0.0)
        stack = jnp.concatenate([e_cum, w_dt, jnp.broadcast_to(e_tot, (SUBLANES, LANES))], axis=0)
        ex = _dot(jnp.concatenate(_split_bf16(stack), axis=1), e_ref[...])
        yield
        e_cum_x = ex[0:L]
        w_dt_x = ex[L:2 * L]
        e_tot_x = ex[2 * L:2 * L + 1]
        src_t = (cum - jnp.log(dt)).T
        xw = (x * w_dt_x).astype(BF16)
        bm_t = bm.T
        yield
        mask = (row_i >= col_i) if d == 0 else (row_i <= col_i)
        ys = []
        for g in range(SSD_GROUPS):
            cg = cm[:, g * SSD_STATE:(g + 1) * SSD_STATE].astype(BF16)
            bg = bm[:, g * SSD_STATE:(g + 1) * SSD_STATE].astype(BF16)
            cb = lax.dot_general(cg, bg, (((1,), (1,)), ((), ())), preferred_element_type=F32).astype(BF16)
            yield
            xg = x[:, g * gw:(g + 1) * gw].astype(BF16)
            gs, xs = [], []
            for hh in range(hpg):
                c = SSD_HEADS * d + hpg * g + hh
                seg = cum[:, c:c + 1] - src_t[c:c + 1, :]
                gs.append(jnp.exp(jnp.where(mask, seg, -jnp.inf)).astype(BF16) * cb)
                xs.append(jnp.where(head_of_lane == hh, xg, jnp.zeros_like(xg)))
            y_diag = _dot(jnp.concatenate(gs, axis=1), jnp.concatenate(xs, axis=0))
            yield
            sg = s_ref[b, d, g]
            y_off = _dot(cg, sg.astype(BF16)) * e_cum_x[:, g * gw:(g + 1) * gw]
            s_ref[b, d, g] = (sg * e_tot_x[:, g * gw:(g + 1) * gw]
                              + _dot(bm_t[g * SSD_STATE:(g + 1) * SSD_STATE, :].astype(BF16),
                                     xw[:, g * gw:(g + 1) * gw]))
            ys.append(y_diag + y_off)
        y = jnp.concatenate(ys, axis=1)
        if d == 0:
            y = y + dskip_ref[...] * x
        y_ref[b] = y

    scans = [direction(b, d, refs[0], refs[1], refs[2]) for b in range(bsz)
             for d, refs in enumerate(((xf_ref, dtf_ref, yf_ref), (xb_ref, dtb_ref, yb_ref)))]
    while scans:
        scans = [s for s in scans if next(s, StopIteration) is not StopIteration]

    @pl.when(i == nchunks - 1)
    def _():
        hfin_ref[...] = s_ref[...]


def _ssd(xbc, dt, a_row, dskip_row, h0, consts, bsz, seq):
    t = xbc.shape[0]
    L = SSD_CHUNK
    n = seq // L
    tril2, triu2, expand2 = consts
    xbc3 = xbc.reshape(bsz, seq, D_XBC)
    dt3 = dt.reshape(bsz, seq, LANES)
    fwd = lambda i: (0, i, 0)
    bwd = lambda i: (0, n - 1 - i, 0)
    const = lambda i: (0, 0)
    state = pl.BlockSpec(h0.shape, lambda i: (0, 0, 0, 0, 0))
    y_f, y_b, h_fin = pl.pallas_call(
        functools.partial(_ssd_kernel, chunk=L, nchunks=n, bsz=bsz),
        out_shape=(jax.ShapeDtypeStruct((bsz, seq, D_SSD), F32), jax.ShapeDtypeStruct((bsz, seq, D_SSD), F32),
                   jax.ShapeDtypeStruct(h0.shape, F32)),
        grid=(n,),
        in_specs=[pl.BlockSpec((bsz, L, D_XBC), fwd), pl.BlockSpec((bsz, L, D_XBC), bwd),
                  pl.BlockSpec((bsz, L, LANES), fwd), pl.BlockSpec((bsz, L, LANES), bwd),
                  pl.BlockSpec((1, LANES), const), pl.BlockSpec((1, D_SSD), const), state,
                  pl.BlockSpec((L, 2 * L), const), pl.BlockSpec((L, 2 * L), const),
                  pl.BlockSpec((2 * LANES, D_SSD), const)],
        out_specs=(pl.BlockSpec((bsz, L, D_SSD), fwd), pl.BlockSpec((bsz, L, D_SSD), bwd), state),
        scratch_shapes=[pltpu.VMEM(h0.shape, F32)],
        compiler_params=_cparams("arbitrary"),
        name="ssd_scan",
    )(xbc3, xbc3, dt3, dt3, a_row, dskip_row, h0, tril2, triu2, expand2)
    return y_f.reshape(t, D_SSD), y_b.reshape(t, D_SSD), h_fin


def _ssd_consts():
    L = SSD_CHUNK
    r = jnp.arange(L)
    tril = (r[:, None] >= r[None, :]).astype(BF16)
    triu = (r[:, None] <= r[None, :]).astype(BF16)
    c = jnp.arange(LANES)[:, None]
    col = jnp.arange(D_SSD)[None, :]
    expand = ((c < 2 * SSD_HEADS) & ((c % SSD_HEADS) == col // SSD_HEADDIM)).astype(BF16)
    return (jnp.concatenate([tril, tril], axis=1), jnp.concatenate([triu, triu], axis=1),
            jnp.concatenate([expand, expand], axis=0))


def _out_kernel(*refs, cw, nparts, fuse_conv):
    if fuse_conv:
        nblk = cw // LANES
        v_refs, (cvw_ref, cvb_ref), rest, ycs = refs[:nblk], refs[nblk:nblk + 2], refs[nblk + 2:-1], refs[-1]
        seg = SUBLANES * SUBLANES

        def load_yc(rs):
            for c in range(nblk):
                _phase_conv_segments(v_refs[c], cvw_ref, cvb_ref, ycs.at[c], rs.start // seg,
                                     (rs.stop - rs.start) // seg, 31, cs=slice(c * LANES, (c + 1) * LANES),
                                     unroll=True)
            return jnp.concatenate([ycs[c, rs, :] for c in range(nblk)], axis=1)
    else:
        yc_ref, rest = refs[0], refs[1:]
        load_yc = lambda rs: yc_ref[rs, :]
    n = rest[4].shape[0] // nparts
    parts = [_out_rows(slice(p * n, (p + 1) * n), p * n, n, cw, load_yc, *rest) for p in range(nparts)]
    while parts:
        parts = [p for p in parts if next(p, StopIteration) is not StopIteration]


def _out_rows(rs, r0, n, cw, load_yc, yr_ref, yf_ref, yb_ref, z_ref, hx_ref, gt1_ref, lng_ref, lnb_ref, ng_ref,
              wout_ref, gffn_ref, sh2_ref, sc2_ref, wr_ref, br_ref, hnew_ref, text_ref, comb_ref, keys_ref):
    yc = load_yc(rs)
    yield
    yr = yr_ref[rs, :]
    nconv = yc.shape[1] + yr.shape[1]
    mu = (jnp.sum(yc, axis=-1, keepdims=True) + jnp.sum(yr, axis=-1, keepdims=True)) / nconv
    dc = yc - mu
    dr = yr - mu
    var = (jnp.sum(dc * dc, axis=-1, keepdims=True) + jnp.sum(dr * dr, axis=-1, keepdims=True)) / nconv
    inv = lax.rsqrt(var + LN_EPS)
    a_c = _silu(dc * inv * lng_ref[:, 0:cw] + lnb_ref[:, 0:cw])
    a_r = _silu(dr * inv * lng_ref[:, cw:] + lnb_ref[:, cw:])
    y = (yf_ref[rs, :] + yb_ref[rs, :]) * _silu(z_ref[rs, :])
    ssd = y * lax.rsqrt(jnp.mean(y * y, axis=-1, keepdims=True) + RMS_EPS) * ng_ref[...]
    yield
    mix = (_dot(a_c.astype(BF16), wout_ref[0:cw, :]) + _dot(a_r.astype(BF16), wout_ref[cw:D_CONV, :])
           + _dot(ssd.astype(BF16), wout_ref[D_CONV:, :]))
    yield
    h = hx_ref[rs, :] + gt1_ref[0] * mix
    hnew_ref[rs, :] = h
    t = h * lax.rsqrt(jnp.mean(h * h, axis=-1, keepdims=True) + RMS_EPS) * gffn_ref[...]
    t = t * (1.0 + sc2_ref[0]) + sh2_ref[0]
    tb = t.astype(BF16)

    logits = _dot(tb, wr_ref[...]) + br_ref[...]
    yield
    lane = lax.broadcasted_iota(jnp.int32, logits.shape, 1).astype(F32)
    big = float(LANES)
    ninf = -jnp.inf
    gl = jnp.where(lane < MOE_GROUPS, logits, ninf)
    gmax = jnp.max(gl, axis=-1, keepdims=True)
    gsel = jnp.min(jnp.where(gl == gmax, lane, big), axis=-1, keepdims=True)
    gprob = 1.0 / jnp.sum(jnp.exp(gl - gmax), axis=-1, keepdims=True)
    lo = ROUTER_LANE0 + gsel * MOE_PER_GROUP
    emask = (lane >= lo) & (lane < lo + MOE_PER_GROUP)
    el = jnp.where(emask, logits, ninf)
    e1 = jnp.max(el, axis=-1, keepdims=True)
    i1 = jnp.min(jnp.where(el == e1, lane, big), axis=-1, keepdims=True)
    el2 = jnp.where(lane == i1, ninf, el)
    e2 = jnp.max(el2, axis=-1, keepdims=True)
    i2 = jnp.min(jnp.where(el2 == e2, lane, big), axis=-1, keepdims=True)
    p2 = jnp.exp(e2 - e1)
    w1 = gprob / (1.0 + p2)
    w2 = gprob * p2 / (1.0 + p2)
    comb_ref[rs, :] = jnp.where(lane == i1, w1, 0.0) + jnp.where(lane == i2, w2, 0.0)
    first = i1 < i2
    l_lo = jnp.minimum(i1, i2) - lo
    l_hi = jnp.maximum(i1, i2) - lo
    key = gsel * PAIRS_PER_GROUP + l_lo * (7.0 - l_lo) * 0.5 + (l_hi - l_lo - 1.0)
    meta = (jnp.where(lane == META_WA, jnp.where(first, w1, w2), 0.0)
            + jnp.where(lane == META_WB, jnp.where(first, w2, w1), 0.0)
            + jnp.where(lane == META_KEY, key, 0.0))
    _store_token_major(text_ref, r0 * TEXT_PITCH, n, TEXT_PITCH, t)
    text_ref[pl.ds(r0 * TEXT_PITCH + TOKEN_ROWS, n, stride=TEXT_PITCH), :] = meta
    keys_ref[:, rs] = meta.T[0:SUBLANES, :]


def _out_proj(yc, yr, yf, yb, z, hx, gt1, lng, lnb, ng, wout, gffn, sh2, sc2, wr, br, seq, tm, col_conv=None):
    t, d = hx.shape
    tpb = seq // tm
    tok = lambda i: (i, 0)
    per_batch = lambda i: (i // tpb, 0, 0)
    const = lambda i: (0, 0)
    row = lambda n: pl.BlockSpec((1, n), const)
    if col_conv is None:
        cw = yc.shape[1]
        conv_args, conv_specs, scratch = (yc,), [pl.BlockSpec((tm, cw), tok)], []
    else:
        v, taps, bias = col_conv
        cw = D_CONV - yr.shape[1]
        nblk = cw // LANES
        conv_args = (v,) * nblk + (taps, bias)
        conv_specs = ([pl.BlockSpec((tm, LANES), lambda i, c=c: (i, c)) for c in range(nblk)]
                      + [pl.BlockSpec((taps.shape[0], cw), const), row(cw)])
        scratch = [pltpu.VMEM((nblk, tm, LANES), F32)]
    return pl.pallas_call(
        functools.partial(_out_kernel, cw=cw, nparts=max(1, tm // 256), fuse_conv=col_conv is not None),
        out_shape=(jax.ShapeDtypeStruct((t, d), F32), jax.ShapeDtypeStruct((t * TEXT_PITCH, LANES), F32),
                   jax.ShapeDtypeStruct((t, LANES), F32), jax.ShapeDtypeStruct((SUBLANES, t), F32)),
        grid=(t // tm,),
        scratch_shapes=scratch,
        in_specs=conv_specs + [pl.BlockSpec((tm, D_CONV - cw), tok),
                  pl.BlockSpec((tm, D_SSD), tok), pl.BlockSpec((tm, D_SSD), tok), pl.BlockSpec((tm, D_SSD), tok),
                  pl.BlockSpec((tm, d), tok), pl.BlockSpec((1, 1, d), per_batch),
                  row(D_CONV), row(D_CONV), row(D_SSD),
                  pl.BlockSpec((D_CONV + D_SSD, d), const), row(d),
                  pl.BlockSpec((1, 1, d), per_batch), pl.BlockSpec((1, 1, d), per_batch),
                  pl.BlockSpec((d, LANES), const), row(LANES)],
        out_specs=(pl.BlockSpec((tm, d), tok), pl.BlockSpec((tm * TEXT_PITCH, LANES), tok),
                   pl.BlockSpec((tm, LANES), tok), pl.BlockSpec((SUBLANES, tm), lambda i: (0, i))),
        compiler_params=_cparams("arbitrary"),
        name="out_proj",
    )(*conv_args, yr, yf, yb, z, hx, gt1, lng, lnb, ng, wout, gffn, sh2, sc2, wr, br)


def _moe_kernel(t_ref, comb_ref, hx_ref, gt2_ref, wg_ref, wu_ref, wd_ref, o_ref, acc_ref):
    e = pl.program_id(1)

    @pl.when(e == 0)
    def _():
        acc_ref[...] = jnp.zeros_like(acc_ref)

    t = t_ref[...].astype(BF16)
    comb = comb_ref[...]
    lane = lax.broadcasted_iota(jnp.int32, comb.shape, 1)
    cw = jnp.sum(jnp.where(lane == e + ROUTER_LANE0, comb, 0.0), axis=-1, keepdims=True)
    hid = _silu(_dot(t, wg_ref[0])) * _dot(t, wu_ref[0])
    acc_ref[...] += _dot((hid * cw).astype(BF16), wd_ref[0])

    @pl.when(e == N_EXPERTS - 1)
    def _():
        o_ref[...] = hx_ref[...] + gt2_ref[0] * acc_ref[...]


def _moe(t_in, comb, hx, gt2, wg, wu, wd, seq, tm):
    t, d = hx.shape
    ff = wg.shape[2]
    tpb = seq // tm
    tok = lambda i, e: (i, 0)
    per_batch = lambda i, e: (i // tpb, 0, 0)
    return pl.pallas_call(
        _moe_kernel,
        out_shape=jax.ShapeDtypeStruct((t, d), F32),
        grid=(t // tm, N_EXPERTS),
        in_specs=[pl.BlockSpec((tm, d), tok), pl.BlockSpec((tm, LANES), tok), pl.BlockSpec((tm, d), tok),
                  pl.BlockSpec((1, 1, d), per_batch),
                  pl.BlockSpec((1, d, ff), lambda i, e: (e, 0, 0)),
                  pl.BlockSpec((1, d, ff), lambda i, e: (e, 0, 0)),
                  pl.BlockSpec((1, ff, d), lambda i, e: (e, 0, 0))],
        out_specs=pl.BlockSpec((tm, d), tok),
        scratch_shapes=[pltpu.VMEM((tm, d), F32)],
        compiler_params=_cparams("arbitrary", "arbitrary"),
        name="moe_dense",
    )(t_in, comb, hx, gt2, wg, wu, wd)


def _moe_sparse_kernel(ea_ref, eb_ref, nvalid_ref, idx_ref, idx_next_ref, text_hbm, *rest, rows, nsteps):
    weights = (rest[0:6], rest[6:12])
    out_hbm, gbuf, obuf, gsem, ssem = rest[12:]
    i = pl.program_id(0)

    def gather_copy(tok, r, s):
        return pltpu.make_async_copy(text_hbm.at[pl.ds(tok * TEXT_PITCH, TEXT_PITCH), :],
                                     gbuf.at[s, pl.ds(r * TEXT_PITCH, TEXT_PITCH), :], gsem.at[s])

    def scatter_copy(tok, r, s):
        return pltpu.make_async_copy(obuf.at[s, pl.ds(r * OUT_PITCH, OUT_PITCH), :],
                                     out_hbm.at[pl.ds(tok * OUT_PITCH, OUT_PITCH), :], ssem.at[s])

    def scatter_wait(n, s):
        @pl.when(n == rows)
        def _():
            for r in range(rows):
                scatter_copy(0, r, s).wait()

        @pl.when(jnp.logical_and(n > 0, n < rows))
        def _():
            def body(r, carry):
                scatter_copy(0, r, s).wait()
                return carry
            lax.fori_loop(0, n, body, 0)

    @pl.when(i == 0)
    def _():
        for r in range(rows):
            gather_copy(idx_ref[0, 0, r], r, 0).start()

    for s in range(2):
        tile = 2 * i + s
        wga_ref, wua_ref, wda_ref, wgb_ref, wub_ref, wdb_ref = weights[s]
        nxt_ref, nxt_half = (idx_ref, 1) if s == 0 else (idx_next_ref, 0)
        n_here = nvalid_ref[tile]

        @pl.when(jnp.logical_or(tile == 0, nvalid_ref[jnp.maximum(tile - 1, 0)] > 0))
        def _():
            for r in range(rows):
                gather_copy(0, r, s).wait()

        @pl.when(tile >= 2)
        def _():
            scatter_wait(nvalid_ref[jnp.maximum(tile - 2, 0)], s)

        @pl.when(n_here > 0)
        def _():
            for r in range(rows):
                gather_copy(nxt_ref[0, nxt_half, r], r, 1 - s).start()
            gs = gbuf.at[s]
            t = _load_token_major(gs, rows, TEXT_PITCH, BF16)
            meta = gs[pl.ds(TOKEN_ROWS, rows, stride=TEXT_PITCH), :]
            wa = meta[:, META_WA:META_WA + 1]
            wb = meta[:, META_WB:META_WB + 1]
            ha = _silu(_dot(t, wga_ref[0])) * _dot(t, wua_ref[0]) * wa
            hb = _silu(_dot(t, wgb_ref[0])) * _dot(t, wub_ref[0]) * wb
            out = _dot(ha.astype(BF16), wda_ref[0]) + _dot(hb.astype(BF16), wdb_ref[0])
            _store_token_major(obuf.at[s], 0, rows, OUT_PITCH, out)

            @pl.when(n_here == rows)
            def _():
                for r in range(rows):
                    scatter_copy(idx_ref[0, s, r], r, s).start()

            @pl.when(n_here < rows)
            def _():
                def body(r, carry):
                    scatter_copy(idx_ref[0, s, r], r, s).start()
                    return carry
                lax.fori_loop(0, n_here, body, 0)

    @pl.when(i == nsteps - 1)
    def _():
        @pl.when(nvalid_ref[2 * i + 1] > 0)
        def _():
            for r in range(rows):
                gather_copy(0, r, 0).wait()
        scatter_wait(nvalid_ref[2 * i], 0)
        scatter_wait(nvalid_ref[2 * i + 1], 1)


def _route_plan(keys, rows):
    t = keys.shape[0]
    ntiles = 2 * (((t + N_BUCKETS * (rows - 1)) // rows + 1) // 2)
    kk = jnp.arange(N_BUCKETS, dtype=jnp.int32)
    cnt = jnp.sum((keys[None, :] == kk[:, None]).astype(jnp.int32), axis=1)
    off = jnp.cumsum(cnt) - cnt
    tiles_k = (cnt + rows - 1) // rows
    tile_end = jnp.cumsum(tiles_k)
    tile_off = tile_end - tiles_k
    order = jnp.argsort(keys, stable=True).astype(jnp.int32)
    tile = jnp.arange(ntiles, dtype=jnp.int32)
    in_k = (tile[:, None] >= tile_off[None, :]) & ((tile[:, None] < tile_end[None, :]) | (kk[None, :] == N_BUCKETS - 1))
    in_k = in_k & ((tiles_k[None, :] > 0) | (kk[None, :] == N_BUCKETS - 1))
    pick = lambda v: jnp.sum(jnp.where(in_k, v[None, :], 0), axis=1)
    k_of_tile = pick(kk)
    tile_in_k = tile - pick(tile_off)
    nvalid = jnp.clip(pick(cnt) - tile_in_k * rows, 0, rows)
    nvalid = jnp.where(tile < tile_end[-1], nvalid, 0).astype(jnp.int32)
    j = jnp.arange(rows, dtype=jnp.int32)
    pos = pick(off)[:, None] + tile_in_k[:, None] * rows + j[None, :]
    tok = order[jnp.clip(pos, 0, t - 1)]
    valid = j[None, :] < nvalid[:, None]
    gidx = jnp.where(valid, tok, 0).astype(jnp.int32).reshape(ntiles // 2, 2, rows)
    g = k_of_tile // PAIRS_PER_GROUP
    pair = k_of_tile % PAIRS_PER_GROUP
    ge3 = (pair >= 3).astype(jnp.int32)
    ge5 = (pair >= 5).astype(jnp.int32)
    ea = (g * MOE_PER_GROUP + ge3 + ge5).astype(jnp.int32)
    eb = (g * MOE_PER_GROUP + pair + 1 - 2 * ge3 - ge5).astype(jnp.int32)
    return ea, eb, nvalid, gidx


def _moe_sparse(text, keys, wg, wu, wd):
    t = keys.shape[0]
    d, ff = wg.shape[1], wg.shape[2]
    rows = MOE_ROWS
    ea, eb, nvalid, gidx = _route_plan(keys, rows)
    nsteps = gidx.shape[0]
    cur = lambda i, *_: (i, 0, 0)
    nxt = lambda i, *_: (jnp.minimum(i + 1, nsteps - 1), 0, 0)
    idx_spec = lambda m: pl.BlockSpec((1, 2, rows), m, memory_space=pltpu.SMEM)

    def weight_specs(half):
        up = lambda e: pl.BlockSpec((1, d, ff), lambda i, ea, eb, nv: ((ea, eb)[e][2 * i + half], 0, 0))
        dn = lambda e: pl.BlockSpec((1, ff, d), lambda i, ea, eb, nv: ((ea, eb)[e][2 * i + half], 0, 0))
        return [up(0), up(0), dn(0), up(1), up(1), dn(1)]

    grid_spec = pltpu.PrefetchScalarGridSpec(
        num_scalar_prefetch=3,
        grid=(nsteps,),
        in_specs=[idx_spec(cur), idx_spec(nxt), pl.BlockSpec(memory_space=pl.ANY)]
                 + weight_specs(0) + weight_specs(1),
        out_specs=pl.BlockSpec(memory_space=pl.ANY),
        scratch_shapes=[pltpu.VMEM((2, rows * TEXT_PITCH, LANES), F32),
                        pltpu.VMEM((2, rows * OUT_PITCH, LANES), F32),
                        pltpu.SemaphoreType.DMA((2,)), pltpu.SemaphoreType.DMA((2,))])
    return pl.pallas_call(
        functools.partial(_moe_sparse_kernel, rows=rows, nsteps=nsteps),
        out_shape=jax.ShapeDtypeStruct((t * OUT_PITCH, LANES), F32),
        grid_spec=grid_spec,
        compiler_params=_cparams("arbitrary"),
        name="moe_sparse",
    )(ea, eb, nvalid, gidx, gidx, text, *([wg, wu, wd] * 4))


def _store_token_major(ref, row0, n, pitch, x):
    for c in range(TOKEN_ROWS):
        ref[pl.ds(row0 + c, n, stride=pitch), :] = x[:, c * LANES:(c + 1) * LANES]


def _load_token_major(ref, n, pitch, dtype, row0=0):
    return jnp.concatenate([ref[pl.ds(row0 + c, n, stride=pitch), :].astype(dtype) for c in range(TOKEN_ROWS)],
                           axis=1)


def _final_kernel(h_ref, moe_ref, gt_ref, g_ref, o_ref):
    h = h_ref[...] + gt_ref[0] * _load_token_major(moe_ref, h_ref.shape[0], OUT_PITCH, F32)
    o_ref[...] = h * lax.rsqrt(jnp.mean(h * h, axis=-1, keepdims=True) + RMS_EPS) * g_ref[...]


def _final_norm(h, moe, gate, g, seq, tm):
    t, d = h.shape
    tpb = seq // tm
    tok = pl.BlockSpec((tm, d), lambda i: (i, 0))
    return pl.pallas_call(
        _final_kernel,
        out_shape=jax.ShapeDtypeStruct((t, d), F32),
        grid=(t // tm,),
        in_specs=[tok, pl.BlockSpec((tm * OUT_PITCH, LANES), lambda i: (i, 0)),
                  pl.BlockSpec((1, 1, d), lambda i: (i // tpb, 0, 0)),
                  pl.BlockSpec((1, d), lambda i: (0, 0))],
        out_specs=tok,
        compiler_params=_cparams("arbitrary"),
        name="final_norm",
    )(h, moe, gate, g)


def kernel(x, c, ctx, c_ctx, w_ada, b_ada, g_mix, g_ffn, w_in, conv_w, conv_b, conv_ln_g, conv_ln_b,
           ssd_conv_w, ssd_conv_b, dt_bias, a_log, d_skip, ssd_norm_g, w_out, w_router_group,
           b_router_group, w_router_expert, b_router_expert, w_gate, w_up, w_down, g_final):
    bsz, seq, d = x.shape
    lc = ctx.shape[1]
    depth = w_in.shape[0]
    lo = 2 * D_CONV + D_SSD
    n_main = lo + D_XBC
    tm_x, tm_c = 512, lc
    consts = _ssd_consts()
    gw = (SSD_HEADS // SSD_GROUPS) * SSD_HEADDIM
    zero_state = jnp.zeros((bsz, 2, SSD_GROUPS, SSD_STATE, gw), F32)

    h_x = x.reshape(bsz * seq, d)
    h_c = ctx.reshape(bsz * lc, d)
    cvecs = jnp.zeros((SUBLANES, d), F32).at[:bsz].set(c).at[bsz].set(c_ctx)
    row = lambda v: v.reshape(1, -1)
    pending = None

    mods_all = _ada(cvecs, w_ada, b_ada).reshape(depth, SUBLANES, 6, d)

    pad_lanes = lambda a, n: jnp.pad(a, [(0, 0)] * (a.ndim - 1) + [(0, n - a.shape[-1])])
    nh2 = 2 * SSD_HEADS
    w_dt_all = pad_lanes(w_in[:, :, n_main:], LANES).astype(BF16)
    dtb_all = pad_lanes(dt_bias.reshape(depth, 1, nh2), LANES)
    a_all = pad_lanes(-jnp.exp(a_log.astype(F32)).reshape(depth, 1, nh2), LANES)
    dskip_all = jnp.repeat(d_skip, SSD_HEADDIM, axis=1).reshape(depth, 1, D_SSD)
    convw_all = jnp.pad(conv_w, ((0, 0), (0, 32 - conv_w.shape[1]), (0, 0)))
    sconvw_all = jnp.pad(ssd_conv_w, ((0, 0), (0, SUBLANES - ssd_conv_w.shape[1]), (0, 0)))
    wr_all = pad_lanes(jnp.concatenate([w_router_group, w_router_expert], axis=2), LANES).astype(BF16)
    br_all = pad_lanes(jnp.concatenate([b_router_group, b_router_expert], axis=1), LANES).reshape(depth, 1, LANES)

    for l in range(depth):
        last = l == depth - 1
        mods = mods_all[l]
        mx = [mods[:bsz, j][:, None, :] for j in range(6)]
        mc = [jnp.broadcast_to(mods[bsz, j][None, None, :], (bsz, 1, d)) for j in range(6)]
        w_main = w_in[l][:, :n_main].astype(BF16)
        w_dt, dtb_row, a_row, dskip_row = w_dt_all[l], dtb_all[l], a_all[l], dskip_all[l]
        convw_pad, wr, br = convw_all[l], wr_all[l], br_all[l]
        wout_b = w_out[l].astype(BF16)
        wg_b, wu_b, wd_b = w_gate[l].astype(BF16), w_up[l].astype(BF16), w_down[l].astype(BF16)

        def out_proj(h, m, s, tm, yc, yr, yf, yb, z, col_conv=None):
            return _out_proj(yc, yr, yf, yb, z, h, m[2], row(conv_ln_g[l]), row(conv_ln_b[l]),
                             row(ssd_norm_g[l]), wout_b, row(g_ffn[l]), m[3], m[4], wr, br, s, tm,
                             col_conv=col_conv)

        in_w = (row(g_mix[l]), w_main, w_dt, sconvw_all[l], ssd_conv_w.shape[1], ssd_conv_b[l], dtb_row)
        v_c, z_c, xbc_c, dt_c = _in_proj(h_c, mc[0], mc[1], *in_w, lc, tm_c)
        yf_c, yb_c, h_ctx = _ssd(xbc_c, dt_c, a_row, dskip_row, zero_state, consts, bsz, lc)

        if pending is None:
            v, z, xbc, dt = _in_proj(h_x, mx[0], mx[1], *in_w, seq, tm_x)
        else:
            h_x, v, z, xbc, dt = _in_proj(h_x, mx[0], mx[1], *in_w, seq, tm_x, moe=pending[0], gate=pending[1])
        yf, yb, _ = _ssd(xbc, dt, a_row, dskip_row, h_ctx, consts, bsz, seq)
        half = D_CONV // 2
        yr = _row_conv(v.reshape(bsz, seq, D_CONV), convw_pad, row(conv_b[l]), half // LANES,
                       half // LANES).reshape(bsz * seq, half)
        h_x, text, _, keys_t = out_proj(h_x, mx, seq, tm_x, None, yr, yf, yb, z,
                                        col_conv=(v, convw_pad, row(conv_b[l])))
        moe_out = _moe_sparse(text, keys_t[META_KEY].astype(jnp.int32), wg_b, wu_b, wd_b)
        pending = (moe_out, mx[5])

        if not last:
            y_all = _seg_conv(v_c, convw_pad, row(conv_b[l]), lc, D_CONV, tm_c)
            h_c, text_c, comb_c, _ = out_proj(h_c, mc, lc, tm_c, y_all[:, :half], y_all[:, half:], yf_c, yb_c, z_c)
            t_c = text_c.reshape(bsz * lc, TEXT_PITCH, LANES)[:, :TOKEN_ROWS].reshape(bsz * lc, d)
            h_c = _moe(t_c, comb_c, h_c, mc[5], wg_b, wu_b, wd_b, bsz * lc, bsz * lc)

    out = _final_norm(h_x, pending[0], pending[1], row(g_final), seq, 2 * tm_x)
    return out.reshape(bsz, seq, d).astype(x.dtype)
```

```python
import functools

import jax
import jax.numpy as jnp
from jax import lax
from jax.experimental import pallas as pl
from jax.experimental.pallas import tpu as pltpu

F32 = jnp.float32
BF16 = jnp.bfloat16

RMS_EPS = 1e-6
LN_EPS = 1e-5

GRID_W = 64
D_CONV = 512
D_SSD = 512
SSD_HEADS = 8
SSD_HEADDIM = 64
SSD_GROUPS = 2
SSD_STATE = 128
D_BC = SSD_GROUPS * SSD_STATE
D_XBC = D_SSD + 2 * D_BC
MOE_GROUPS = 4
MOE_PER_GROUP = 4
N_EXPERTS = MOE_GROUPS * MOE_PER_GROUP

LANES = 128
SUBLANES = 8
CONV_PAD = 16
SSD_CHUNK = 128
ROUTER_LANE0 = MOE_GROUPS
PAIRS_PER_GROUP = MOE_PER_GROUP * (MOE_PER_GROUP - 1) // 2
N_BUCKETS = MOE_GROUPS * PAIRS_PER_GROUP
META_WA, META_WB, META_KEY = 0, 1, 2
MOE_ROWS = 256
ROWS_PER_ITER = 4
TOKEN_ROWS = 8
TEXT_PITCH = TOKEN_ROWS + 1
OUT_PITCH = TOKEN_ROWS
VMEM_LIMIT = 56 * 1024 * 1024


def _cparams(*sem):
    return pltpu.CompilerParams(dimension_semantics=sem, vmem_limit_bytes=VMEM_LIMIT)


def _dot(a, b):
    return jnp.dot(a, b, preferred_element_type=F32)


def _silu(x):
    return x * jax.nn.sigmoid(x)


def _split_bf16(x):
    hi = x.astype(BF16)
    lo = (x - hi.astype(F32)).astype(BF16)
    return hi, lo


def _ada_kernel(c_ref, w_ref, b_ref, o_ref):
    c = c_ref[...]
    o_ref[0] = jnp.dot(_silu(c), w_ref[0], preferred_element_type=F32,
                       precision=lax.Precision.HIGHEST) + b_ref[0]


def _ada(cvecs, w_ada, b_ada):
    depth, d, n = w_ada.shape
    bn = 1536
    rows = cvecs.shape[0]
    return pl.pallas_call(
        _ada_kernel,
        out_shape=jax.ShapeDtypeStruct((depth, rows, n), F32),
        grid=(depth, n // bn),
        in_specs=[pl.BlockSpec((rows, d), lambda l, j: (0, 0)),
                  pl.BlockSpec((1, d, bn), lambda l, j: (l, 0, j)),
                  pl.BlockSpec((1, 1, bn), lambda l, j: (l, 0, j))],
        out_specs=pl.BlockSpec((1, rows, bn), lambda l, j: (l, 0, j)),
        compiler_params=_cparams("arbitrary", "arbitrary"),
        name="ada",
    )(cvecs, w_ada, b_ada.reshape(depth, 1, n))


def _in_kernel(*refs, merge, tm, tpb, k):
    if merge:
        (xp_ref, x_ref, xn_ref, mp_ref, m_ref, mn_ref, gt_ref, sh_ref, sc_ref, g_ref, w_ref, wdt_ref,
         cw_ref, cb_ref, dtb_ref, h_ref, v_ref, z_ref, xbc_ref, dt_ref, scr, oscr) = refs
    else:
        (xp_ref, x_ref, xn_ref, sh_ref, sc_ref, g_ref, w_ref, wdt_ref,
         cw_ref, cb_ref, dtb_ref, v_ref, z_ref, xbc_ref, dt_ref, scr, oscr) = refs
        mp_ref = m_ref = mn_ref = None
    i = pl.program_id(0)
    first = (i % tpb) == 0
    last = (i % tpb) == tpb - 1
    nparts = scr.shape[0]
    n = tm // nparts
    nslab = D_XBC // LANES
    ng = n // SUBLANES

    def stream(xr, mr, r0, rows):
        x = xr[r0:r0 + rows, :]
        if merge:
            x = x + gt_ref[0] * _load_token_major(mr, rows, OUT_PITCH, F32, row0=r0 * OUT_PITCH)
        return x

    def rows_part(p):
        lo, hi = p * n, (p + 1) * n
        x_mid = stream(x_ref, m_ref, lo, n)
        if merge:
            h_ref[lo:hi, :] = x_mid
        before = stream(xp_ref, mp_ref, 0, SUBLANES) if p == 0 else stream(x_ref, m_ref, lo - SUBLANES, SUBLANES)
        after = (stream(xn_ref, mn_ref, 0, SUBLANES) if p == nparts - 1
                 else stream(x_ref, m_ref, hi, SUBLANES))
        x = jnp.concatenate([before, x_mid, after], axis=0)
        ms = jnp.mean(x * x, axis=-1, keepdims=True)
        h = x * lax.rsqrt(ms + RMS_EPS) * g_ref[...]
        h = (h * (1.0 + sc_ref[0]) + sh_ref[0]).astype(BF16)
        mid = slice(SUBLANES, SUBLANES + n)
        yield
        xbc0 = 2 * D_CONV + D_SSD

        def project_xbc(c2):
            xe = _dot(h, w_ref[:, xbc0 + 2 * c2 * LANES:xbc0 + (2 * c2 + 2) * LANES])
            for c in (2 * c2, 2 * c2 + 1):
                cs = slice((c - 2 * c2) * LANES, (c - 2 * c2 + 1) * LANES)
                head = xe[0:SUBLANES, cs]
                tail = xe[SUBLANES + n:, cs]
                scr[p, c, 0:SUBLANES, :] = jnp.where(first, 0.0, head) if p == 0 else head
                scr[p, c, mid, :] = xe[mid, cs]
                scr[p, c, SUBLANES + n:2 * SUBLANES + n, :] = (jnp.where(last, 0.0, tail) if p == nparts - 1
                                                                else tail)

        def conv_slab(c):
            cs = slice(c * LANES, (c + 1) * LANES)
            slab = scr.at[p, c]
            nb = 2 if ng % (2 * SUBLANES) == 0 else 1
            gb = ng // nb
            for blk in range(nb):
                loaded = {}

                def phase(q, grp):
                    if (q, grp) not in loaded:
                        start = SUBLANES * (1 + grp + blk * gb) + q
                        loaded[q, grp] = slab[pl.ds(start, gb, stride=SUBLANES), :]
                    return loaded[q, grp]

                for q in range(SUBLANES):
                    acc = jnp.broadcast_to(cb_ref[:, cs], (gb, LANES))
                    for j in range(k):
                        off = q + j - k // 2
                        acc = acc + cw_ref[j:j + 1, cs] * phase(off % SUBLANES, off // SUBLANES)
                    oscr[p, c, pl.ds(SUBLANES * blk * gb + q, gb, stride=SUBLANES), :] = _silu(acc)
            xbc_ref[lo:hi, cs] = oscr[p, c]

        def glu():
            u = _dot(h, w_ref[:, 0:D_CONV])[mid]
            gate = _dot(h, w_ref[:, D_CONV:2 * D_CONV])[mid]
            v_ref[lo:hi, :] = u * jax.nn.sigmoid(gate)

        def z_and_dt():
            z_ref[lo:hi, :] = _dot(h, w_ref[:, 2 * D_CONV:2 * D_CONV + D_SSD])[mid]
            xdt = _dot(h, wdt_ref[...])[mid] + dtb_ref[...]
            dt_ref[lo:hi, :] = jnp.maximum(xdt, 0.0) + jnp.log1p(jnp.exp(-jnp.abs(xdt)))

        project_xbc(0)
        for c2 in range(1, nslab // 2):
            project_xbc(c2)
            conv_slab(2 * c2 - 2)
            conv_slab(2 * c2 - 1)
            yield
        glu()
        conv_slab(nslab - 2)
        yield
        z_and_dt()
        conv_slab(nslab - 1)

    parts = [rows_part(p) for p in range(nparts)]
    while parts:
        parts = [g for g in parts if next(g, StopIteration) is not StopIteration]


def _in_proj(x, shift, scale, g, w_main, w_dt, conv_taps, k, conv_b, dt_bias, seq, tm, moe=None, gate=None):
    t, d = x.shape
    tpb = seq // tm
    n_main = w_main.shape[1]
    merge = moe is not None
    nparts = 1
    nb8 = tm // SUBLANES
    last8 = t // SUBLANES - 1
    tok = lambda i: (i, 0)
    prev = lambda i: (jnp.maximum(i * nb8 - 1, 0), 0)
    nxt = lambda i: (jnp.minimum((i + 1) * nb8, last8), 0)
    per_batch = lambda i: (i // tpb, 0, 0)
    const = lambda i: (0, 0)
    tok_d = pl.BlockSpec((tm, d), tok)
    mod = pl.BlockSpec((1, 1, d), per_batch)
    halo = lambda m: pl.BlockSpec((SUBLANES, d), m)
    mhalo = lambda m: pl.BlockSpec((SUBLANES * OUT_PITCH, LANES), m)
    outs = (jax.ShapeDtypeStruct((t, D_CONV), F32), jax.ShapeDtypeStruct((t, D_SSD), F32),
            jax.ShapeDtypeStruct((t, D_XBC), F32), jax.ShapeDtypeStruct((t, LANES), F32))
    out_specs = (pl.BlockSpec((tm, D_CONV), tok), pl.BlockSpec((tm, D_SSD), tok),
                 pl.BlockSpec((tm, D_XBC), tok), pl.BlockSpec((tm, LANES), tok))
    args = (x, x, x)
    in_specs = [halo(prev), tok_d, halo(nxt)]
    if merge:
        args += (moe, moe, moe, gate)
        in_specs += [mhalo(prev), pl.BlockSpec((tm * OUT_PITCH, LANES), tok), mhalo(nxt), mod]
        outs = (jax.ShapeDtypeStruct((t, d), F32),) + outs
        out_specs = (tok_d,) + out_specs
    return pl.pallas_call(
        functools.partial(_in_kernel, merge=merge, tm=tm, tpb=tpb, k=k),
        out_shape=outs,
        grid=(t // tm,),
        in_specs=in_specs + [mod, mod, pl.BlockSpec((1, d), const),
                             pl.BlockSpec((d, n_main), const), pl.BlockSpec((d, LANES), const),
                             pl.BlockSpec((SUBLANES, D_XBC), const), pl.BlockSpec((1, D_XBC), const),
                             pl.BlockSpec((1, LANES), const)],
        out_specs=out_specs,
        scratch_shapes=[pltpu.VMEM((nparts, D_XBC // LANES, tm // nparts + 2 * SUBLANES, LANES), F32),
                        pltpu.VMEM((nparts, D_XBC // LANES, tm // nparts, LANES), F32)],
        compiler_params=_cparams("arbitrary"),
        name="in_proj",
    )(*args, shift, scale, g, w_main, w_dt, conv_taps, conv_b.reshape(1, D_XBC), dt_bias)


def _segconv_kernel(v_ref, w_ref, b_ref, o_ref, scr, *, seg, nseg, k, cb):
    zeros = jnp.zeros((CONV_PAD, cb), F32)
    scr[0:CONV_PAD, :] = zeros
    scr[CONV_PAD + seg:2 * CONV_PAD + seg, :] = zeros

    def body(s, carry):
        base = pl.multiple_of(s * seg, seg)
        scr[CONV_PAD:CONV_PAD + seg, :] = v_ref[pl.ds(base, seg), :]
        for c in range(cb // LANES):
            cs = slice(c * LANES, (c + 1) * LANES)
            acc = jnp.broadcast_to(b_ref[:, cs], (seg, LANES))
            for j in range(k):
                acc = acc + w_ref[j:j + 1, cs] * scr[pl.ds(CONV_PAD + j - k // 2, seg), cs]
            o_ref[pl.ds(base, seg), cs] = acc
        return carry

    lax.fori_loop(0, nseg, body, 0)


def _seg_conv(v, w_pad, b, seg, cb, tm):
    t = v.shape[0]
    k = 31
    tok = lambda i: (i, 0)
    const = lambda i: (0, 0)
    return pl.pallas_call(
        functools.partial(_segconv_kernel, seg=seg, nseg=tm // seg, k=k, cb=cb),
        out_shape=jax.ShapeDtypeStruct((t, cb), F32),
        grid=(t // tm,),
        in_specs=[pl.BlockSpec((tm, cb), tok), pl.BlockSpec((w_pad.shape[0], cb), const),
                  pl.BlockSpec((1, cb), const)],
        out_specs=pl.BlockSpec((tm, cb), tok),
        scratch_shapes=[pltpu.VMEM((seg + 2 * CONV_PAD, cb), F32)],
        compiler_params=_cparams("arbitrary"),
        name="seg_conv",
    )(v, w_pad, b)


def _phase_conv_segments(v_ref, w_ref, b_ref, o_ref, seg0, nseg, k, cs=slice(0, LANES), unroll=False):
    seg = SUBLANES * SUBLANES
    sub = lax.broadcasted_iota(jnp.int32, (SUBLANES, LANES), 0)

    def body(s, carry):
        base = s * seg if unroll else pl.multiple_of(s * seg, seg)
        x = [v_ref[pl.ds(base + j, SUBLANES, stride=SUBLANES), :] for j in range(SUBLANES)]
        shifted = {}

        def tap(q, c):
            if c == 0:
                return x[q]
            if (q, c) not in shifted:
                moved = pltpu.roll(x[q], (-c) % SUBLANES, axis=0)
                shifted[q, c] = jnp.where((sub + c >= 0) & (sub + c < SUBLANES), moved, 0.0)
            return shifted[q, c]

        for j in range(SUBLANES):
            acc = jnp.broadcast_to(b_ref[:, cs], (SUBLANES, LANES))
            for t in range(k):
                off = j + t - k // 2
                acc = acc + w_ref[t:t + 1, cs] * tap(off % SUBLANES, off // SUBLANES)
            o_ref[pl.ds(base + j, SUBLANES, stride=SUBLANES), :] = acc
        return carry

    if unroll:
        for s in range(seg0, seg0 + nseg):
            body(s, 0)
    else:
        lax.fori_loop(seg0, seg0 + nseg, body, 0)


def _rowconv_kernel(v_ref, w_ref, b_ref, o_ref, scr, *, width, rows, k):
    half = k // 2
    pad = jnp.zeros((half * width, LANES), F32)
    scr[0:half * width, :] = pad
    scr[(half + rows) * width:(2 * half + rows) * width, :] = pad
    scr[half * width:(half + rows) * width, :] = v_ref[0]

    def body(i, carry):
        r = i * ROWS_PER_ITER
        accs = [jnp.broadcast_to(b_ref[...], (width, LANES)) for _ in range(ROWS_PER_ITER)]
        for j in range(k + ROWS_PER_ITER - 1):
            start = pl.multiple_of((r + j) * width, width)
            window = scr[pl.ds(start, width), :]
            for o in range(ROWS_PER_ITER):
                if 0 <= j - o < k:
                    accs[o] = accs[o] + w_ref[j - o:j - o + 1, :] * window
        for o in range(ROWS_PER_ITER):
            o_ref[0, pl.ds(pl.multiple_of((r + o) * width, width), width), :] = accs[o]
        return carry

    lax.fori_loop(0, rows // ROWS_PER_ITER, body, 0)


def _row_conv(v3, w_pad, b, lane_block0, nblk):
    bsz, seq, _ = v3.shape
    rows = seq // GRID_W
    k = 31
    return pl.pallas_call(
        functools.partial(_rowconv_kernel, width=GRID_W, rows=rows, k=k),
        out_shape=jax.ShapeDtypeStruct((bsz, seq, nblk * LANES), F32),
        grid=(bsz, nblk),
        in_specs=[pl.BlockSpec((1, seq, LANES), lambda bi, c: (bi, 0, lane_block0 + c)),
                  pl.BlockSpec((w_pad.shape[0], LANES), lambda bi, c: (0, lane_block0 + c)),
                  pl.BlockSpec((1, LANES), lambda bi, c: (0, lane_block0 + c))],
        out_specs=pl.BlockSpec((1, seq, LANES), lambda bi, c: (bi, 0, c)),
        scratch_shapes=[pltpu.VMEM(((rows + 2 * (k // 2)) * GRID_W, LANES), F32)],
        compiler_params=_cparams("arbitrary", "arbitrary"),
        name="row_conv",
    )(v3, w_pad, b)


def _ssd_kernel(xf_ref, xb_ref, dtf_ref, dtb_ref, a_ref, dskip_ref, h0_ref, tril_ref, triu_ref, e_ref,
                yf_ref, yb_ref, hfin_ref, s_ref, *, chunk, nchunks, bsz):
    i = pl.program_id(0)

    @pl.when(i == 0)
    def _():
        s_ref[...] = h0_ref[...]

    L = chunk
    hpg = SSD_HEADS // SSD_GROUPS
    gw = hpg * SSD_HEADDIM
    lane = lax.broadcasted_iota(jnp.int32, (L, LANES), 1)
    row_i = lax.broadcasted_iota(jnp.int32, (L, L), 0)
    col_i = lax.broadcasted_iota(jnp.int32, (L, L), 1)
    head_of_lane = lax.broadcasted_iota(jnp.int32, (L, gw), 1) // SSD_HEADDIM

    def direction(b, d, x_ref, dt_ref, y_ref):
        xall = x_ref[b]
        x = xall[:, :D_SSD]
        bm = xall[:, D_SSD:D_SSD + D_BC]
        cm = xall[:, D_SSD + D_BC:]
        dt = dt_ref[b]
        dmask = (lane >= SSD_HEADS * d) & (lane < SSD_HEADS * (d + 1))
        da = jnp.where(dmask, dt * a_ref[...], 0.0)
        tri2 = tril_ref[...] if d == 0 else triu_ref[...]
        cum = _dot(tri2, jnp.concatenate(_split_bf16(da), axis=0))
        yield
        tot = cum[L - 1:L, :] if d == 0 else cum[0:1, :]
        e_cum = jnp.where(dmask, jnp.exp(cum), 0.0)
        w_dt = jnp.where(dmask, jnp.exp(tot - cum) * dt, 0.0)
        e_tot = jnp.where(dmask[0:1], jnp.exp(tot), 0.0)
        stack = jnp.concatenate([e_cum, w_dt, jnp.broadcast_to(e_tot, (SUBLANES, LANES))], axis=0)
        ex = _dot(jnp.concatenate(_split_bf16(stack), axis=1), e_ref[...])
        yield
        e_cum_x = ex[0:L]
        w_dt_x = ex[L:2 * L]
        e_tot_x = ex[2 * L:2 * L + 1]
        src_t = (cum - jnp.log(dt)).T
        xw = (x * w_dt_x).astype(BF16)
        bm_t = bm.T
        yield
        mask = (row_i >= col_i) if d == 0 else (row_i <= col_i)
        ys = []
        for g in range(SSD_GROUPS):
            cg = cm[:, g * SSD_STATE:(g + 1) * SSD_STATE].astype(BF16)
            bg = bm[:, g * SSD_STATE:(g + 1) * SSD_STATE].astype(BF16)
            cb = lax.dot_general(cg, bg, (((1,), (1,)), ((), ())), preferred_element_type=F32).astype(BF16)
            yield
            xg = x[:, g * gw:(g + 1) * gw].astype(BF16)
            gs, xs = [], []
            for hh in range(hpg):
                c = SSD_HEADS * d + hpg * g + hh
                seg = cum[:, c:c + 1] - src_t[c:c + 1, :]
                gs.append(jnp.exp(jnp.where(mask, seg, -jnp.inf)).astype(BF16) * cb)
                xs.append(jnp.where(head_of_lane == hh, xg, jnp.zeros_like(xg)))
            y_diag = _dot(jnp.concatenate(gs, axis=1), jnp.concatenate(xs, axis=0))
            yield
            sg = s_ref[b, d, g]
            y_off = _dot(cg, sg.astype(BF16)) * e_cum_x[:, g * gw:(g + 1) * gw]
            s_ref[b, d, g] = (sg * e_tot_x[:, g * gw:(g + 1) * gw]
                              + _dot(bm_t[g * SSD_STATE:(g + 1) * SSD_STATE, :].astype(BF16),
                                     xw[:, g * gw:(g + 1) * gw]))
            ys.append(y_diag + y_off)
        y = jnp.concatenate(ys, axis=1)
        if d == 0:
            y = y + dskip_ref[...] * x
        y_ref[b] = y

    scans = [direction(b, d, refs[0], refs[1], refs[2]) for b in range(bsz)
             for d, refs in enumerate(((xf_ref, dtf_ref, yf_ref), (xb_ref, dtb_ref, yb_ref)))]
    while scans:
        scans = [s for s in scans if next(s, StopIteration) is not StopIteration]

    @pl.when(i == nchunks - 1)
    def _():
        hfin_ref[...] = s_ref[...]


def _ssd(xbc, dt, a_row, dskip_row, h0, consts, bsz, seq):
    t = xbc.shape[0]
    L = SSD_CHUNK
    n = seq // L
    tril2, triu2, expand2 = consts
    xbc3 = xbc.reshape(bsz, seq, D_XBC)
    dt3 = dt.reshape(bsz, seq, LANES)
    fwd = lambda i: (0, i, 0)
    bwd = lambda i: (0, n - 1 - i, 0)
    const = lambda i: (0, 0)
    state = pl.BlockSpec(h0.shape, lambda i: (0, 0, 0, 0, 0))
    y_f, y_b, h_fin = pl.pallas_call(
        functools.partial(_ssd_kernel, chunk=L, nchunks=n, bsz=bsz),
        out_shape=(jax.ShapeDtypeStruct((bsz, seq, D_SSD), F32), jax.ShapeDtypeStruct((bsz, seq, D_SSD), F32),
                   jax.ShapeDtypeStruct(h0.shape, F32)),
        grid=(n,),
        in_specs=[pl.BlockSpec((bsz, L, D_XBC), fwd), pl.BlockSpec((bsz, L, D_XBC), bwd),
                  pl.BlockSpec((bsz, L, LANES), fwd), pl.BlockSpec((bsz, L, LANES), bwd),
                  pl.BlockSpec((1, LANES), const), pl.BlockSpec((1, D_SSD), const), state,
                  pl.BlockSpec((L, 2 * L), const), pl.BlockSpec((L, 2 * L), const),
                  pl.BlockSpec((2 * LANES, D_SSD), const)],
        out_specs=(pl.BlockSpec((bsz, L, D_SSD), fwd), pl.BlockSpec((bsz, L, D_SSD), bwd), state),
        scratch_shapes=[pltpu.VMEM(h0.shape, F32)],
        compiler_params=_cparams("arbitrary"),
        name="ssd_scan",
    )(xbc3, xbc3, dt3, dt3, a_row, dskip_row, h0, tril2, triu2, expand2)
    return y_f.reshape(t, D_SSD), y_b.reshape(t, D_SSD), h_fin


def _ssd_consts():
    L = SSD_CHUNK
    r = jnp.arange(L)
    tril = (r[:, None] >= r[None, :]).astype(BF16)
    triu = (r[:, None] <= r[None, :]).astype(BF16)
    c = jnp.arange(LANES)[:, None]
    col = jnp.arange(D_SSD)[None, :]
    expand = ((c < 2 * SSD_HEADS) & ((c % SSD_HEADS) == col // SSD_HEADDIM)).astype(BF16)
    return (jnp.concatenate([tril, tril], axis=1), jnp.concatenate([triu, triu], axis=1),
            jnp.concatenate([expand, expand], axis=0))


def _out_kernel(*refs, cw, nparts, fuse_conv):
    if fuse_conv:
        nblk = cw // LANES
        v_refs, (cvw_ref, cvb_ref), rest, ycs = refs[:nblk], refs[nblk:nblk + 2], refs[nblk + 2:-1], refs[-1]
        seg = SUBLANES * SUBLANES

        def load_yc(rs):
            for c in range(nblk):
                _phase_conv_segments(v_refs[c], cvw_ref, cvb_ref, ycs.at[c], rs.start // seg,
                                     (rs.stop - rs.start) // seg, 31, cs=slice(c * LANES, (c + 1) * LANES),
                                     unroll=True)
            return jnp.concatenate([ycs[c, rs, :] for c in range(nblk)], axis=1)
    else:
        yc_ref, rest = refs[0], refs[1:]
        load_yc = lambda rs: yc_ref[rs, :]
    n = rest[4].shape[0] // nparts
    parts = [_out_rows(slice(p * n, (p + 1) * n), p * n, n, cw, load_yc, *rest) for p in range(nparts)]
    while parts:
        parts = [p for p in parts if next(p, StopIteration) is not StopIteration]


def _out_rows(rs, r0, n, cw, load_yc, yr_ref, yf_ref, yb_ref, z_ref, hx_ref, gt1_ref, lng_ref, lnb_ref, ng_ref,
              wout_ref, gffn_ref, sh2_ref, sc2_ref, wr_ref, br_ref, hnew_ref, text_ref, comb_ref, keys_ref):
    yc = load_yc(rs)
    yield
    yr = yr_ref[rs, :]
    nconv = yc.shape[1] + yr.shape[1]
    mu = (jnp.sum(yc, axis=-1, keepdims=True) + jnp.sum(yr, axis=-1, keepdims=True)) / nconv
    dc = yc - mu
    dr = yr - mu
    var = (jnp.sum(dc * dc, axis=-1, keepdims=True) + jnp.sum(dr * dr, axis=-1, keepdims=True)) / nconv
    inv = lax.rsqrt(var + LN_EPS)
    a_c = _silu(dc * inv * lng_ref[:, 0:cw] + lnb_ref[:, 0:cw])
    a_r = _silu(dr * inv * lng_ref[:, cw:] + lnb_ref[:, cw:])
    y = (yf_ref[rs, :] + yb_ref[rs, :]) * _silu(z_ref[rs, :])
    ssd = y * lax.rsqrt(jnp.mean(y * y, axis=-1, keepdims=True) + RMS_EPS) * ng_ref[...]
    yield
    mix = (_dot(a_c.astype(BF16), wout_ref[0:cw, :]) + _dot(a_r.astype(BF16), wout_ref[cw:D_CONV, :])
           + _dot(ssd.astype(BF16), wout_ref[D_CONV:, :]))
    yield
    h = hx_ref[rs, :] + gt1_ref[0] * mix
    hnew_ref[rs, :] = h
    t = h * lax.rsqrt(jnp.mean(h * h, axis=-1, keepdims=True) + RMS_EPS) * gffn_ref[...]
    t = t * (1.0 + sc2_ref[0]) + sh2_ref[0]
    tb = t.astype(BF16)

    logits = _dot(tb, wr_ref[...]) + br_ref[...]
    yield
    lane = lax.broadcasted_iota(jnp.int32, logits.shape, 1).astype(F32)
    big = float(LANES)
    ninf = -jnp.inf
    gl = jnp.where(lane < MOE_GROUPS, logits, ninf)
    gmax = jnp.max(gl, axis=-1, keepdims=True)
    gsel = jnp.min(jnp.where(gl == gmax, lane, big), axis=-1, keepdims=True)
    gprob = 1.0 / jnp.sum(jnp.exp(gl - gmax), axis=-1, keepdims=True)
    lo = ROUTER_LANE0 + gsel * MOE_PER_GROUP
    emask = (lane >= lo) & (lane < lo + MOE_PER_GROUP)
    el = jnp.where(emask, logits, ninf)
    e1 = jnp.max(el, axis=-1, keepdims=True)
    i1 = jnp.min(jnp.where(el == e1, lane, big), axis=-1, keepdims=True)
    el2 = jnp.where(lane == i1, ninf, el)
    e2 = jnp.max(el2, axis=-1, keepdims=True)
    i2 = jnp.min(jnp.where(el2 == e2, lane, big), axis=-1, keepdims=True)
    p2 = jnp.exp(e2 - e1)
    w1 = gprob / (1.0 + p2)
    w2 = gprob * p2 / (1.0 + p2)
    comb_ref[rs, :] = jnp.where(lane == i1, w1, 0.0) + jnp.where(lane == i2, w2, 0.0)
    first = i1 < i2
    l_lo = jnp.minimum(i1, i2) - lo
    l_hi = jnp.maximum(i1, i2) - lo
    key = gsel * PAIRS_PER_GROUP + l_lo * (7.0 - l_lo) * 0.5 + (l_hi - l_lo - 1.0)
    meta = (jnp.where(lane == META_WA, jnp.where(first, w1, w2), 0.0)
            + jnp.where(lane == META_WB, jnp.where(first, w2, w1), 0.0)
            + jnp.where(lane == META_KEY, key, 0.0))
    _store_token_major(text_ref, r0 * TEXT_PITCH, n, TEXT_PITCH, t)
    text_ref[pl.ds(r0 * TEXT_PITCH + TOKEN_ROWS, n, stride=TEXT_PITCH), :] = meta
    keys_ref[:, rs] = meta.T[0:SUBLANES, :]


def _out_proj(yc, yr, yf, yb, z, hx, gt1, lng, lnb, ng, wout, gffn, sh2, sc2, wr, br, seq, tm, col_conv=None):
    t, d = hx.shape
    tpb = seq // tm
    tok = lambda i: (i, 0)
    per_batch = lambda i: (i // tpb, 0, 0)
    const = lambda i: (0, 0)
    row = lambda n: pl.BlockSpec((1, n), const)
    if col_conv is None:
        cw = yc.shape[1]
        conv_args, conv_specs, scratch = (yc,), [pl.BlockSpec((tm, cw), tok)], []
    else:
        v, taps, bias = col_conv
        cw = D_CONV - yr.shape[1]
        nblk = cw // LANES
        conv_args = (v,) * nblk + (taps, bias)
        conv_specs = ([pl.BlockSpec((tm, LANES), lambda i, c=c: (i, c)) for c in range(nblk)]
                      + [pl.BlockSpec((taps.shape[0], cw), const), row(cw)])
        scratch = [pltpu.VMEM((nblk, tm, LANES), F32)]
    return pl.pallas_call(
        functools.partial(_out_kernel, cw=cw, nparts=max(1, tm // 256), fuse_conv=col_conv is not None),
        out_shape=(jax.ShapeDtypeStruct((t, d), F32), jax.ShapeDtypeStruct((t * TEXT_PITCH, LANES), F32),
                   jax.ShapeDtypeStruct((t, LANES), F32), jax.ShapeDtypeStruct((SUBLANES, t), F32)),
        grid=(t // tm,),
        scratch_shapes=scratch,
        in_specs=conv_specs + [pl.BlockSpec((tm, D_CONV - cw), tok),
                  pl.BlockSpec((tm, D_SSD), tok), pl.BlockSpec((tm, D_SSD), tok), pl.BlockSpec((tm, D_SSD), tok),
                  pl.BlockSpec((tm, d), tok), pl.BlockSpec((1, 1, d), per_batch),
                  row(D_CONV), row(D_CONV), row(D_SSD),
                  pl.BlockSpec((D_CONV + D_SSD, d), const), row(d),
                  pl.BlockSpec((1, 1, d), per_batch), pl.BlockSpec((1, 1, d), per_batch),
                  pl.BlockSpec((d, LANES), const), row(LANES)],
        out_specs=(pl.BlockSpec((tm, d), tok), pl.BlockSpec((tm * TEXT_PITCH, LANES), tok),
                   pl.BlockSpec((tm, LANES), tok), pl.BlockSpec((SUBLANES, tm), lambda i: (0, i))),
        compiler_params=_cparams("arbitrary"),
        name="out_proj",
    )(*conv_args, yr, yf, yb, z, hx, gt1, lng, lnb, ng, wout, gffn, sh2, sc2, wr, br)


def _moe_kernel(t_ref, comb_ref, hx_ref, gt2_ref, wg_ref, wu_ref, wd_ref, o_ref, acc_ref):
    e = pl.program_id(1)

    @pl.when(e == 0)
    def _():
        acc_ref[...] = jnp.zeros_like(acc_ref)

    t = t_ref[...].astype(BF16)
    comb = comb_ref[...]
    lane = lax.broadcasted_iota(jnp.int32, comb.shape, 1)
    cw = jnp.sum(jnp.where(lane == e + ROUTER_LANE0, comb, 0.0), axis=-1, keepdims=True)
    hid = _silu(_dot(t, wg_ref[0])) * _dot(t, wu_ref[0])
    acc_ref[...] += _dot((hid * cw).astype(BF16), wd_ref[0])

    @pl.when(e == N_EXPERTS - 1)
    def _():
        o_ref[...] = hx_ref[...] + gt2_ref[0] * acc_ref[...]


def _moe(t_in, comb, hx, gt2, wg, wu, wd, seq, tm):
    t, d = hx.shape
    ff = wg.shape[2]
    tpb = seq // tm
    tok = lambda i, e: (i, 0)
    per_batch = lambda i, e: (i // tpb, 0, 0)
    return pl.pallas_call(
        _moe_kernel,
        out_shape=jax.ShapeDtypeStruct((t, d), F32),
        grid=(t // tm, N_EXPERTS),
        in_specs=[pl.BlockSpec((tm, d), tok), pl.BlockSpec((tm, LANES), tok), pl.BlockSpec((tm, d), tok),
                  pl.BlockSpec((1, 1, d), per_batch),
                  pl.BlockSpec((1, d, ff), lambda i, e: (e, 0, 0)),
                  pl.BlockSpec((1, d, ff), lambda i, e: (e, 0, 0)),
                  pl.BlockSpec((1, ff, d), lambda i, e: (e, 0, 0))],
        out_specs=pl.BlockSpec((tm, d), tok),
        scratch_shapes=[pltpu.VMEM((tm, d), F32)],
        compiler_params=_cparams("arbitrary", "arbitrary"),
        name="moe_dense",
    )(t_in, comb, hx, gt2, wg, wu, wd)


def _moe_sparse_kernel(ea_ref, eb_ref, nvalid_ref, idx_ref, idx_next_ref, text_hbm, *rest, rows, nsteps):
    weights = (rest[0:6], rest[6:12])
    out_hbm, gbuf, obuf, gsem, ssem = rest[12:]
    i = pl.program_id(0)

    def gather_copy(tok, r, s):
        return pltpu.make_async_copy(text_hbm.at[pl.ds(tok * TEXT_PITCH, TEXT_PITCH), :],
                                     gbuf.at[s, pl.ds(r * TEXT_PITCH, TEXT_PITCH), :], gsem.at[s])

    def scatter_copy(tok, r, s):
        return pltpu.make_async_copy(obuf.at[s, pl.ds(r * OUT_PITCH, OUT_PITCH), :],
                                     out_hbm.at[pl.ds(tok * OUT_PITCH, OUT_PITCH), :], ssem.at[s])

    def scatter_wait(n, s):
        @pl.when(n == rows)
        def _():
            for r in range(rows):
                scatter_copy(0, r, s).wait()

        @pl.when(jnp.logical_and(n > 0, n < rows))
        def _():
            def body(r, carry):
                scatter_copy(0, r, s).wait()
                return carry
            lax.fori_loop(0, n, body, 0)

    @pl.when(i == 0)
    def _():
        for r in range(rows):
            gather_copy(idx_ref[0, 0, r], r, 0).start(priority=r % 2)

    for s in range(2):
        tile = 2 * i + s
        wga_ref, wua_ref, wda_ref, wgb_ref, wub_ref, wdb_ref = weights[s]
        nxt_ref, nxt_half = (idx_ref, 1) if s == 0 else (idx_next_ref, 0)
        n_here = nvalid_ref[tile]

        @pl.when(jnp.logical_or(tile == 0, nvalid_ref[jnp.maximum(tile - 1, 0)] > 0))
        def _():
            for r in range(rows):
                gather_copy(0, r, s).wait()

        @pl.when(tile >= 2)
        def _():
            scatter_wait(nvalid_ref[jnp.maximum(tile - 2, 0)], s)

        @pl.when(n_here > 0)
        def _():
            def prefetch(lo, hi):
                for r in range(lo, hi):
                    gather_copy(nxt_ref[0, nxt_half, r], r, 1 - s).start(priority=r % 2)

            quarter = rows // 4
            prefetch(0, quarter)
            gs = gbuf.at[s]
            t = _load_token_major(gs, rows, TEXT_PITCH, BF16)
            meta = gs[pl.ds(TOKEN_ROWS, rows, stride=TEXT_PITCH), :]
            wa = meta[:, META_WA:META_WA + 1]
            wb = meta[:, META_WB:META_WB + 1]
            gate_a = _dot(t, wga_ref[0])
            prefetch(quarter, 2 * quarter)
            up_a = _dot(t, wua_ref[0])
            prefetch(2 * quarter, 3 * quarter)
            gate_b = _dot(t, wgb_ref[0])
            prefetch(3 * quarter, rows)
            up_b = _dot(t, wub_ref[0])
            ha = _silu(gate_a) * up_a * wa
            hb = _silu(gate_b) * up_b * wb
            out = _dot(ha.astype(BF16), wda_ref[0]) + _dot(hb.astype(BF16), wdb_ref[0])
            _store_token_major(obuf.at[s], 0, rows, OUT_PITCH, out)

            @pl.when(n_here == rows)
            def _():
                for r in range(rows):
                    scatter_copy(idx_ref[0, s, r], r, s).start(priority=r % 2)

            @pl.when(n_here < rows)
            def _():
                def body(r, carry):
                    scatter_copy(idx_ref[0, s, r], r, s).start()
                    return carry
                lax.fori_loop(0, n_here, body, 0)

    @pl.when(i == nsteps - 1)
    def _():
        @pl.when(nvalid_ref[2 * i + 1] > 0)
        def _():
            for r in range(rows):
                gather_copy(0, r, 0).wait()
        scatter_wait(nvalid_ref[2 * i], 0)
        scatter_wait(nvalid_ref[2 * i + 1], 1)


def _route_plan(keys, rows):
    t = keys.shape[0]
    ntiles = 2 * (((t + N_BUCKETS * (rows - 1)) // rows + 1) // 2)
    kk = jnp.arange(N_BUCKETS, dtype=jnp.int32)
    cnt = jnp.sum((keys[None, :] == kk[:, None]).astype(jnp.int32), axis=1)
    off = jnp.cumsum(cnt) - cnt
    tiles_k = (cnt + rows - 1) // rows
    tile_end = jnp.cumsum(tiles_k)
    tile_off = tile_end - tiles_k
    order = jnp.argsort(keys, stable=True).astype(jnp.int32)
    tile = jnp.arange(ntiles, dtype=jnp.int32)
    in_k = (tile[:, None] >= tile_off[None, :]) & ((tile[:, None] < tile_end[None, :]) | (kk[None, :] == N_BUCKETS - 1))
    in_k = in_k & ((tiles_k[None, :] > 0) | (kk[None, :] == N_BUCKETS - 1))
    pick = lambda v: jnp.sum(jnp.where(in_k, v[None, :], 0), axis=1)
    k_of_tile = pick(kk)
    tile_in_k = tile - pick(tile_off)
    nvalid = jnp.clip(pick(cnt) - tile_in_k * rows, 0, rows)
    nvalid = jnp.where(tile < tile_end[-1], nvalid, 0).astype(jnp.int32)
    j = jnp.arange(rows, dtype=jnp.int32)
    pos = pick(off)[:, None] + tile_in_k[:, None] * rows + j[None, :]
    tok = order[jnp.clip(pos, 0, t - 1)]
    valid = j[None, :] < nvalid[:, None]
    gidx = jnp.where(valid, tok, 0).astype(jnp.int32).reshape(ntiles // 2, 2, rows)
    g = k_of_tile // PAIRS_PER_GROUP
    pair = k_of_tile % PAIRS_PER_GROUP
    ge3 = (pair >= 3).astype(jnp.int32)
    ge5 = (pair >= 5).astype(jnp.int32)
    ea = (g * MOE_PER_GROUP + ge3 + ge5).astype(jnp.int32)
    eb = (g * MOE_PER_GROUP + pair + 1 - 2 * ge3 - ge5).astype(jnp.int32)
    return ea, eb, nvalid, gidx


def _moe_sparse(text, keys, wg, wu, wd):
    t = keys.shape[0]
    d, ff = wg.shape[1], wg.shape[2]
    rows = MOE_ROWS
    ea, eb, nvalid, gidx = _route_plan(keys, rows)
    nsteps = gidx.shape[0]
    cur = lambda i, *_: (i, 0, 0)
    nxt = lambda i, *_: (jnp.minimum(i + 1, nsteps - 1), 0, 0)
    idx_spec = lambda m: pl.BlockSpec((1, 2, rows), m, memory_space=pltpu.SMEM)

    def weight_specs(half):
        up = lambda e: pl.BlockSpec((1, d, ff), lambda i, ea, eb, nv: ((ea, eb)[e][2 * i + half], 0, 0))
        dn = lambda e: pl.BlockSpec((1, ff, d), lambda i, ea, eb, nv: ((ea, eb)[e][2 * i + half], 0, 0))
        return [up(0), up(0), dn(0), up(1), up(1), dn(1)]

    grid_spec = pltpu.PrefetchScalarGridSpec(
        num_scalar_prefetch=3,
        grid=(nsteps,),
        in_specs=[idx_spec(cur), idx_spec(nxt), pl.BlockSpec(memory_space=pl.ANY)]
                 + weight_specs(0) + weight_specs(1),
        out_specs=pl.BlockSpec(memory_space=pl.ANY),
        scratch_shapes=[pltpu.VMEM((2, rows * TEXT_PITCH, LANES), F32),
                        pltpu.VMEM((2, rows * OUT_PITCH, LANES), F32),
                        pltpu.SemaphoreType.DMA((2,)), pltpu.SemaphoreType.DMA((2,))])
    return pl.pallas_call(
        functools.partial(_moe_sparse_kernel, rows=rows, nsteps=nsteps),
        out_shape=jax.ShapeDtypeStruct((t * OUT_PITCH, LANES), F32),
        grid_spec=grid_spec,
        compiler_params=_cparams("arbitrary"),
        name="moe_sparse",
    )(ea, eb, nvalid, gidx, gidx, text, *([wg, wu, wd] * 4))


def _store_token_major(ref, row0, n, pitch, x):
    for c in range(TOKEN_ROWS):
        ref[pl.ds(row0 + c, n, stride=pitch), :] = x[:, c * LANES:(c + 1) * LANES]


def _load_token_major(ref, n, pitch, dtype, row0=0):
    return jnp.concatenate([ref[pl.ds(row0 + c, n, stride=pitch), :].astype(dtype) for c in range(TOKEN_ROWS)],
                           axis=1)


def _final_kernel(h_ref, moe_ref, gt_ref, g_ref, o_ref):
    h = h_ref[...] + gt_ref[0] * _load_token_major(moe_ref, h_ref.shape[0], OUT_PITCH, F32)
    o_ref[...] = h * lax.rsqrt(jnp.mean(h * h, axis=-1, keepdims=True) + RMS_EPS) * g_ref[...]


def _final_norm(h, moe, gate, g, seq, tm):
    t, d = h.shape
    tpb = seq // tm
    tok = pl.BlockSpec((tm, d), lambda i: (i, 0))
    return pl.pallas_call(
        _final_kernel,
        out_shape=jax.ShapeDtypeStruct((t, d), F32),
        grid=(t // tm,),
        in_specs=[tok, pl.BlockSpec((tm * OUT_PITCH, LANES), lambda i: (i, 0)),
                  pl.BlockSpec((1, 1, d), lambda i: (i // tpb, 0, 0)),
                  pl.BlockSpec((1, d), lambda i: (0, 0))],
        out_specs=tok,
        compiler_params=_cparams("arbitrary"),
        name="final_norm",
    )(h, moe, gate, g)


def kernel(x, c, ctx, c_ctx, w_ada, b_ada, g_mix, g_ffn, w_in, conv_w, conv_b, conv_ln_g, conv_ln_b,
           ssd_conv_w, ssd_conv_b, dt_bias, a_log, d_skip, ssd_norm_g, w_out, w_router_group,
           b_router_group, w_router_expert, b_router_expert, w_gate, w_up, w_down, g_final):
    bsz, seq, d = x.shape
    lc = ctx.shape[1]
    depth = w_in.shape[0]
    lo = 2 * D_CONV + D_SSD
    n_main = lo + D_XBC
    tm_x, tm_c = 512, lc
    consts = _ssd_consts()
    gw = (SSD_HEADS // SSD_GROUPS) * SSD_HEADDIM
    zero_state = jnp.zeros((bsz, 2, SSD_GROUPS, SSD_STATE, gw), F32)

    h_x = x.reshape(bsz * seq, d)
    h_c = ctx.reshape(bsz * lc, d)
    cvecs = jnp.zeros((SUBLANES, d), F32).at[:bsz].set(c).at[bsz].set(c_ctx)
    row = lambda v: v.reshape(1, -1)
    pending = None

    mods_all = _ada(cvecs, w_ada, b_ada).reshape(depth, SUBLANES, 6, d)

    pad_lanes = lambda a, n: jnp.pad(a, [(0, 0)] * (a.ndim - 1) + [(0, n - a.shape[-1])])
    nh2 = 2 * SSD_HEADS
    w_dt_all = pad_lanes(w_in[:, :, n_main:], LANES).astype(BF16)
    dtb_all = pad_lanes(dt_bias.reshape(depth, 1, nh2), LANES)
    a_all = pad_lanes(-jnp.exp(a_log.astype(F32)).reshape(depth, 1, nh2), LANES)
    dskip_all = jnp.repeat(d_skip, SSD_HEADDIM, axis=1).reshape(depth, 1, D_SSD)
    convw_all = jnp.pad(conv_w, ((0, 0), (0, 32 - conv_w.shape[1]), (0, 0)))
    sconvw_all = jnp.pad(ssd_conv_w, ((0, 0), (0, SUBLANES - ssd_conv_w.shape[1]), (0, 0)))
    wr_all = pad_lanes(jnp.concatenate([w_router_group, w_router_expert], axis=2), LANES).astype(BF16)
    br_all = pad_lanes(jnp.concatenate([b_router_group, b_router_expert], axis=1), LANES).reshape(depth, 1, LANES)

    for l in range(depth):
        last = l == depth - 1
        mods = mods_all[l]
        mx = [mods[:bsz, j][:, None, :] for j in range(6)]
        mc = [jnp.broadcast_to(mods[bsz, j][None, None, :], (bsz, 1, d)) for j in range(6)]
        w_main = w_in[l][:, :n_main].astype(BF16)
        w_dt, dtb_row, a_row, dskip_row = w_dt_all[l], dtb_all[l], a_all[l], dskip_all[l]
        convw_pad, wr, br = convw_all[l], wr_all[l], br_all[l]
        wout_b = w_out[l].astype(BF16)
        wg_b, wu_b, wd_b = w_gate[l].astype(BF16), w_up[l].astype(BF16), w_down[l].astype(BF16)

        def out_proj(h, m, s, tm, yc, yr, yf, yb, z, col_conv=None):
            return _out_proj(yc, yr, yf, yb, z, h, m[2], row(conv_ln_g[l]), row(conv_ln_b[l]),
                             row(ssd_norm_g[l]), wout_b, row(g_ffn[l]), m[3], m[4], wr, br, s, tm,
                             col_conv=col_conv)

        in_w = (row(g_mix[l]), w_main, w_dt, sconvw_all[l], ssd_conv_w.shape[1], ssd_conv_b[l], dtb_row)
        v_c, z_c, xbc_c, dt_c = _in_proj(h_c, mc[0], mc[1], *in_w, lc, tm_c)
        yf_c, yb_c, h_ctx = _ssd(xbc_c, dt_c, a_row, dskip_row, zero_state, consts, bsz, lc)

        if pending is None:
            v, z, xbc, dt = _in_proj(h_x, mx[0], mx[1], *in_w, seq, tm_x)
        else:
            h_x, v, z, xbc, dt = _in_proj(h_x, mx[0], mx[1], *in_w, seq, tm_x, moe=pending[0], gate=pending[1])
        yf, yb, _ = _ssd(xbc, dt, a_row, dskip_row, h_ctx, consts, bsz, seq)
        half = D_CONV // 2
        yr = _row_conv(v.reshape(bsz, seq, D_CONV), convw_pad, row(conv_b[l]), half // LANES,
                       half // LANES).reshape(bsz * seq, half)
        h_x, text, _, keys_t = out_proj(h_x, mx, seq, tm_x, None, yr, yf, yb, z,
                                        col_conv=(v, convw_pad, row(conv_b[l])))
        moe_out = _moe_sparse(text, keys_t[META_KEY].astype(jnp.int32), wg_b, wu_b, wd_b)
        pending = (moe_out, mx[5])

        if not last:
            y_all = _seg_conv(v_c, convw_pad, row(conv_b[l]), lc, D_CONV, tm_c)
            h_c, text_c, comb_c, _ = out_proj(h_c, mc, lc, tm_c, y_all[:, :half], y_all[:, half:], yf_c, yb_c, z_c)
            t_c = text_c.reshape(bsz * lc, TEXT_PITCH, LANES)[:, :TOKEN_ROWS].reshape(bsz * lc, d)
            h_c = _moe(t_c, comb_c, h_c, mc[5], wg_b, wu_b, wd_b, bsz * lc, bsz * lc)

    out = _final_norm(h_x, pending[0], pending[1], row(g_final), seq, tm_x)
    return out.reshape(bsz, seq, d).astype(x.dtype)
```
